```python
import jax, jax.numpy as jnp
from jax import lax
import numpy as np

D_MODEL = 1024
BATCH = 1
SEQ = 16384
DEPTH = 1
DEC_BATCH = 2
DEC_SEQ = 8192
PAST_LEN = 128

GRID_W = 64
EPS = 1e-6
HG_HEADS = 4
HG_DK = 128
HG_DV = 128
HG_CHUNK = 64
HG_KWIDTH = HG_HEADS * HG_DK
HG_WIDTH = HG_HEADS * HG_DV
NA_HEADS = 8
NA_DH = 64
NA_WIDTH = NA_HEADS * NA_DH
NA_KH = 8
NA_KW = 16
MIX_WIDTH = HG_WIDTH + NA_WIDTH
IN_WIDTH = 3 * HG_KWIDTH + 2 * HG_WIDTH + 3 * NA_WIDTH
IN_SPLITS = (HG_KWIDTH, 2 * HG_KWIDTH, 3 * HG_KWIDTH, 3 * HG_KWIDTH + HG_WIDTH,
             3 * HG_KWIDTH + 2 * HG_WIDTH, 3 * HG_KWIDTH + 2 * HG_WIDTH + NA_WIDTH,
             3 * HG_KWIDTH + 2 * HG_WIDTH + 2 * NA_WIDTH)
N_GROUPS = 4
EXPERTS_PER_GROUP = 8
N_EXPERTS = N_GROUPS * EXPERTS_PER_GROUP
TOP_K_IN_GROUP = 2
EXPERT_FF = 512
MOE_BLOCK = 128

kernel_name = 'hymba_hgrn2_natten_hmoe_encoder'


def rms_norm(x, w):
    xf = x.astype(jnp.float32)
    y = xf * lax.rsqrt(jnp.mean(xf * xf, axis=-1, keepdims=True) + EPS)
    return (y * w.astype(jnp.float32)).astype(x.dtype)


def head_rms(a, w):
    af = a.astype(jnp.float32)
    return af * lax.rsqrt(jnp.mean(af * af, axis=-1, keepdims=True) + EPS) * w.astype(jnp.float32)


def lower_bound(logits, l):
    return jnp.cumsum(jax.nn.softmax(logits.astype(jnp.float32), axis=0), axis=0)[l]


def hgrn2_direction(q, k, v, log_f):
    B, T, H, DK = q.shape
    DV = v.shape[-1]
    nc = T // HG_CHUNK

    def to_chunks(a):
        return a.reshape(B, nc, HG_CHUNK, H, a.shape[-1]).transpose(1, 0, 3, 2, 4)

    lower_tri = jnp.tril(jnp.ones((HG_CHUNK, HG_CHUNK), dtype=bool))

    def step(S, inp):
        qc, kc, vc, gc = inp
        b = jnp.cumsum(gc, axis=2)
        inter = jnp.einsum('bhik,bhkv->bhiv', qc * jnp.exp(b), S)
        diff = b[:, :, :, None, :] - b[:, :, None, :, :]
        decay = jnp.where(lower_tri[:, :, None], jnp.exp(jnp.minimum(diff, 0.0)), 0.0)
        scores = jnp.einsum('bhik,bhjk,bhijk->bhij', qc, kc, decay)
        intra = jnp.einsum('bhij,bhjv->bhiv', scores, vc)
        b_last = b[:, :, -1:, :]
        k_dec = kc * jnp.exp(b_last - b)
        S = jnp.exp(b_last[:, :, 0, :])[..., None] * S + jnp.einsum('bhck,bhcv->bhkv', k_dec, vc)
        return S, inter + intra

    S0 = jnp.zeros((B, H, DK, DV), jnp.float32)
    _, o = lax.scan(step, S0, (to_chunks(q), to_chunks(k), to_chunks(v), to_chunks(log_f)))
    return o.transpose(1, 0, 3, 2, 4).reshape(B, T, H, DV)


def hgrn2_mixer(hq, hf_fwd, hf_bwd, hi, hg, lb_fwd, lb_bwd, norm_w):
    B, T, _ = hq.shape

    def heads(a, d):
        return a.astype(jnp.float32).reshape(B, T, HG_HEADS, d)

    qh = heads(hq, HG_DK)
    vh = heads(hi, HG_DV)

    def gates(f_logits, lb):
        lbh = lb.reshape(HG_HEADS, HG_DK)
        f = lbh + (1.0 - lbh) * jax.nn.sigmoid(heads(f_logits, HG_DK))
        return 1.0 - f, jnp.log(f)

    k_f, lf_f = gates(hf_fwd, lb_fwd)
    k_b, lf_b = gates(hf_bwd, lb_bwd)
    o_f = hgrn2_direction(qh, k_f, vh, lf_f)
    flip = lambda a: jnp.flip(a, axis=1)
    o_b = flip(hgrn2_direction(flip(qh), flip(k_b), flip(vh), flip(lf_b)))
    o = head_rms(o_f + o_b, norm_w.reshape(HG_HEADS, HG_DV)).reshape(B, T, HG_WIDTH)
    return (o * jax.nn.sigmoid(hg.astype(jnp.float32))).astype(hq.dtype)


def neighbourhood_attention(aq, ak, av, q_norm_w, k_norm_w, rpb, out_norm_w):
    B, T, _ = aq.shape
    rows = T // GRID_W
    kh = min(NA_KH, rows)
    kw = NA_KW

    def grid(a):
        return a.reshape(B, rows, GRID_W, NA_HEADS, NA_DH)

    qg = head_rms(grid(aq), q_norm_w) * (NA_DH ** -0.5)
    kg = head_rms(grid(ak), k_norm_w)
    vg = grid(av).astype(jnp.float32)
    cols = jnp.arange(GRID_W)
    col_start = jnp.clip(cols - kw // 2, 0, GRID_W - kw)
    col_idx = col_start[:, None] + jnp.arange(kw)[None, :]
    dc_idx = col_idx - cols[:, None] + (NA_KW - 1)
    rpb_cols = rpb.astype(jnp.float32)[:, :, dc_idx]

    def row_block(args):
        r, q_row = args
        rs = jnp.clip(r - kh // 2, 0, rows - kh)
        k_win = lax.dynamic_slice_in_dim(kg, rs, kh, axis=1)[:, :, col_idx]
        v_win = lax.dynamic_slice_in_dim(vg, rs, kh, axis=1)[:, :, col_idx]
        dr_idx = rs + jnp.arange(kh) - r + (NA_KH - 1)
        bias = rpb_cols[:, dr_idx].transpose(0, 2, 1, 3)
        s = jnp.einsum('bchd,bacwhd->bhcaw', q_row, k_win) + bias
        p = jax.nn.softmax(s.reshape(B, NA_HEADS, GRID_W, kh * kw), axis=-1).reshape(s.shape)
        return jnp.einsum('bhcaw,bacwhd->bchd', p, v_win)

    out = lax.map(row_block, (jnp.arange(rows), qg.transpose(1, 0, 2, 3, 4)))
    out = head_rms(out.transpose(1, 0, 2, 3, 4), out_norm_w.reshape(NA_HEADS, NA_DH))
    return out.reshape(B, T, NA_WIDTH).astype(aq.dtype)


def hierarchical_moe(x, w_rg, b_rg, w_re, b_re, w_gate, w_up, w_down):
    N, D = x.shape
    xf = x.astype(jnp.float32)
    p_grp = jax.nn.softmax(xf @ w_rg.astype(jnp.float32) + b_rg.astype(jnp.float32), axis=-1)
    p_top, grp = lax.top_k(p_grp, 1)
    e_logits = (xf @ w_re.astype(jnp.float32) + b_re.astype(jnp.float32)).reshape(N, N_GROUPS, EXPERTS_PER_GROUP)
    grp_idx = jnp.broadcast_to(grp[:, :, None], (N, 1, EXPERTS_PER_GROUP))
    in_grp = jnp.take_along_axis(e_logits, grp_idx, axis=1)[:, 0]
    top_logit, top_idx = lax.top_k(in_grp, TOP_K_IN_GROUP)
    w_sel = p_top * jax.nn.softmax(top_logit, axis=-1)
    expert_id = grp * EXPERTS_PER_GROUP + top_idx
    gates = jnp.sum(jax.nn.one_hot(expert_id, N_EXPERTS, dtype=jnp.float32) * w_sel[..., None], axis=1)

    def block(args):
        xb, gb = args
        h = jax.nn.silu(jnp.einsum('nd,edf->nef', xb, w_gate)) * jnp.einsum('nd,edf->nef', xb, w_up)
        return jnp.einsum('nef,efd->nd', h * gb[:, :, None].astype(h.dtype), w_down)

    y = lax.map(block, (x.reshape(N // MOE_BLOCK, MOE_BLOCK, D), gates.reshape(N // MOE_BLOCK, MOE_BLOCK, N_EXPERTS)))
    return y.reshape(N, D)


def encoder_layer(x, l, norm1_w, w_in, lb_fwd_logits, lb_bwd_logits, hgrn_norm_w, q_norm_w, k_norm_w,
                  na_rpb, na_out_norm_w, w_out, norm2_w, w_router_group, b_router_group,
                  w_router_expert, b_router_expert, w_expert_gate, w_expert_up, w_expert_down):
    B, T, D = x.shape
    h = rms_norm(x, norm1_w[l])
    proj = jnp.einsum('btd,de->bte', h, w_in[l])
    hq, hf_f, hf_b, hi, hg, aq, ak, av = jnp.split(proj, IN_SPLITS, axis=-1)
    o_a = hgrn2_mixer(hq, hf_f, hf_b, hi, hg, lower_bound(lb_fwd_logits, l), lower_bound(lb_bwd_logits, l), hgrn_norm_w[l])
    o_b = neighbourhood_attention(aq, ak, av, q_norm_w[l], k_norm_w[l], na_rpb[l], na_out_norm_w[l])
    mix = jnp.concatenate([o_a, o_b], axis=-1).astype(x.dtype)
    x = x + jnp.einsum('btm,md->btd', mix, w_out[l]).astype(x.dtype)
    h2 = rms_norm(x, norm2_w[l]).reshape(B * T, D)
    y = hierarchical_moe(h2, w_router_group[l], b_router_group[l], w_router_expert[l], b_router_expert[l],
                         w_expert_gate[l], w_expert_up[l], w_expert_down[l])
    return x + y.reshape(B, T, D).astype(x.dtype)


def trunk(x, norm1_w, w_in, lb_fwd_logits, lb_bwd_logits, hgrn_norm_w, q_norm_w, k_norm_w, na_rpb,
          na_out_norm_w, w_out, norm2_w, w_router_group, b_router_group, w_router_expert, b_router_expert,
          w_expert_gate, w_expert_up, w_expert_down):
    for l in range(DEPTH):
        x = encoder_layer(x, l, norm1_w, w_in, lb_fwd_logits, lb_bwd_logits, hgrn_norm_w, q_norm_w, k_norm_w,
                          na_rpb, na_out_norm_w, w_out, norm2_w, w_router_group, b_router_group,
                          w_router_expert, b_router_expert, w_expert_gate, w_expert_up, w_expert_down)
    return x


def setup_inputs(seed: int = 0) -> dict:
    key = jax.random.key(seed)
    ks = jax.random.split(key, 20)
    nrm = jax.random.normal
    f32 = jnp.float32
    return {
        'x_prompt': nrm(ks[0], (BATCH, SEQ, D_MODEL), f32),
        'x_sample': nrm(ks[1], (DEC_BATCH, DEC_SEQ, D_MODEL), f32),
        'norm1_w': 1.0 + 0.05 * nrm(ks[2], (DEPTH, D_MODEL), f32),
        'w_in': nrm(ks[3], (DEPTH, D_MODEL, IN_WIDTH), f32) * D_MODEL ** -0.5,
        'lb_fwd_logits': 0.5 * nrm(ks[4], (DEPTH + 1, HG_KWIDTH), f32),
        'lb_bwd_logits': 0.5 * nrm(ks[5], (DEPTH + 1, HG_KWIDTH), f32),
        'hgrn_norm_w': 1.0 + 0.05 * nrm(ks[6], (DEPTH, HG_WIDTH), f32),
        'q_norm_w': 1.0 + 0.05 * nrm(ks[7], (DEPTH, NA_DH), f32),
        'k_norm_w': 1.0 + 0.05 * nrm(ks[8], (DEPTH, NA_DH), f32),
        'na_rpb': 0.1 * nrm(ks[9], (DEPTH, NA_HEADS, 2 * NA_KH - 1, 2 * NA_KW - 1), f32),
        'na_out_norm_w': 1.0 + 0.05 * nrm(ks[10], (DEPTH, NA_WIDTH), f32),
        'w_out': nrm(ks[11], (DEPTH, MIX_WIDTH, D_MODEL), f32) * MIX_WIDTH ** -0.5,
        'norm2_w': 1.0 + 0.05 * nrm(ks[12], (DEPTH, D_MODEL), f32),
        'w_router_group': nrm(ks[13], (DEPTH, D_MODEL, N_GROUPS), f32) * D_MODEL ** -0.5,
        'b_router_group': 0.01 * nrm(ks[14], (DEPTH, N_GROUPS), f32),
        'w_router_expert': nrm(ks[15], (DEPTH, D_MODEL, N_EXPERTS), f32) * D_MODEL ** -0.5,
        'b_router_expert': 0.01 * nrm(ks[16], (DEPTH, N_EXPERTS), f32),
        'w_expert_gate': nrm(ks[17], (DEPTH, N_EXPERTS, D_MODEL, EXPERT_FF), f32) * D_MODEL ** -0.5,
        'w_expert_up': nrm(ks[18], (DEPTH, N_EXPERTS, D_MODEL, EXPERT_FF), f32) * D_MODEL ** -0.5,
        'w_expert_down': nrm(ks[19], (DEPTH, N_EXPERTS, EXPERT_FF, D_MODEL), f32) * EXPERT_FF ** -0.5,
    }


def reference(x_prompt, x_sample, norm1_w, w_in, lb_fwd_logits, lb_bwd_logits, hgrn_norm_w, q_norm_w,
              k_norm_w, na_rpb, na_out_norm_w, w_out, norm2_w, w_router_group, b_router_group,
              w_router_expert, b_router_expert, w_expert_gate, w_expert_up, w_expert_down):
    y_prompt = trunk(x_prompt, norm1_w, w_in, lb_fwd_logits, lb_bwd_logits, hgrn_norm_w, q_norm_w, k_norm_w,
                     na_rpb, na_out_norm_w, w_out, norm2_w, w_router_group, b_router_group, w_router_expert,
                     b_router_expert, w_expert_gate, w_expert_up, w_expert_down)
    y_sample = trunk(x_sample, norm1_w, w_in, lb_fwd_logits, lb_bwd_logits, hgrn_norm_w, q_norm_w, k_norm_w,
                     na_rpb, na_out_norm_w, w_out, norm2_w, w_router_group, b_router_group, w_router_expert,
                     b_router_expert, w_expert_gate, w_expert_up, w_expert_down)
    return (y_prompt, y_sample)
```

```python
import functools

import numpy as np
import jax
import jax.numpy as jnp
from jax import lax
from jax.experimental import pallas as pl
from jax.experimental.pallas import tpu as pltpu

D_MODEL = 1024
EPS = 1e-6
GRID_W = 64
HG_HEADS = 4
HG_D = 128
HG_WIDTH = HG_HEADS * HG_D
HG_CHUNK = 64
HG_LEVELS = 6
NA_HEADS = 8
NA_DH = 64
NA_WIDTH = NA_HEADS * NA_DH
NA_KH = 8
NA_KW = 16
NA_ROWS_PER_STEP = 8
N_GROUPS = 4
EXPERTS_PER_GROUP = 8
N_EXPERTS = N_GROUPS * EXPERTS_PER_GROUP
EXPERT_FF = 512
ROUTER_LANES = 128
EXPERT_LANE0 = N_GROUPS
NEG_BIG = -1e30
LANE = 128
VMEM_LIMIT = 56 * 1024 * 1024

COL_Q, COL_V, COL_G, COL_AQ, COL_AK, COL_AV = range(6)

_f32 = jnp.float32
_bf16 = jnp.bfloat16


def _dot(a, b):
    return jnp.dot(a, b, preferred_element_type=_f32)


def _dot_nt(a, b):
    return lax.dot_general(a, b, (((1,), (1,)), ((), ())), preferred_element_type=_f32)


def _dot_tn(a, b):
    return lax.dot_general(a, b, (((0,), (0,)), ((), ())), preferred_element_type=_f32)


def _split2(x):
    hi = x.astype(_bf16)
    lo = (x - hi.astype(_f32)).astype(_bf16)
    return hi, lo


def _params(*sem):
    return pltpu.CompilerParams(dimension_semantics=sem, vmem_limit_bytes=VMEM_LIMIT)


def _const_spec(shape):
    nd = len(shape)
    return pl.BlockSpec(shape, lambda *_: (0,) * nd)


def _in_proj_kernel(x_ref, n1_ref, w_ref, lb_ref, qkw_ref, seg_ref, proj_ref, gate_ref):
    x = x_ref[...]
    ms = jnp.mean(x * x, axis=-1, keepdims=True)
    h = (x * lax.rsqrt(ms + EPS) * n1_ref[...]).astype(_bf16)
    W = HG_WIDTH

    def col(j):
        return _dot(h, w_ref[:, j * W:(j + 1) * W])

    proj_ref[:, COL_Q * W:(COL_Q + 1) * W] = col(0).astype(_bf16)
    for d in range(2):
        z = col(1 + d)
        lb = lb_ref[d:d + 1, :]
        f = lb + (1.0 - lb) * (1.0 / (1.0 + jnp.exp(-z)))
        gate_ref[:, d * W:(d + 1) * W] = jnp.log(f)
    proj_ref[:, COL_V * W:(COL_V + 1) * W] = col(3).astype(_bf16)
    proj_ref[:, COL_G * W:(COL_G + 1) * W] = col(4).astype(_bf16)
    seg = seg_ref[...]
    for d, c in ((0, COL_AQ), (1, COL_AK)):
        a = col(5 + d)
        hi, lo = _split2(a * a)
        msh = _dot(hi, seg) + _dot(lo, seg)
        proj_ref[:, c * W:(c + 1) * W] = (a * lax.rsqrt(msh + EPS) * qkw_ref[d:d + 1, :]).astype(_bf16)
    proj_ref[:, COL_AV * W:(COL_AV + 1) * W] = col(7).astype(_bf16)


def _in_proj(x, n1w, w_in_bf, lb, qkw, seg, tm):
    S, T, D = x.shape
    return pl.pallas_call(
        _in_proj_kernel,
        grid=(S, T // tm),
        in_specs=[
            pl.BlockSpec((None, tm, D), lambda s, i: (s, i, 0)),
            _const_spec((1, D)),
            _const_spec(w_in_bf.shape),
            _const_spec(lb.shape),
            _const_spec(qkw.shape),
            _const_spec(seg.shape),
        ],
        out_specs=[
            pl.BlockSpec((None, tm, 6 * HG_WIDTH), lambda s, i: (s, i, 0)),
            pl.BlockSpec((None, tm, 2 * HG_WIDTH), lambda s, i: (s, i, 0)),
        ],
        out_shape=[
            jax.ShapeDtypeStruct((S, T, 6 * HG_WIDTH), _bf16),
            jax.ShapeDtypeStruct((S, T, 2 * HG_WIDTH), _f32),
        ],
        compiler_params=_params("parallel", "parallel"),
        name="in_proj",
    )(x, n1w, w_in_bf, lb, qkw, seg)


def _hgrn_constants():
    C = HG_CHUNK
    idx = np.arange(C)
    dmats, masks = [], []
    for direction in (0, 1):
        if direction == 0:
            tri = (idx[None, :] <= idx[:, None]).astype(np.float64)
        else:
            tri = (idx[None, :] >= idx[:, None]).astype(np.float64)
        dm = [tri]
        mk = [np.eye(C)]
        for lvl in range(HG_LEVELS):
            h = 1 << lvl
            start = (idx // (2 * h)) * (2 * h)
            ref = start + h - 1 if direction == 0 else start + h
            gather = np.zeros((C, C))
            gather[idx, ref] = 1.0
            dm.append(tri - gather @ tri)
            upper = (idx & h) != 0
            later, earlier = (upper, ~upper) if direction == 0 else (~upper, upper)
            same = (idx[:, None] // (2 * h)) == (idx[None, :] // (2 * h))
            mk.append((same & later[:, None] & earlier[None, :]).astype(np.float64))
        dmats.append(np.concatenate(dm, axis=0))
        masks.append(np.stack(mk, axis=0))
    return np.stack(dmats, 0), np.stack(masks, 0)


def _hgrn_kernel(qf_ref, qb_ref, vf_ref, vb_ref, gf_ref, gb_ref, dmat_ref, mask_ref,
                 of_ref, ob_ref, state_ref, *, chunks):
    C = HG_CHUNK

    @pl.when(pl.program_id(1) == 0)
    def _():
        state_ref[...] = jnp.zeros_like(state_ref)

    def chunk_step(c, carry):
        for direction, (q_ref, v_ref, g_ref, o_ref) in enumerate(
                ((qf_ref, vf_ref, gf_ref, of_ref), (qb_ref, vb_ref, gb_ref, ob_ref))):
            cc = c if direction == 0 else chunks - 1 - c
            rows = pl.ds(pl.multiple_of(cc * C, C), C)
            g = g_ref[rows, :]
            g_hi, g_lo = _split2(g)
            dmat = dmat_ref[direction]
            delta_all = _dot(dmat, g_hi) + _dot(dmat, g_lo)
            last = C - 1 if direction == 0 else 0
            for hd in range(HG_HEADS):
                cols = slice(hd * HG_D, (hd + 1) * HG_D)
                q = q_ref[rows, cols].astype(_f32)
                v = v_ref[rows, cols]
                k = 1.0 - jnp.exp(g[:, cols])
                b = delta_all[0:C, cols]
                scores = mask_ref[direction, 0] * _dot_nt(q.astype(_bf16), k.astype(_bf16))
                for lvl in range(HG_LEVELS):
                    d = delta_all[(lvl + 1) * C:(lvl + 2) * C, cols]
                    e = jnp.exp(-jnp.abs(d))
                    p = _dot_nt((q * e).astype(_bf16), (k * e).astype(_bf16))
                    scores = scores + mask_ref[direction, lvl + 1] * p
                st = state_ref[direction, hd]
                b_last = b[last:last + 1, :]
                inter = _dot_nt((q * jnp.exp(b)).astype(_bf16), st.astype(_bf16))
                intra = _dot(scores.astype(_bf16), v)
                o_ref[rows, cols] = inter + intra
                k_dec = (k * jnp.exp(b_last - b)).astype(_bf16)
                state_ref[direction, hd] = jnp.exp(b_last) * st + _dot_tn(v, k_dec)
        return carry

    lax.fori_loop(0, chunks, chunk_step, 0)


def _hgrn2(proj, gates, dmat, mask, tb):
    S, T, _ = proj.shape
    n = T // tb
    W = HG_WIDTH
    fwd = lambda col: (lambda s, i: (s, i, col))
    bwd = lambda col: (lambda s, i: (s, n - 1 - i, col))
    blk = lambda im: pl.BlockSpec((None, tb, W), im)
    return pl.pallas_call(
        functools.partial(_hgrn_kernel, chunks=tb // HG_CHUNK),
        grid=(S, n),
        in_specs=[
            blk(fwd(COL_Q)), blk(bwd(COL_Q)), blk(fwd(COL_V)), blk(bwd(COL_V)),
            blk(fwd(0)), blk(bwd(1)),
            _const_spec(dmat.shape), _const_spec(mask.shape),
        ],
        out_specs=[blk(fwd(0)), blk(bwd(0))],
        out_shape=[jax.ShapeDtypeStruct((S, T, W), _f32)] * 2,
        scratch_shapes=[pltpu.VMEM((2, HG_HEADS, HG_D, HG_D), _f32)],
        compiler_params=_params("parallel", "arbitrary"),
        name="hgrn2",
    )(proj, proj, proj, proj, gates, gates, dmat, mask)


def _natten_bias(rpb):
    W = GRID_W
    case = np.arange(NA_KH)[:, None, None, None]
    qc = np.arange(W)[None, :, None, None]
    a = np.arange(NA_KH)[None, None, :, None]
    kc = np.arange(W)[None, None, None, :]
    cs = np.clip(qc - NA_KW // 2, 0, W - NA_KW)
    valid = np.broadcast_to((kc >= cs) & (kc < cs + NA_KW), (NA_KH, W, NA_KH, W))
    dr = np.broadcast_to(a - case + (NA_KH - 1), (NA_KH, W, NA_KH, W))
    dc = np.broadcast_to(np.clip(kc - qc + (NA_KW - 1), 0, 2 * NA_KW - 2), (NA_KH, W, NA_KH, W))
    bias = rpb.astype(_f32)[:, dr, dc]
    bias = jnp.where(valid[None], bias, NEG_BIG)
    bias = bias.transpose(1, 0, 2, 3, 4).reshape(NA_KH, NA_HEADS // 2, 2 * W, NA_KH * W)
    return bias


def _natten_kernel(q_ref, kp_ref, kc_ref, kn_ref, vp_ref, vc_ref, vn_ref, bias_ref, ow_ref, seg_ref,
                   o_ref, kbuf, vbuf, *, grid_rows):
    R = NA_ROWS_PER_STEP
    W = GRID_W
    blk = R * W
    for j, (k_r, v_r) in enumerate(((kp_ref, vp_ref), (kc_ref, vc_ref), (kn_ref, vn_ref))):
        kbuf[j * blk:(j + 1) * blk, :] = k_r[...]
        vbuf[j * blk:(j + 1) * blk, :] = v_r[...]
    r0 = pl.program_id(1) * R
    lane = lax.broadcasted_iota(jnp.int32, (W, LANE), 1)
    first = lane < NA_DH
    seg = seg_ref[...]

    def row_step(i, carry):
        r = r0 + i
        rs = jnp.clip(r - NA_KH // 2, 0, grid_rows - NA_KH)
        case = r - rs
        off = pl.multiple_of((rs - (r0 - R)) * W, W)
        qrows = pl.ds(pl.multiple_of(i * W, W), W)
        outs = []
        for p in range(NA_HEADS // 2):
            cols = slice(p * LANE, (p + 1) * LANE)
            qp = q_ref[qrows, cols]
            zero = jnp.zeros_like(qp)
            lhs = jnp.concatenate([jnp.where(first, qp, zero), jnp.where(first, zero, qp)], axis=0)
            s = _dot_nt(lhs, kbuf[pl.ds(off, NA_KH * W), cols]) + bias_ref[case, p]
            m = jnp.max(s, axis=-1, keepdims=True)
            e = jnp.exp(s - m)
            l = jnp.sum(e, axis=-1, keepdims=True)
            pv = _dot(e.astype(_bf16), vbuf[pl.ds(off, NA_KH * W), cols]) / l
            outs.append(jnp.where(first, pv[0:W], pv[W:2 * W]))
        out = jnp.concatenate(outs, axis=-1)
        hi, lo = _split2(out * out)
        msh = _dot(hi, seg) + _dot(lo, seg)
        o_ref[qrows, :] = (out * lax.rsqrt(msh + EPS) * ow_ref[...]).astype(_bf16)
        return carry

    lax.fori_loop(0, R, row_step, 0)


def _natten(proj, bias, out_w, seg):
    S, T, _ = proj.shape
    rows = T // GRID_W
    assert rows >= NA_KH and rows % NA_ROWS_PER_STEP == 0
    nb = rows // NA_ROWS_PER_STEP
    blk_t = NA_ROWS_PER_STEP * GRID_W
    Wd = NA_WIDTH
    cur = lambda col: (lambda s, i: (s, i, col))
    prv = lambda col: (lambda s, i: (s, jnp.maximum(i - 1, 0), col))
    nxt = lambda col: (lambda s, i: (s, jnp.minimum(i + 1, nb - 1), col))
    blk = lambda im: pl.BlockSpec((None, blk_t, Wd), im)
    return pl.pallas_call(
        functools.partial(_natten_kernel, grid_rows=rows),
        grid=(S, nb),
        in_specs=[
            blk(cur(COL_AQ)),
            blk(prv(COL_AK)), blk(cur(COL_AK)), blk(nxt(COL_AK)),
            blk(prv(COL_AV)), blk(cur(COL_AV)), blk(nxt(COL_AV)),
            _const_spec(bias.shape), _const_spec(out_w.shape), _const_spec(seg.shape),
        ],
        out_specs=blk(cur(0)),
        out_shape=jax.ShapeDtypeStruct((S, T, Wd), _bf16),
        scratch_shapes=[pltpu.VMEM((3 * blk_t, Wd), _bf16), pltpu.VMEM((3 * blk_t, Wd), _bf16)],
        compiler_params=_params("parallel", "parallel"),
        name="natten",
    )(proj, proj, proj, proj, proj, proj, proj, bias, out_w, seg)


def _out_proj_kernel(x_ref, of_ref, ob_ref, hg_ref, na_ref, w_ref, hw_ref, n2_ref, rw_ref, rb_ref,
                     x1_ref, h2_ref, gates_ref):
    o = of_ref[...] + ob_ref[...]
    parts = []
    for hd in range(HG_HEADS):
        oh = o[:, hd * HG_D:(hd + 1) * HG_D]
        ms = jnp.mean(oh * oh, axis=-1, keepdims=True)
        parts.append(oh * lax.rsqrt(ms + EPS))
    on = jnp.concatenate(parts, axis=-1) * hw_ref[...]
    sig = 1.0 / (1.0 + jnp.exp(-hg_ref[...].astype(_f32)))
    mix_a = (on * sig).astype(_bf16)
    acc = _dot(mix_a, w_ref[0:HG_WIDTH, :]) + _dot(na_ref[...], w_ref[HG_WIDTH:, :])
    x1 = x_ref[...] + acc
    x1_ref[...] = x1
    ms2 = jnp.mean(x1 * x1, axis=-1, keepdims=True)
    h2 = x1 * lax.rsqrt(ms2 + EPS) * n2_ref[...]
    h2_ref[...] = h2.astype(_bf16)

    h_hi, h_lo = _split2(h2)
    w_hi = rw_ref[0]
    w_lo = rw_ref[1]
    lg = _dot(h_hi, w_hi) + _dot(h_lo, w_hi) + _dot(h_hi, w_lo) + rb_ref[...]
    lane_i = lax.broadcasted_iota(jnp.int32, lg.shape, 1)
    lane = lane_i.astype(_f32)
    lane_grp = ((lane_i - EXPERT_LANE0) >> 3).astype(_f32)

    def first_argmax(vals, valid):
        masked = jnp.where(valid, vals, NEG_BIG)
        top = jnp.max(masked, axis=-1, keepdims=True)
        idx = jnp.min(jnp.where(valid & (masked == top), lane, float(ROUTER_LANES)), axis=-1, keepdims=True)
        return top, idx

    is_grp = lane_i < N_GROUPS
    g_top, g_idx = first_argmax(lg, is_grp)
    p_top = 1.0 / jnp.sum(jnp.where(is_grp, jnp.exp(lg - g_top), 0.0), axis=-1, keepdims=True)
    in_grp = (lane_i >= EXPERT_LANE0) & (lane_i < EXPERT_LANE0 + N_EXPERTS) & (lane_grp == g_idx)
    t1, i1 = first_argmax(lg, in_grp)
    t2, i2 = first_argmax(lg, in_grp & (lane != i1))
    e2 = jnp.exp(t2 - t1)
    w1 = p_top / (1.0 + e2)
    w2 = p_top * e2 / (1.0 + e2)
    gates_ref[...] = jnp.where(lane == i1, w1, jnp.where(lane == i2, w2, 0.0))


def _out_proj(x, o_f, o_b, proj, na, w_out_bf, hw, n2w, rw, rb, tm):
    S, T, D = x.shape
    W = HG_WIDTH
    row = lambda width, col=0: pl.BlockSpec((None, tm, width), lambda s, i: (s, i, col))
    return pl.pallas_call(
        _out_proj_kernel,
        grid=(S, T // tm),
        in_specs=[
            row(D), row(W), row(W), row(W, COL_G), row(W),
            _const_spec(w_out_bf.shape), _const_spec(hw.shape), _const_spec(n2w.shape),
            _const_spec(rw.shape), _const_spec(rb.shape),
        ],
        out_specs=[row(D), row(D), row(ROUTER_LANES)],
        out_shape=[
            jax.ShapeDtypeStruct((S, T, D), _f32),
            jax.ShapeDtypeStruct((S, T, D), _bf16),
            jax.ShapeDtypeStruct((S, T, ROUTER_LANES), _f32),
        ],
        compiler_params=_params("parallel", "parallel"),
        name="out_proj",
    )(x, o_f, o_b, proj, na, w_out_bf, hw, n2w, rw, rb)


def _moe_kernel(x1_ref, h2_ref, gates_ref, wg_ref, wu_ref, wd_ref, y_ref):
    e = pl.program_id(1)

    @pl.when(e == 0)
    def _():
        y_ref[...] = x1_ref[...]

    h2 = h2_ref[...]
    a = _dot(h2, wg_ref[...])
    u = _dot(h2, wu_ref[...])
    gates = gates_ref[...]
    lane = lax.broadcasted_iota(jnp.int32, gates.shape, 1)
    gate = jnp.sum(jnp.where(lane == e + EXPERT_LANE0, gates, 0.0), axis=-1, keepdims=True)
    hmid = a * (1.0 / (1.0 + jnp.exp(-a))) * u * gate
    y_ref[...] += _dot(hmid.astype(_bf16), wd_ref[...])


def _moe(x1, h2, gates, wg, wu, wd, tm):
    N, D = x1.shape
    return pl.pallas_call(
        _moe_kernel,
        grid=(N // tm, N_EXPERTS),
        in_specs=[
            pl.BlockSpec((tm, D), lambda i, e: (i, 0)),
            pl.BlockSpec((tm, D), lambda i, e: (i, 0)),
            pl.BlockSpec((tm, ROUTER_LANES), lambda i, e: (i, 0)),
            pl.BlockSpec((None, D, EXPERT_FF), lambda i, e: (e, 0, 0)),
            pl.BlockSpec((None, D, EXPERT_FF), lambda i, e: (e, 0, 0)),
            pl.BlockSpec((None, EXPERT_FF, D), lambda i, e: (e, 0, 0)),
        ],
        out_specs=pl.BlockSpec((tm, D), lambda i, e: (i, 0)),
        out_shape=jax.ShapeDtypeStruct((N, D), _f32),
        compiler_params=_params("parallel", "arbitrary"),
        name="moe",
    )(x1, h2, gates, wg, wu, wd)


def _lower_bound(logits, l):
    return jnp.cumsum(jax.nn.softmax(logits.astype(_f32), axis=0), axis=0)[l]


def _layer(x, l, c, tm=512, hgrn_tb=512, moe_tm=1024):
    S, T, D = x.shape
    tm = min(tm, T)
    proj, gates = _in_proj(x, c["n1w"][l], c["w_in"][l], c["lb"][l], c["qkw"][l], c["seg"], tm)
    o_f, o_b = _hgrn2(proj, gates, c["dmat"], c["mask"], min(hgrn_tb, T))
    na = _natten(proj, c["na_bias"][l], c["na_ow"][l], c["seg"])
    x1, h2, rgates = _out_proj(x, o_f, o_b, proj, na, c["w_out"][l], c["hw"][l], c["n2w"][l],
                               c["rw"][l], c["rb"][l], tm)
    N = S * T
    y = _moe(x1.reshape(N, D), h2.reshape(N, D), rgates.reshape(N, ROUTER_LANES),
             c["wg"][l], c["wu"][l], c["wd"][l], min(moe_tm, N))
    return y.reshape(S, T, D)


def _prepare(norm1_w, w_in, lb_fwd_logits, lb_bwd_logits, hgrn_norm_w, q_norm_w, k_norm_w, na_rpb,
             na_out_norm_w, w_out, norm2_w, w_router_group, b_router_group, w_router_expert,
             b_router_expert, w_expert_gate, w_expert_up, w_expert_down):
    depth = w_in.shape[0]
    dmat, mask = _hgrn_constants()
    head = np.arange(NA_WIDTH) // NA_DH
    seg = (head[:, None] == head[None, :]).astype(np.float32) / NA_DH
    pad = ROUTER_LANES - N_GROUPS - N_EXPERTS
    rw = jnp.pad(jnp.concatenate([w_router_group, w_router_expert], axis=-1).astype(_f32),
                 ((0, 0), (0, 0), (0, pad)))
    rw_hi = rw.astype(_bf16)
    rw_lo = (rw - rw_hi.astype(_f32)).astype(_bf16)
    rb = jnp.pad(jnp.concatenate([b_router_group, b_router_expert], axis=-1).astype(_f32),
                 ((0, 0), (0, pad)))
    return {
        "n1w": norm1_w.astype(_f32)[:, None, :],
        "w_in": w_in.astype(_bf16),
        "lb": jnp.stack([jnp.stack([_lower_bound(lb_fwd_logits, l), _lower_bound(lb_bwd_logits, l)])
                         for l in range(depth)]),
        "qkw": jnp.stack([jnp.tile(q_norm_w.astype(_f32), (1, NA_HEADS)) * (NA_DH ** -0.5),
                          jnp.tile(k_norm_w.astype(_f32), (1, NA_HEADS))], axis=1),
        "seg": jnp.asarray(seg, _bf16),
        "dmat": jnp.asarray(dmat, _bf16),
        "mask": jnp.asarray(mask, _f32),
        "na_bias": jnp.stack([_natten_bias(na_rpb[l]) for l in range(depth)]),
        "na_ow": na_out_norm_w.astype(_f32)[:, None, :],
        "w_out": w_out.astype(_bf16),
        "hw": hgrn_norm_w.astype(_f32)[:, None, :],
        "n2w": norm2_w.astype(_f32)[:, None, :],
        "rw": jnp.stack([rw_hi, rw_lo], axis=1),
        "rb": rb[:, None, :],
        "wg": w_expert_gate.astype(_bf16),
        "wu": w_expert_up.astype(_bf16),
        "wd": w_expert_down.astype(_bf16),
    }


def kernel(x_prompt, x_sample, norm1_w, w_in, lb_fwd_logits, lb_bwd_logits, hgrn_norm_w, q_norm_w, k_norm_w, na_rpb, na_out_norm_w, w_out, norm2_w, w_router_group, b_router_group, w_router_expert, b_router_expert, w_expert_gate, w_expert_up, w_expert_down):
    c = _prepare(norm1_w, w_in, lb_fwd_logits, lb_bwd_logits, hgrn_norm_w, q_norm_w, k_norm_w, na_rpb,
                 na_out_norm_w, w_out, norm2_w, w_router_group, b_router_group, w_router_expert,
                 b_router_expert, w_expert_gate, w_expert_up, w_expert_down)
    outs = []
    for x in (x_prompt, x_sample):
        for l in range(w_in.shape[0]):
            x = _layer(x, l, c)
        outs.append(x)
    return tuple(outs)
```

```python
import functools

import numpy as np
import jax
import jax.numpy as jnp
from jax import lax
from jax.experimental import pallas as pl
from jax.experimental.pallas import tpu as pltpu

D_MODEL = 1024
EPS = 1e-6
GRID_W = 64
HG_HEADS = 4
HG_D = 128
HG_WIDTH = HG_HEADS * HG_D
HG_CHUNK = 64
HG_LEVELS = 6
NA_HEADS = 8
NA_DH = 64
NA_WIDTH = NA_HEADS * NA_DH
NA_KH = 8
NA_KW = 16
NA_ROWS_PER_STEP = 8
N_GROUPS = 4
EXPERTS_PER_GROUP = 8
N_EXPERTS = N_GROUPS * EXPERTS_PER_GROUP
EXPERT_FF = 512
ROUTER_LANES = 128
EXPERT_LANE0 = N_GROUPS
NEG_BIG = -1e30
LANE = 128
VMEM_LIMIT = 56 * 1024 * 1024

COL_Q, COL_V, COL_G, COL_AQ, COL_AK, COL_AV = range(6)

_f32 = jnp.float32
_bf16 = jnp.bfloat16


def _dot(a, b):
    return jnp.dot(a, b, preferred_element_type=_f32)


def _dot_nt(a, b):
    return lax.dot_general(a, b, (((1,), (1,)), ((), ())), preferred_element_type=_f32)


def _dot_tn(a, b):
    return lax.dot_general(a, b, (((0,), (0,)), ((), ())), preferred_element_type=_f32)


def _split2(x):
    hi = x.astype(_bf16)
    lo = (x - hi.astype(_f32)).astype(_bf16)
    return hi, lo


def _params(*sem):
    return pltpu.CompilerParams(dimension_semantics=sem, vmem_limit_bytes=VMEM_LIMIT)


def _const_spec(shape):
    nd = len(shape)
    return pl.BlockSpec(shape, lambda *_: (0,) * nd)


def _in_proj_kernel(x_ref, n1_ref, w_ref, lb_ref, qkw_ref, seg_ref, proj_ref, gate_ref):
    x = x_ref[...]
    ms = jnp.mean(x * x, axis=-1, keepdims=True)
    h = (x * lax.rsqrt(ms + EPS) * n1_ref[...]).astype(_bf16)
    W = HG_WIDTH

    def col(j):
        return _dot(h, w_ref[:, j * W:(j + 1) * W])

    proj_ref[:, COL_Q * W:(COL_Q + 1) * W] = col(0).astype(_bf16)
    for d in range(2):
        z = col(1 + d)
        lb = lb_ref[d:d + 1, :]
        f = lb + (1.0 - lb) * (1.0 / (1.0 + jnp.exp(-z)))
        gate_ref[:, d * W:(d + 1) * W] = jnp.log(f)
    proj_ref[:, COL_V * W:(COL_V + 1) * W] = col(3).astype(_bf16)
    proj_ref[:, COL_G * W:(COL_G + 1) * W] = col(4).astype(_bf16)
    seg = seg_ref[...]
    for d, c in ((0, COL_AQ), (1, COL_AK)):
        a = col(5 + d)
        hi, lo = _split2(a * a)
        msh = _dot(hi, seg) + _dot(lo, seg)
        proj_ref[:, c * W:(c + 1) * W] = (a * lax.rsqrt(msh + EPS) * qkw_ref[d:d + 1, :]).astype(_bf16)
    proj_ref[:, COL_AV * W:(COL_AV + 1) * W] = col(7).astype(_bf16)


def _in_proj(x, n1w, w_in_bf, lb, qkw, seg, tm):
    S, T, D = x.shape
    return pl.pallas_call(
        _in_proj_kernel,
        grid=(S, T // tm),
        in_specs=[
            pl.BlockSpec((None, tm, D), lambda s, i: (s, i, 0)),
            _const_spec((1, D)),
            _const_spec(w_in_bf.shape),
            _const_spec(lb.shape),
            _const_spec(qkw.shape),
            _const_spec(seg.shape),
        ],
        out_specs=[
            pl.BlockSpec((None, tm, 6 * HG_WIDTH), lambda s, i: (s, i, 0)),
            pl.BlockSpec((None, tm, 2 * HG_WIDTH), lambda s, i: (s, i, 0)),
        ],
        out_shape=[
            jax.ShapeDtypeStruct((S, T, 6 * HG_WIDTH), _bf16),
            jax.ShapeDtypeStruct((S, T, 2 * HG_WIDTH), _f32),
        ],
        compiler_params=_params("parallel", "parallel"),
        name="in_proj",
    )(x, n1w, w_in_bf, lb, qkw, seg)


def _hgrn_constants():
    C = HG_CHUNK
    idx = np.arange(C)
    dmats, masks = [], []
    for direction in (0, 1):
        if direction == 0:
            tri = (idx[None, :] <= idx[:, None]).astype(np.float64)
        else:
            tri = (idx[None, :] >= idx[:, None]).astype(np.float64)
        dm = [tri]
        mk = [np.eye(C)]
        for lvl in range(HG_LEVELS):
            h = 1 << lvl
            start = (idx // (2 * h)) * (2 * h)
            ref = start + h - 1 if direction == 0 else start + h
            gather = np.zeros((C, C))
            gather[idx, ref] = 1.0
            dm.append(tri - gather @ tri)
            upper = (idx & h) != 0
            later, earlier = (upper, ~upper) if direction == 0 else (~upper, upper)
            same = (idx[:, None] // (2 * h)) == (idx[None, :] // (2 * h))
            mk.append((same & later[:, None] & earlier[None, :]).astype(np.float64))
        dmats.append(np.concatenate(dm, axis=0))
        masks.append(np.stack(mk, axis=0))
    return np.stack(dmats, 0), np.stack(masks, 0)


def _hgrn_kernel(qf_ref, qb_ref, vf_ref, vb_ref, gf_ref, gb_ref, dmat_ref, mask_ref,
                 of_ref, ob_ref, state_ref, *, chunks):
    C = HG_CHUNK

    @pl.when(pl.program_id(1) == 0)
    def _():
        state_ref[...] = jnp.zeros_like(state_ref)

    def chunk_step(c, carry):
        for direction, (q_ref, v_ref, g_ref, o_ref) in enumerate(
                ((qf_ref, vf_ref, gf_ref, of_ref), (qb_ref, vb_ref, gb_ref, ob_ref))):
            cc = c if direction == 0 else chunks - 1 - c
            rows = pl.ds(pl.multiple_of(cc * C, C), C)
            g = g_ref[rows, :]
            g_hi, g_lo = _split2(g)
            dmat = dmat_ref[direction]
            delta_all = _dot(dmat, g_hi) + _dot(dmat, g_lo)
            last = C - 1 if direction == 0 else 0
            for hd in range(HG_HEADS):
                cols = slice(hd * HG_D, (hd + 1) * HG_D)
                q = q_ref[rows, cols].astype(_f32)
                v = v_ref[rows, cols]
                k = 1.0 - jnp.exp(g[:, cols])
                b = delta_all[0:C, cols]
                scores = mask_ref[direction, 0] * _dot_nt(q.astype(_bf16), k.astype(_bf16))
                for lvl in range(HG_LEVELS):
                    d = delta_all[(lvl + 1) * C:(lvl + 2) * C, cols]
                    e = jnp.exp(-jnp.abs(d))
                    p = _dot_nt((q * e).astype(_bf16), (k * e).astype(_bf16))
                    scores = scores + mask_ref[direction, lvl + 1] * p
                st = state_ref[direction, hd]
                b_last = b[last:last + 1, :]
                inter = _dot_nt((q * jnp.exp(b)).astype(_bf16), st.astype(_bf16))
                intra = _dot(scores.astype(_bf16), v)
                o_ref[rows, cols] = inter + intra
                k_dec = (k * jnp.exp(b_last - b)).astype(_bf16)
                state_ref[direction, hd] = jnp.exp(b_last) * st + _dot_tn(v, k_dec)
        return carry

    lax.fori_loop(0, chunks, chunk_step, 0)


def _hgrn2(proj, gates, dmat, mask, tb):
    S, T, _ = proj.shape
    n = T // tb
    W = HG_WIDTH
    fwd = lambda col: (lambda s, i: (s, i, col))
    bwd = lambda col: (lambda s, i: (s, n - 1 - i, col))
    blk = lambda im: pl.BlockSpec((None, tb, W), im)
    return pl.pallas_call(
        functools.partial(_hgrn_kernel, chunks=tb // HG_CHUNK),
        grid=(S, n),
        in_specs=[
            blk(fwd(COL_Q)), blk(bwd(COL_Q)), blk(fwd(COL_V)), blk(bwd(COL_V)),
            blk(fwd(0)), blk(bwd(1)),
            _const_spec(dmat.shape), _const_spec(mask.shape),
        ],
        out_specs=[blk(fwd(0)), blk(bwd(0))],
        out_shape=[jax.ShapeDtypeStruct((S, T, W), _f32)] * 2,
        scratch_shapes=[pltpu.VMEM((2, HG_HEADS, HG_D, HG_D), _f32)],
        compiler_params=_params("parallel", "arbitrary"),
        name="hgrn2",
    )(proj, proj, proj, proj, gates, gates, dmat, mask)


def _natten_bias(rpb):
    W = GRID_W
    case = np.arange(NA_KH)[:, None]
    a = np.arange(NA_KH)[None, :]
    dr = (a - case + (NA_KH - 1)).reshape(-1)
    sel_r = (dr[:, None] == np.arange(2 * NA_KH - 1)[None, :]).astype(np.float32)
    qc = np.arange(W)[:, None]
    kc = np.arange(W)[None, :]
    cs = np.clip(qc - NA_KW // 2, 0, W - NA_KW)
    valid = (kc >= cs) & (kc < cs + NA_KW)
    dc = np.clip(kc - qc + (NA_KW - 1), 0, 2 * NA_KW - 2).reshape(-1)
    sel_c = (dc[:, None] == np.arange(2 * NA_KW - 1)[None, :]).astype(np.float32)
    bias = jnp.einsum("hrc,sr,qc->hsq", rpb.astype(_f32), sel_r, sel_c, precision=lax.Precision.HIGHEST)
    bias = bias.reshape(NA_HEADS, NA_KH, NA_KH, W, W)
    bias = jnp.where(valid[None, None, None], bias, NEG_BIG)
    bias = bias.transpose(1, 0, 3, 2, 4)
    return bias.reshape(NA_KH, NA_HEADS // 2, 2 * W, NA_KH * W)


def _natten_kernel(q_ref, kp_ref, kc_ref, kn_ref, vp_ref, vc_ref, vn_ref, bias_ref, ow_ref, seg_ref,
                   o_ref, kbuf, vbuf, *, grid_rows):
    R = NA_ROWS_PER_STEP
    W = GRID_W
    blk = R * W
    for j, (k_r, v_r) in enumerate(((kp_ref, vp_ref), (kc_ref, vc_ref), (kn_ref, vn_ref))):
        kbuf[j * blk:(j + 1) * blk, :] = k_r[...]
        vbuf[j * blk:(j + 1) * blk, :] = v_r[...]
    r0 = pl.program_id(1) * R
    lane = lax.broadcasted_iota(jnp.int32, (W, LANE), 1)
    first = lane < NA_DH
    seg = seg_ref[...]

    def row_step(i, carry):
        r = r0 + i
        rs = jnp.clip(r - NA_KH // 2, 0, grid_rows - NA_KH)
        case = r - rs
        off = pl.multiple_of((rs - (r0 - R)) * W, W)
        qrows = pl.ds(pl.multiple_of(i * W, W), W)
        outs = []
        for p in range(NA_HEADS // 2):
            cols = slice(p * LANE, (p + 1) * LANE)
            qp = q_ref[qrows, cols]
            zero = jnp.zeros_like(qp)
            lhs = jnp.concatenate([jnp.where(first, qp, zero), jnp.where(first, zero, qp)], axis=0)
            s = _dot_nt(lhs, kbuf[pl.ds(off, NA_KH * W), cols]) + bias_ref[case, p]
            m = jnp.max(s, axis=-1, keepdims=True)
            e = jnp.exp(s - m)
            l = jnp.sum(e, axis=-1, keepdims=True)
            pv = _dot(e.astype(_bf16), vbuf[pl.ds(off, NA_KH * W), cols]) / l
            outs.append(jnp.where(first, pv[0:W], pv[W:2 * W]))
        out = jnp.concatenate(outs, axis=-1)
        hi, lo = _split2(out * out)
        msh = _dot(hi, seg) + _dot(lo, seg)
        o_ref[qrows, :] = (out * lax.rsqrt(msh + EPS) * ow_ref[...]).astype(_bf16)
        return carry

    lax.fori_loop(0, R, row_step, 0)


def _natten(proj, bias, out_w, seg):
    S, T, _ = proj.shape
    rows = T // GRID_W
    assert rows >= NA_KH and rows % NA_ROWS_PER_STEP == 0
    nb = rows // NA_ROWS_PER_STEP
    blk_t = NA_ROWS_PER_STEP * GRID_W
    Wd = NA_WIDTH
    cur = lambda col: (lambda s, i: (s, i, col))
    prv = lambda col: (lambda s, i: (s, jnp.maximum(i - 1, 0), col))
    nxt = lambda col: (lambda s, i: (s, jnp.minimum(i + 1, nb - 1), col))
    blk = lambda im: pl.BlockSpec((None, blk_t, Wd), im)
    return pl.pallas_call(
        functools.partial(_natten_kernel, grid_rows=rows),
        grid=(S, nb),
        in_specs=[
            blk(cur(COL_AQ)),
            blk(prv(COL_AK)), blk(cur(COL_AK)), blk(nxt(COL_AK)),
            blk(prv(COL_AV)), blk(cur(COL_AV)), blk(nxt(COL_AV)),
            _const_spec(bias.shape), _const_spec(out_w.shape), _const_spec(seg.shape),
        ],
        out_specs=blk(cur(0)),
        out_shape=jax.ShapeDtypeStruct((S, T, Wd), _bf16),
        scratch_shapes=[pltpu.VMEM((3 * blk_t, Wd), _bf16), pltpu.VMEM((3 * blk_t, Wd), _bf16)],
        compiler_params=_params("parallel", "parallel"),
        name="natten",
    )(proj, proj, proj, proj, proj, proj, proj, bias, out_w, seg)


def _out_proj_kernel(x_ref, of_ref, ob_ref, hg_ref, na_ref, w_ref, hw_ref, n2_ref, rw_ref, rb_ref, tril_ref,
                     x1_ref, h2_ref, route_ref, w1_ref, w2_ref, cnt_ref, carry_ref, *, steps_per_tile):
    @pl.when(pl.program_id(1) % steps_per_tile == 0)
    def _():
        carry_ref[...] = jnp.zeros_like(carry_ref)

    o = of_ref[...] + ob_ref[...]
    parts = []
    for hd in range(HG_HEADS):
        oh = o[:, hd * HG_D:(hd + 1) * HG_D]
        ms = jnp.mean(oh * oh, axis=-1, keepdims=True)
        parts.append(oh * lax.rsqrt(ms + EPS))
    on = jnp.concatenate(parts, axis=-1) * hw_ref[...]
    sig = 1.0 / (1.0 + jnp.exp(-hg_ref[...].astype(_f32)))
    mix_a = (on * sig).astype(_bf16)
    acc = _dot(mix_a, w_ref[0:HG_WIDTH, :]) + _dot(na_ref[...], w_ref[HG_WIDTH:, :])
    x1 = x_ref[...] + acc
    x1_ref[...] = x1
    ms2 = jnp.mean(x1 * x1, axis=-1, keepdims=True)
    h2 = x1 * lax.rsqrt(ms2 + EPS) * n2_ref[...]
    h2_ref[...] = h2

    h_hi, h_lo = _split2(h2)
    w_hi = rw_ref[0]
    w_lo = rw_ref[1]
    lg = _dot(h_hi, w_hi) + _dot(h_lo, w_hi) + _dot(h_hi, w_lo) + rb_ref[...]
    lane_i = lax.broadcasted_iota(jnp.int32, lg.shape, 1)
    lane = lane_i.astype(_f32)
    lane_grp = ((lane_i - EXPERT_LANE0) >> 3).astype(_f32)

    def first_argmax(vals, valid):
        masked = jnp.where(valid, vals, NEG_BIG)
        top = jnp.max(masked, axis=-1, keepdims=True)
        idx = jnp.min(jnp.where(valid & (masked == top), lane, float(ROUTER_LANES)), axis=-1, keepdims=True)
        return top, idx

    is_grp = lane_i < N_GROUPS
    g_top, g_idx = first_argmax(lg, is_grp)
    p_top = 1.0 / jnp.sum(jnp.where(is_grp, jnp.exp(lg - g_top), 0.0), axis=-1, keepdims=True)
    in_grp = (lane_i >= EXPERT_LANE0) & (lane_i < EXPERT_LANE0 + N_EXPERTS) & (lane_grp == g_idx)
    t1, i1 = first_argmax(lg, in_grp)
    t2, i2 = first_argmax(lg, in_grp & (lane != i1))
    e2 = jnp.exp(t2 - t1)
    w1 = p_top / (1.0 + e2)
    w2 = p_top * e2 / (1.0 + e2)
    w1_ref[...] = jnp.broadcast_to(w1, w1_ref.shape)
    w2_ref[...] = jnp.broadcast_to(w2, w2_ref.shape)

    sel1 = lane == i1
    sel2 = lane == i2
    onehot = jnp.where(sel1 | sel2, 1.0, 0.0)
    before = _dot(tril_ref[...], onehot.astype(_bf16)) + carry_ref[...]
    rank1 = jnp.sum(jnp.where(sel1, before, 0.0), axis=-1, keepdims=True)
    rank2 = jnp.sum(jnp.where(sel2, before, 0.0), axis=-1, keepdims=True)
    carry = carry_ref[...] + jnp.sum(onehot, axis=0, keepdims=True)
    carry_ref[...] = carry
    cnt_ref[...] = jnp.broadcast_to(carry, cnt_ref.shape)
    route_ref[...] = jnp.where(lane_i == 0, i1 - EXPERT_LANE0,
                     jnp.where(lane_i == 1, i2 - EXPERT_LANE0,
                     jnp.where(lane_i == 2, rank1, jnp.where(lane_i == 3, rank2, 0.0))))


def _out_proj(x, o_f, o_b, proj, na, w_out_bf, hw, n2w, rw, rb, tril, tm, moe_tm):
    S, T, D = x.shape
    W = HG_WIDTH
    n = T // tm
    row = lambda width, col=0: pl.BlockSpec((None, tm, width), lambda s, i: (s, i, col))
    lanes = jax.ShapeDtypeStruct((S, T, ROUTER_LANES), _f32)
    return pl.pallas_call(
        functools.partial(_out_proj_kernel, steps_per_tile=moe_tm // tm),
        grid=(S, n),
        in_specs=[
            row(D), row(W), row(W), row(W, COL_G), row(W),
            _const_spec(w_out_bf.shape), _const_spec(hw.shape), _const_spec(n2w.shape),
            _const_spec(rw.shape), _const_spec(rb.shape), _const_spec(tril.shape),
        ],
        out_specs=[row(D), row(D), row(ROUTER_LANES), row(ROUTER_LANES), row(ROUTER_LANES),
                   pl.BlockSpec((None, None, 8, ROUTER_LANES), lambda s, i: (s, i, 0, 0))],
        out_shape=[
            jax.ShapeDtypeStruct((S, T, D), _f32),
            jax.ShapeDtypeStruct((S, T, D), _f32),
            lanes, lanes, lanes,
            jax.ShapeDtypeStruct((S, n, 8, ROUTER_LANES), _f32),
        ],
        scratch_shapes=[pltpu.VMEM((1, ROUTER_LANES), _f32)],
        compiler_params=_params("parallel", "arbitrary"),
        name="out_proj",
    )(x, o_f, o_b, proj, na, w_out_bf, hw, n2w, rw, rb, tril)


MOE_WINDOW = 256
MOE_SEG_ALIGN = 8
MOE_UNROLL = 8


def _moe_rows(tm):
    return 2 * tm + N_EXPERTS * MOE_SEG_ALIGN + MOE_WINDOW


def _moe_kernel(pos_ref, seg_ref, x1_ref, h2_ref, w1_ref, w2_ref, wg_ref, wu_ref, wd_ref, y_ref, buf_ref, *, tm):
    i = pl.program_id(0)
    e = pl.program_id(1)
    D = D_MODEL

    def positions(t):
        p = pos_ref[i * tm + t]
        return p & 0xFFFF, p >> 16

    @pl.when(e == 0)
    def _():
        buf_ref[...] = jnp.zeros_like(buf_ref)

        def dispatch(blk, carry):
            for k in range(MOE_UNROLL):
                t = blk * MOE_UNROLL + k
                p1, p2 = positions(t)
                row = h2_ref[pl.ds(t, 1), :]
                buf_ref[pl.ds(p1, 1), :] = row
                buf_ref[pl.ds(p2, 1), :] = row
            return carry

        lax.fori_loop(0, tm // MOE_UNROLL, dispatch, 0)

    start = seg_ref[(i * N_EXPERTS + e) * 2]
    count = seg_ref[(i * N_EXPERTS + e) * 2 + 1]

    def window(j, carry):
        r0 = pl.multiple_of(start + j * MOE_WINDOW, MOE_SEG_ALIGN)
        rows = pl.ds(r0, MOE_WINDOW)
        xs = buf_ref[rows, :]
        xb = xs.astype(_bf16)
        a = _dot(xb, wg_ref[...])
        u = _dot(xb, wu_ref[...])
        hmid = a * (1.0 / (1.0 + jnp.exp(-a))) * u
        out = _dot(hmid.astype(_bf16), wd_ref[...])
        ridx = lax.broadcasted_iota(jnp.int32, (MOE_WINDOW, D), 0) + j * MOE_WINDOW
        buf_ref[rows, :] = jnp.where(ridx < count, out, xs)
        return carry

    lax.fori_loop(0, (count + MOE_WINDOW - 1) // MOE_WINDOW, window, 0)

    @pl.when(e == N_EXPERTS - 1)
    def _():
        def combine(blk, carry):
            for k in range(MOE_UNROLL):
                t = blk * MOE_UNROLL + k
                p1, p2 = positions(t)
                g1 = jnp.concatenate([w1_ref[pl.ds(t, 1), :]] * (D // ROUTER_LANES), axis=-1)
                g2 = jnp.concatenate([w2_ref[pl.ds(t, 1), :]] * (D // ROUTER_LANES), axis=-1)
                y_ref[pl.ds(t, 1), :] = (x1_ref[pl.ds(t, 1), :] + g1 * buf_ref[pl.ds(p1, 1), :]
                                         + g2 * buf_ref[pl.ds(p2, 1), :])
            return carry

        lax.fori_loop(0, tm // MOE_UNROLL, combine, 0)


def _moe(pos, seg, x1, h2, w1, w2, wg, wu, wd, tm):
    N, D = x1.shape
    once = pl.Buffered(1)
    tile = lambda width: pl.BlockSpec((tm, width), lambda i, e, *_: (i, 0), pipeline_mode=once)
    wspec = lambda a, b: pl.BlockSpec((None, a, b), lambda i, e, *_: (e, 0, 0))
    return pl.pallas_call(
        functools.partial(_moe_kernel, tm=tm),
        grid_spec=pltpu.PrefetchScalarGridSpec(
            num_scalar_prefetch=2,
            grid=(N // tm, N_EXPERTS),
            in_specs=[
                tile(D), tile(D), tile(ROUTER_LANES), tile(ROUTER_LANES),
                wspec(D, EXPERT_FF), wspec(D, EXPERT_FF), wspec(EXPERT_FF, D),
            ],
            out_specs=pl.BlockSpec((tm, D), lambda i, e, *_: (i, 0), pipeline_mode=once),
            scratch_shapes=[pltpu.VMEM((_moe_rows(tm), D), _f32)],
        ),
        out_shape=jax.ShapeDtypeStruct((N, D), _f32),
        compiler_params=_params("parallel", "arbitrary"),
        name="moe",
    )(pos, seg, x1, h2, w1, w2, wg, wu, wd)


def _moe_plan(route, cnt, tm, steps_per_tile):
    N = route.shape[0]
    n_tiles = N // tm
    counts = cnt.reshape(-1, 8, ROUTER_LANES)[steps_per_tile - 1::steps_per_tile, 0,
                                               EXPERT_LANE0:EXPERT_LANE0 + N_EXPERTS].astype(jnp.int32)
    seg_len = (counts + MOE_SEG_ALIGN - 1) // MOE_SEG_ALIGN * MOE_SEG_ALIGN
    seg_start = jnp.cumsum(seg_len, axis=1) - seg_len
    r = route[:, :4].astype(jnp.int32).reshape(n_tiles, tm, 4)
    experts = jnp.arange(N_EXPERTS, dtype=jnp.int32)

    def row_of(expert, rank):
        onehot = expert[..., None] == experts
        return jnp.sum(jnp.where(onehot, seg_start[:, None, :], 0), axis=-1) + rank

    pos = row_of(r[..., 0], r[..., 2]) | (row_of(r[..., 1], r[..., 3]) << 16)
    seg = jnp.stack([seg_start, counts], axis=-1)
    return pos.reshape(-1), seg.reshape(-1)


def _lower_bound(logits, l):
    return jnp.cumsum(jax.nn.softmax(logits.astype(_f32), axis=0), axis=0)[l]


ROW_TILE = 512
HGRN_BLOCK = 512
MOE_TILE = 2048


def _layer(x, l, c):
    S, T, D = x.shape
    tm = min(ROW_TILE, T)
    moe_tm = min(MOE_TILE, T)
    proj, gates = _in_proj(x, c["n1w"][l], c["w_in"][l], c["lb"][l], c["qkw"][l], c["seg"], tm)
    o_f, o_b = _hgrn2(proj, gates, c["dmat"], c["mask"], min(HGRN_BLOCK, T))
    na = _natten(proj, c["na_bias"][l], c["na_ow"][l], c["seg"])
    x1, h2, route, w1, w2, cnt = _out_proj(x, o_f, o_b, proj, na, c["w_out"][l], c["hw"][l], c["n2w"][l],
                                           c["rw"][l], c["rb"][l], c["tril"][:tm, :tm], tm, moe_tm)
    N = S * T
    flat = lambda a: a.reshape(N, a.shape[-1])
    pos, seg = _moe_plan(flat(route), cnt, moe_tm, moe_tm // tm)
    y = _moe(pos, seg, flat(x1), flat(h2), flat(w1), flat(w2), c["wg"][l], c["wu"][l], c["wd"][l], moe_tm)
    return y.reshape(S, T, D)


def _prepare(norm1_w, w_in, lb_fwd_logits, lb_bwd_logits, hgrn_norm_w, q_norm_w, k_norm_w, na_rpb,
             na_out_norm_w, w_out, norm2_w, w_router_group, b_router_group, w_router_expert,
             b_router_expert, w_expert_gate, w_expert_up, w_expert_down):
    depth = w_in.shape[0]
    dmat, mask = _hgrn_constants()
    head = np.arange(NA_WIDTH) // NA_DH
    seg = (head[:, None] == head[None, :]).astype(np.float32) / NA_DH
    pad = ROUTER_LANES - N_GROUPS - N_EXPERTS
    rw = jnp.pad(jnp.concatenate([w_router_group, w_router_expert], axis=-1).astype(_f32),
                 ((0, 0), (0, 0), (0, pad)))
    rw_hi = rw.astype(_bf16)
    rw_lo = (rw - rw_hi.astype(_f32)).astype(_bf16)
    rb = jnp.pad(jnp.concatenate([b_router_group, b_router_expert], axis=-1).astype(_f32),
                 ((0, 0), (0, pad)))
    return {
        "n1w": norm1_w.astype(_f32)[:, None, :],
        "w_in": w_in.astype(_bf16),
        "lb": jnp.stack([jnp.stack([_lower_bound(lb_fwd_logits, l), _lower_bound(lb_bwd_logits, l)])
                         for l in range(depth)]),
        "qkw": jnp.stack([jnp.tile(q_norm_w.astype(_f32), (1, NA_HEADS)) * (NA_DH ** -0.5),
                          jnp.tile(k_norm_w.astype(_f32), (1, NA_HEADS))], axis=1),
        "seg": jnp.asarray(seg, _bf16),
        "dmat": jnp.asarray(dmat, _bf16),
        "mask": jnp.asarray(mask, _f32),
        "na_bias": jnp.stack([_natten_bias(na_rpb[l]) for l in range(depth)]),
        "na_ow": na_out_norm_w.astype(_f32)[:, None, :],
        "w_out": w_out.astype(_bf16),
        "hw": hgrn_norm_w.astype(_f32)[:, None, :],
        "n2w": norm2_w.astype(_f32)[:, None, :],
        "rw": jnp.stack([rw_hi, rw_lo], axis=1),
        "rb": rb[:, None, :],
        "tril": jnp.asarray(np.tril(np.ones((ROW_TILE, ROW_TILE), np.float32), -1), _bf16),
        "wg": w_expert_gate.astype(_bf16),
        "wu": w_expert_up.astype(_bf16),
        "wd": w_expert_down.astype(_bf16),
    }


def kernel(x_prompt, x_sample, norm1_w, w_in, lb_fwd_logits, lb_bwd_logits, hgrn_norm_w, q_norm_w, k_norm_w, na_rpb, na_out_norm_w, w_out, norm2_w, w_router_group, b_router_group, w_router_expert, b_router_expert, w_expert_gate, w_expert_up, w_expert_down):
    c = _prepare(norm1_w, w_in, lb_fwd_logits, lb_bwd_logits, hgrn_norm_w, q_norm_w, k_norm_w, na_rpb,
                 na_out_norm_w, w_out, norm2_w, w_router_group, b_router_group, w_router_expert,
                 b_router_expert, w_expert_gate, w_expert_up, w_expert_down)
    outs = []
    for x in (x_prompt, x_sample):
        for l in range(w_in.shape[0]):
            x = _layer(x, l, c)
        outs.append(x)
    return tuple(outs)
```

```python
import functools

import numpy as np
import jax
import jax.numpy as jnp
from jax import lax
from jax.experimental import pallas as pl
from jax.experimental.pallas import tpu as pltpu

D_MODEL = 1024
EPS = 1e-6
GRID_W = 64
HG_HEADS = 4
HG_D = 128
HG_WIDTH = HG_HEADS * HG_D
HG_CHUNK = 64
HG_LEVELS = 6
HG_MATMUL_LEVELS = 3
NA_HEADS = 8
NA_DH = 64
NA_WIDTH = NA_HEADS * NA_DH
NA_KH = 8
NA_KW = 16
NA_ROWS_PER_STEP = 8
N_GROUPS = 4
EXPERTS_PER_GROUP = 8
N_EXPERTS = N_GROUPS * EXPERTS_PER_GROUP
EXPERT_FF = 512
ROUTER_LANES = 128
EXPERT_LANE0 = N_GROUPS
NEG_BIG = -1e30
LANE = 128
TOKEN_SUBLANES = D_MODEL // LANE
VMEM_LIMIT = 56 * 1024 * 1024

COL_Q, COL_V, COL_G, COL_AQ, COL_AK, COL_AV = range(6)

_f32 = jnp.float32
_bf16 = jnp.bfloat16


def _dot(a, b):
    return jnp.dot(a, b, preferred_element_type=_f32)


def _dot_nt(a, b):
    return lax.dot_general(a, b, (((1,), (1,)), ((), ())), preferred_element_type=_f32)


def _dot_tn(a, b):
    return lax.dot_general(a, b, (((0,), (0,)), ((), ())), preferred_element_type=_f32)


def _split2(x):
    hi = x.astype(_bf16)
    lo = (x - hi.astype(_f32)).astype(_bf16)
    return hi, lo


def _params(*sem):
    return pltpu.CompilerParams(dimension_semantics=sem, vmem_limit_bytes=VMEM_LIMIT)


def _const_spec(shape):
    nd = len(shape)
    return pl.BlockSpec(shape, lambda *_: (0,) * nd)


def _in_proj_kernel(x_ref, n1_ref, w_ref, lb_ref, qkw_ref, seg_ref, proj_ref, gate_ref):
    x = x_ref[...]
    ms = jnp.mean(x * x, axis=-1, keepdims=True)
    h = (x * lax.rsqrt(ms + EPS) * n1_ref[...]).astype(_bf16)
    W = HG_WIDTH

    def col(j):
        return _dot(h, w_ref[:, j * W:(j + 1) * W])

    proj_ref[:, COL_Q * W:(COL_Q + 1) * W] = col(0).astype(_bf16)
    for d in range(2):
        z = col(1 + d)
        lb = lb_ref[d:d + 1, :]
        f = lb + (1.0 - lb) * (1.0 / (1.0 + jnp.exp(-z)))
        gate_ref[:, d * W:(d + 1) * W] = jnp.log(f)
    proj_ref[:, COL_V * W:(COL_V + 1) * W] = col(3).astype(_bf16)
    proj_ref[:, COL_G * W:(COL_G + 1) * W] = col(4).astype(_bf16)
    seg = seg_ref[...]
    for d, c in ((0, COL_AQ), (1, COL_AK)):
        a = col(5 + d)
        hi, lo = _split2(a * a)
        msh = _dot(hi, seg) + _dot(lo, seg)
        proj_ref[:, c * W:(c + 1) * W] = (a * lax.rsqrt(msh + EPS) * qkw_ref[d:d + 1, :]).astype(_bf16)
    proj_ref[:, COL_AV * W:(COL_AV + 1) * W] = col(7).astype(_bf16)


def _in_proj(x, n1w, w_in_bf, lb, qkw, seg, tm):
    S, T, D = x.shape
    return pl.pallas_call(
        _in_proj_kernel,
        grid=(S, T // tm),
        in_specs=[
            pl.BlockSpec((None, tm, D), lambda s, i: (s, i, 0)),
            _const_spec((1, D)),
            _const_spec(w_in_bf.shape),
            _const_spec(lb.shape),
            _const_spec(qkw.shape),
            _const_spec(seg.shape),
        ],
        out_specs=[
            pl.BlockSpec((None, tm, 6 * HG_WIDTH), lambda s, i: (s, i, 0)),
            pl.BlockSpec((None, tm, 2 * HG_WIDTH), lambda s, i: (s, i, 0)),
        ],
        out_shape=[
            jax.ShapeDtypeStruct((S, T, 6 * HG_WIDTH), _bf16),
            jax.ShapeDtypeStruct((S, T, 2 * HG_WIDTH), _f32),
        ],
        compiler_params=_params("parallel", "parallel"),
        name="in_proj",
    )(x, n1w, w_in_bf, lb, qkw, seg)


def _hgrn_constants():
    C = HG_CHUNK
    idx = np.arange(C)
    dmats, masks = [], []
    for direction in (0, 1):
        if direction == 0:
            tri = (idx[None, :] <= idx[:, None]).astype(np.float64)
        else:
            tri = (idx[None, :] >= idx[:, None]).astype(np.float64)
        dm = [tri]
        mk = [np.eye(C)]
        for lvl in range(HG_LEVELS):
            h = 1 << lvl
            start = (idx // (2 * h)) * (2 * h)
            ref = start + h - 1 if direction == 0 else start + h
            if lvl < HG_MATMUL_LEVELS:
                gather = np.zeros((C, C))
                gather[idx, ref] = 1.0
                dm.append(tri - gather @ tri)
            upper = (idx & h) != 0
            later, earlier = (upper, ~upper) if direction == 0 else (~upper, upper)
            same = (idx[:, None] // (2 * h)) == (idx[None, :] // (2 * h))
            mk.append((same & later[:, None] & earlier[None, :]).astype(np.float64))
        dmats.append(np.concatenate(dm, axis=0))
        masks.append(np.stack(mk, axis=0))
    return np.stack(dmats, 0), np.stack(masks, 0)


def _hgrn_kernel(qf_ref, qb_ref, vf_ref, vb_ref, gf_ref, gb_ref, dmat_ref, mask_ref,
                 of_ref, ob_ref, state_ref, *, chunks):
    C = HG_CHUNK

    @pl.when(pl.program_id(1) == 0)
    def _():
        state_ref[...] = jnp.zeros_like(state_ref)

    def chunk_step(c, carry):
        for direction, (q_ref, v_ref, g_ref, o_ref) in enumerate(
                ((qf_ref, vf_ref, gf_ref, of_ref), (qb_ref, vb_ref, gb_ref, ob_ref))):
            cc = c if direction == 0 else chunks - 1 - c
            rows = pl.ds(pl.multiple_of(cc * C, C), C)
            g = g_ref[rows, :]
            g_hi, g_lo = _split2(g)
            dmat = dmat_ref[direction]
            delta_all = _dot(dmat, g_hi) + _dot(dmat, g_lo)
            last = C - 1 if direction == 0 else 0
            for hd in range(HG_HEADS):
                cols = slice(hd * HG_D, (hd + 1) * HG_D)
                q = q_ref[rows, cols].astype(_f32)
                v = v_ref[rows, cols]
                k = 1.0 - jnp.exp(g[:, cols])
                b = delta_all[0:C, cols]
                scores = mask_ref[direction, 0] * _dot_nt(q.astype(_bf16), k.astype(_bf16))
                for lvl in range(HG_LEVELS):
                    if lvl < HG_MATMUL_LEVELS:
                        d = delta_all[(lvl + 1) * C:(lvl + 2) * C, cols]
                    else:
                        h = 1 << lvl
                        ref0 = h - 1 if direction == 0 else h
                        d = b - jnp.concatenate(
                            [jnp.broadcast_to(b[s + ref0:s + ref0 + 1, :], (2 * h, HG_D))
                             for s in range(0, C, 2 * h)], axis=0)
                    e = jnp.exp(-jnp.abs(d))
                    p = _dot_nt((q * e).astype(_bf16), (k * e).astype(_bf16))
                    scores = scores + mask_ref[direction, lvl + 1] * p
                st = state_ref[direction, hd]
                b_last = b[last:last + 1, :]
                inter = _dot_nt((q * jnp.exp(b)).astype(_bf16), st.astype(_bf16))
                intra = _dot(scores.astype(_bf16), v)
                o_ref[rows, cols] = inter + intra
                k_dec = (k * jnp.exp(b_last - b)).astype(_bf16)
                state_ref[direction, hd] = jnp.exp(b_last) * st + _dot_tn(v, k_dec)
        return carry

    lax.fori_loop(0, chunks, chunk_step, 0, unroll=2)


def _hgrn2(proj, gates, dmat, mask, tb):
    S, T, _ = proj.shape
    n = T // tb
    W = HG_WIDTH
    fwd = lambda col: (lambda s, i: (s, i, col))
    bwd = lambda col: (lambda s, i: (s, n - 1 - i, col))
    blk = lambda im: pl.BlockSpec((None, tb, W), im)
    return pl.pallas_call(
        functools.partial(_hgrn_kernel, chunks=tb // HG_CHUNK),
        grid=(S, n),
        in_specs=[
            blk(fwd(COL_Q)), blk(bwd(COL_Q)), blk(fwd(COL_V)), blk(bwd(COL_V)),
            blk(fwd(0)), blk(bwd(1)),
            _const_spec(dmat.shape), _const_spec(mask.shape),
        ],
        out_specs=[blk(fwd(0)), blk(bwd(0))],
        out_shape=[jax.ShapeDtypeStruct((S, T, W), _f32)] * 2,
        scratch_shapes=[pltpu.VMEM((2, HG_HEADS, HG_D, HG_D), _f32)],
        compiler_params=_params("parallel", "arbitrary"),
        name="hgrn2",
    )(proj, proj, proj, proj, gates, gates, dmat, mask)


def _natten_bias(rpb):
    W = GRID_W
    case = np.arange(NA_KH)[:, None]
    a = np.arange(NA_KH)[None, :]
    dr = (a - case + (NA_KH - 1)).reshape(-1)
    sel_r = (dr[:, None] == np.arange(2 * NA_KH - 1)[None, :]).astype(np.float32)
    qc = np.arange(W)[:, None]
    kc = np.arange(W)[None, :]
    cs = np.clip(qc - NA_KW // 2, 0, W - NA_KW)
    valid = (kc >= cs) & (kc < cs + NA_KW)
    dc = np.clip(kc - qc + (NA_KW - 1), 0, 2 * NA_KW - 2).reshape(-1)
    sel_c = (dc[:, None] == np.arange(2 * NA_KW - 1)[None, :]).astype(np.float32)
    bias = jnp.einsum("hrc,sr,qc->hsq", rpb.astype(_f32), sel_r, sel_c, precision=lax.Precision.HIGHEST)
    bias = bias.reshape(NA_HEADS, NA_KH, NA_KH, W, W)
    bias = jnp.where(valid[None, None, None], bias, NEG_BIG)
    bias = bias.transpose(1, 0, 3, 2, 4)
    return bias.reshape(NA_KH, NA_HEADS // 2, 2 * W, NA_KH * W)


def _natten_kernel(q_ref, kp_ref, kc_ref, kn_ref, vp_ref, vc_ref, vn_ref, bias_ref, ow_ref, seg_ref,
                   o_ref, kbuf, vbuf, *, grid_rows):
    R = NA_ROWS_PER_STEP
    W = GRID_W
    blk = R * W
    for j, (k_r, v_r) in enumerate(((kp_ref, vp_ref), (kc_ref, vc_ref), (kn_ref, vn_ref))):
        kbuf[j * blk:(j + 1) * blk, :] = k_r[...]
        vbuf[j * blk:(j + 1) * blk, :] = v_r[...]
    r0 = pl.program_id(1) * R
    lane = lax.broadcasted_iota(jnp.int32, (W, LANE), 1)
    first = lane < NA_DH
    seg = seg_ref[...]

    def row_step(i, carry):
        r = r0 + i
        rs = jnp.clip(r - NA_KH // 2, 0, grid_rows - NA_KH)
        case = r - rs
        off = pl.multiple_of((rs - (r0 - R)) * W, W)
        qrows = pl.ds(pl.multiple_of(i * W, W), W)
        outs = []
        for p in range(NA_HEADS // 2):
            cols = slice(p * LANE, (p + 1) * LANE)
            qp = q_ref[qrows, cols]
            zero = jnp.zeros_like(qp)
            lhs = jnp.concatenate([jnp.where(first, qp, zero), jnp.where(first, zero, qp)], axis=0)
            s = _dot_nt(lhs, kbuf[pl.ds(off, NA_KH * W), cols]) + bias_ref[case, p]
            m = jnp.max(s, axis=-1, keepdims=True)
            e = jnp.exp(s - m)
            l = jnp.sum(e, axis=-1, keepdims=True)
            pv = _dot(e.astype(_bf16), vbuf[pl.ds(off, NA_KH * W), cols]) / l
            outs.append(jnp.where(first, pv[0:W], pv[W:2 * W]))
        out = jnp.concatenate(outs, axis=-1)
        hi, lo = _split2(out * out)
        msh = _dot(hi, seg) + _dot(lo, seg)
        o_ref[qrows, :] = (out * lax.rsqrt(msh + EPS) * ow_ref[...]).astype(_bf16)
        return carry

    lax.fori_loop(0, R, row_step, 0, unroll=2)


def _natten(proj, bias, out_w, seg):
    S, T, _ = proj.shape
    rows = T // GRID_W
    assert rows >= NA_KH and rows % NA_ROWS_PER_STEP == 0
    nb = rows // NA_ROWS_PER_STEP
    blk_t = NA_ROWS_PER_STEP * GRID_W
    Wd = NA_WIDTH
    cur = lambda col: (lambda s, i: (s, i, col))
    prv = lambda col: (lambda s, i: (s, jnp.maximum(i - 1, 0), col))
    nxt = lambda col: (lambda s, i: (s, jnp.minimum(i + 1, nb - 1), col))
    blk = lambda im: pl.BlockSpec((None, blk_t, Wd), im)
    return pl.pallas_call(
        functools.partial(_natten_kernel, grid_rows=rows),
        grid=(S, nb),
        in_specs=[
            blk(cur(COL_AQ)),
            blk(prv(COL_AK)), blk(cur(COL_AK)), blk(nxt(COL_AK)),
            blk(prv(COL_AV)), blk(cur(COL_AV)), blk(nxt(COL_AV)),
            _const_spec(bias.shape), _const_spec(out_w.shape), _const_spec(seg.shape),
        ],
        out_specs=blk(cur(0)),
        out_shape=jax.ShapeDtypeStruct((S, T, Wd), _bf16),
        scratch_shapes=[pltpu.VMEM((3 * blk_t, Wd), _bf16), pltpu.VMEM((3 * blk_t, Wd), _bf16)],
        compiler_params=_params("parallel", "parallel"),
        name="natten",
    )(proj, proj, proj, proj, proj, proj, proj, bias, out_w, seg)


def _out_proj_kernel(x_ref, of_ref, ob_ref, hg_ref, na_ref, w_ref, hw_ref, n2_ref, rw_ref, rb_ref, tril_ref,
                     x1_ref, h2_ref, route_ref, w1_ref, w2_ref, cnt_ref, carry_ref, *, steps_per_tile):
    @pl.when(pl.program_id(1) % steps_per_tile == 0)
    def _():
        carry_ref[...] = jnp.zeros_like(carry_ref)

    o = of_ref[...] + ob_ref[...]
    parts = []
    for hd in range(HG_HEADS):
        oh = o[:, hd * HG_D:(hd + 1) * HG_D]
        ms = jnp.mean(oh * oh, axis=-1, keepdims=True)
        parts.append(oh * lax.rsqrt(ms + EPS))
    on = jnp.concatenate(parts, axis=-1) * hw_ref[...]
    sig = 1.0 / (1.0 + jnp.exp(-hg_ref[...].astype(_f32)))
    mix_a = (on * sig).astype(_bf16)
    acc = _dot(mix_a, w_ref[0:HG_WIDTH, :]) + _dot(na_ref[...], w_ref[HG_WIDTH:, :])
    x1 = x_ref[...] + acc
    x1_ref[...] = x1
    ms2 = jnp.mean(x1 * x1, axis=-1, keepdims=True)
    h2 = x1 * lax.rsqrt(ms2 + EPS) * n2_ref[...]
    for c in range(TOKEN_SUBLANES):
        h2_ref[pl.ds(c, h2.shape[0], stride=TOKEN_SUBLANES), :] = h2[:, c * LANE:(c + 1) * LANE]

    h_hi, h_lo = _split2(h2)
    w_hi = rw_ref[0]
    w_lo = rw_ref[1]
    lg = _dot(h_hi, w_hi) + _dot(h_lo, w_hi) + _dot(h_hi, w_lo) + rb_ref[...]
    lane_i = lax.broadcasted_iota(jnp.int32, lg.shape, 1)
    lane = lane_i.astype(_f32)
    lane_grp = ((lane_i - EXPERT_LANE0) >> 3).astype(_f32)

    def first_argmax(vals, valid):
        masked = jnp.where(valid, vals, NEG_BIG)
        top = jnp.max(masked, axis=-1, keepdims=True)
        idx = jnp.min(jnp.where(valid & (masked == top), lane, float(ROUTER_LANES)), axis=-1, keepdims=True)
        return top, idx

    is_grp = lane_i < N_GROUPS
    g_top, g_idx = first_argmax(lg, is_grp)
    p_top = 1.0 / jnp.sum(jnp.where(is_grp, jnp.exp(lg - g_top), 0.0), axis=-1, keepdims=True)
    in_grp = (lane_i >= EXPERT_LANE0) & (lane_i < EXPERT_LANE0 + N_EXPERTS) & (lane_grp == g_idx)
    t1, i1 = first_argmax(lg, in_grp)
    t2, i2 = first_argmax(lg, in_grp & (lane != i1))
    e2 = jnp.exp(t2 - t1)
    w1 = p_top / (1.0 + e2)
    w2 = p_top * e2 / (1.0 + e2)
    w1_ref[...] = jnp.broadcast_to(w1, w1_ref.shape)
    w2_ref[...] = jnp.broadcast_to(w2, w2_ref.shape)

    sel1 = lane == i1
    sel2 = lane == i2
    onehot = jnp.where(sel1 | sel2, 1.0, 0.0)
    before = _dot(tril_ref[...], onehot.astype(_bf16)) + carry_ref[...]
    rank1 = jnp.sum(jnp.where(sel1, before, 0.0), axis=-1, keepdims=True)
    rank2 = jnp.sum(jnp.where(sel2, before, 0.0), axis=-1, keepdims=True)
    carry = carry_ref[...] + jnp.sum(onehot, axis=0, keepdims=True)
    carry_ref[...] = carry
    cnt_ref[...] = jnp.broadcast_to(carry, cnt_ref.shape)
    route_ref[...] = jnp.where(lane_i == 0, i1 - EXPERT_LANE0,
                     jnp.where(lane_i == 1, i2 - EXPERT_LANE0,
                     jnp.where(lane_i == 2, rank1, jnp.where(lane_i == 3, rank2, 0.0))))


def _out_proj(x, o_f, o_b, proj, na, w_out_bf, hw, n2w, rw, rb, tril, tm, moe_tm):
    S, T, D = x.shape
    W = HG_WIDTH
    n = T // tm
    row = lambda width, col=0: pl.BlockSpec((None, tm, width), lambda s, i: (s, i, col))
    lanes = jax.ShapeDtypeStruct((S, T, ROUTER_LANES), _f32)
    return pl.pallas_call(
        functools.partial(_out_proj_kernel, steps_per_tile=moe_tm // tm),
        grid=(S, n),
        in_specs=[
            row(D), row(W), row(W), row(W, COL_G), row(W),
            _const_spec(w_out_bf.shape), _const_spec(hw.shape), _const_spec(n2w.shape),
            _const_spec(rw.shape), _const_spec(rb.shape), _const_spec(tril.shape),
        ],
        out_specs=[row(D), pl.BlockSpec((None, tm * TOKEN_SUBLANES, LANE), lambda s, i: (s, i, 0)),
                   row(ROUTER_LANES), row(ROUTER_LANES), row(ROUTER_LANES),
                   pl.BlockSpec((None, None, 8, ROUTER_LANES), lambda s, i: (s, i, 0, 0))],
        out_shape=[
            jax.ShapeDtypeStruct((S, T, D), _f32),
            jax.ShapeDtypeStruct((S, T * TOKEN_SUBLANES, LANE), _f32),
            lanes, lanes, lanes,
            jax.ShapeDtypeStruct((S, n, 8, ROUTER_LANES), _f32),
        ],
        scratch_shapes=[pltpu.VMEM((1, ROUTER_LANES), _f32)],
        compiler_params=_params("parallel", "arbitrary"),
        name="out_proj",
    )(x, o_f, o_b, proj, na, w_out_bf, hw, n2w, rw, rb, tril)


MOE_WINDOWS = (128, 256)
MOE_UNROLL = 8
MOE_COMBINE_BLOCK = 256


def _moe_rows(tm):
    return 2 * tm + MOE_WINDOWS[-1]


def _moe_kernel(pos_ref, seg_ref, x1_ref, h2_ref, w1_ref, w2_ref, wg_ref, wu_ref, wd_ref, y_ref,
                buf_ref, mix_ref, *, tm, cb):
    i = pl.program_id(0)
    e = pl.program_id(1)
    TS = TOKEN_SUBLANES

    def positions(t):
        p = pos_ref[i * tm + t]
        return p & 0xFFFF, p >> 16

    def token_tile(row):
        return pl.ds(pl.multiple_of(row * TS, TS), TS)

    @pl.when(e == 0)
    def _():
        buf_ref[...] = jnp.zeros_like(buf_ref)

        def dispatch(blk, carry):
            for k in range(MOE_UNROLL):
                t = blk * MOE_UNROLL + k
                p1, p2 = positions(t)
                tok = h2_ref[token_tile(t), :]
                buf_ref[token_tile(p1), :] = tok
                buf_ref[token_tile(p2), :] = tok
            return carry

        lax.fori_loop(0, tm // MOE_UNROLL, dispatch, 0)

    start = seg_ref[(i * N_EXPERTS + e) * 2]
    count = seg_ref[(i * N_EXPERTS + e) * 2 + 1]

    def window(rows_w, j):
        r0 = start + j * rows_w
        chunk = lambda c: pl.ds(r0 * TS + c, rows_w, stride=TS)
        xs = [buf_ref[chunk(c), :] for c in range(TS)]
        xb = jnp.concatenate(xs, axis=-1).astype(_bf16)
        a = _dot(xb, wg_ref[...])
        u = _dot(xb, wu_ref[...])
        hmid = a * (1.0 / (1.0 + jnp.exp(-a))) * u
        out = _dot(hmid.astype(_bf16), wd_ref[...])
        valid = lax.broadcasted_iota(jnp.int32, (rows_w, LANE), 0) + j * rows_w < count
        for c in range(TS):
            buf_ref[chunk(c), :] = jnp.where(valid, out[:, c * LANE:(c + 1) * LANE], xs[c])

    small, large = MOE_WINDOWS

    @pl.when((count > 0) & (count <= small))
    def _():
        window(small, 0)

    @pl.when(count > small)
    def _():
        def body(j, carry):
            window(large, j)
            return carry

        lax.fori_loop(0, (count + large - 1) // large, body, 0)

    @pl.when(e == N_EXPERTS - 1)
    def _():
        def combine(blk, carry):
            def tokens(grp, c2):
                for k in range(MOE_UNROLL):
                    tl = grp * MOE_UNROLL + k
                    t = blk * cb + tl
                    p1, p2 = positions(t)
                    g1 = jnp.broadcast_to(w1_ref[pl.ds(t, 1), :], (TS, LANE))
                    g2 = jnp.broadcast_to(w2_ref[pl.ds(t, 1), :], (TS, LANE))
                    mix_ref[token_tile(tl), :] = (g1 * buf_ref[token_tile(p1), :]
                                                  + g2 * buf_ref[token_tile(p2), :])
                return c2

            lax.fori_loop(0, cb // MOE_UNROLL, tokens, 0)
            rows = pl.ds(pl.multiple_of(blk * cb, cb), cb)
            for c in range(TS):
                cols = slice(c * LANE, (c + 1) * LANE)
                y_ref[rows, cols] = x1_ref[rows, cols] + mix_ref[pl.ds(c, cb, stride=TS), :]
            return carry

        lax.fori_loop(0, tm // cb, combine, 0)


def _moe(pos, seg, x1, h2, w1, w2, wg, wu, wd, tm):
    N, D = x1.shape
    once = pl.Buffered(1)
    tile = lambda rows, width: pl.BlockSpec((rows, width), lambda i, e, *_: (i, 0), pipeline_mode=once)
    wspec = lambda a, b: pl.BlockSpec((None, a, b), lambda i, e, *_: (e, 0, 0))
    cb = min(MOE_COMBINE_BLOCK, tm)
    return pl.pallas_call(
        functools.partial(_moe_kernel, tm=tm, cb=cb),
        grid_spec=pltpu.PrefetchScalarGridSpec(
            num_scalar_prefetch=2,
            grid=(N // tm, N_EXPERTS),
            in_specs=[
                tile(tm, D), tile(tm * TOKEN_SUBLANES, LANE), tile(tm, ROUTER_LANES), tile(tm, ROUTER_LANES),
                wspec(D, EXPERT_FF), wspec(D, EXPERT_FF), wspec(EXPERT_FF, D),
            ],
            out_specs=tile(tm, D),
            scratch_shapes=[pltpu.VMEM((_moe_rows(tm) * TOKEN_SUBLANES, LANE), _f32),
                            pltpu.VMEM((cb * TOKEN_SUBLANES, LANE), _f32)],
        ),
        out_shape=jax.ShapeDtypeStruct((N, D), _f32),
        compiler_params=_params("parallel", "arbitrary"),
        name="moe",
    )(pos, seg, x1, h2, w1, w2, wg, wu, wd)


def _moe_plan(route, cnt, tm, steps_per_tile):
    N = route.shape[0]
    n_tiles = N // tm
    counts = cnt.reshape(-1, 8, ROUTER_LANES)[steps_per_tile - 1::steps_per_tile, 0,
                                               EXPERT_LANE0:EXPERT_LANE0 + N_EXPERTS].astype(jnp.int32)
    seg_start = jnp.cumsum(counts, axis=1) - counts
    r = route[:, :4].astype(jnp.int32).reshape(n_tiles, tm, 4)
    experts = jnp.arange(N_EXPERTS, dtype=jnp.int32)

    def row_of(expert, rank):
        onehot = expert[..., None] == experts
        return jnp.sum(jnp.where(onehot, seg_start[:, None, :], 0), axis=-1) + rank

    pos = row_of(r[..., 0], r[..., 2]) | (row_of(r[..., 1], r[..., 3]) << 16)
    seg = jnp.stack([seg_start, counts], axis=-1)
    return pos.reshape(-1), seg.reshape(-1)


def _lower_bound(logits, l):
    return jnp.cumsum(jax.nn.softmax(logits.astype(_f32), axis=0), axis=0)[l]


ROW_TILE = 512
HGRN_BLOCK = 512
MOE_TILE = 2048


def _layer(x, l, c):
    S, T, D = x.shape
    tm = min(ROW_TILE, T)
    moe_tm = min(MOE_TILE, T)
    proj, gates = _in_proj(x, c["n1w"][l], c["w_in"][l], c["lb"][l], c["qkw"][l], c["seg"], tm)
    o_f, o_b = _hgrn2(proj, gates, c["dmat"], c["mask"], min(HGRN_BLOCK, T))
    na = _natten(proj, c["na_bias"][l], c["na_ow"][l], c["seg"])
    x1, h2, route, w1, w2, cnt = _out_proj(x, o_f, o_b, proj, na, c["w_out"][l], c["hw"][l], c["n2w"][l],
                                           c["rw"][l], c["rb"][l], c["tril"][:tm, :tm], tm, moe_tm)
    N = S * T
    flat = lambda a: a.reshape(N, a.shape[-1])
    pos, seg = _moe_plan(flat(route), cnt, moe_tm, moe_tm // tm)
    y = _moe(pos, seg, flat(x1), h2.reshape(N * TOKEN_SUBLANES, LANE), flat(w1), flat(w2),
             c["wg"][l], c["wu"][l], c["wd"][l], moe_tm)
    return y.reshape(S, T, D)


def _prepare(norm1_w, w_in, lb_fwd_logits, lb_bwd_logits, hgrn_norm_w, q_norm_w, k_norm_w, na_rpb,
             na_out_norm_w, w_out, norm2_w, w_router_group, b_router_group, w_router_expert,
             b_router_expert, w_expert_gate, w_expert_up, w_expert_down):
    depth = w_in.shape[0]
    dmat, mask = _hgrn_constants()
    head = np.arange(NA_WIDTH) // NA_DH
    seg = (head[:, None] == head[None, :]).astype(np.float32) / NA_DH
    pad = ROUTER_LANES - N_GROUPS - N_EXPERTS
    rw = jnp.pad(jnp.concatenate([w_router_group, w_router_expert], axis=-1).astype(_f32),
                 ((0, 0), (0, 0), (0, pad)))
    rw_hi = rw.astype(_bf16)
    rw_lo = (rw - rw_hi.astype(_f32)).astype(_bf16)
    rb = jnp.pad(jnp.concatenate([b_router_group, b_router_expert], axis=-1).astype(_f32),
                 ((0, 0), (0, pad)))
    return {
        "n1w": norm1_w.astype(_f32)[:, None, :],
        "w_in": w_in.astype(_bf16),
        "lb": jnp.stack([jnp.stack([_lower_bound(lb_fwd_logits, l), _lower_bound(lb_bwd_logits, l)])
                         for l in range(depth)]),
        "qkw": jnp.stack([jnp.tile(q_norm_w.astype(_f32), (1, NA_HEADS)) * (NA_DH ** -0.5),
                          jnp.tile(k_norm_w.astype(_f32), (1, NA_HEADS))], axis=1),
        "seg": jnp.asarray(seg, _bf16),
        "dmat": jnp.asarray(dmat, _bf16),
        "mask": jnp.asarray(mask, _f32),
        "na_bias": jnp.stack([_natten_bias(na_rpb[l]) for l in range(depth)]),
        "na_ow": na_out_norm_w.astype(_f32)[:, None, :],
        "w_out": w_out.astype(_bf16),
        "hw": hgrn_norm_w.astype(_f32)[:, None, :],
        "n2w": norm2_w.astype(_f32)[:, None, :],
        "rw": jnp.stack([rw_hi, rw_lo], axis=1),
        "rb": rb[:, None, :],
        "tril": jnp.asarray(np.tril(np.ones((ROW_TILE, ROW_TILE), np.float32), -1), _bf16),
        "wg": w_expert_gate.astype(_bf16),
        "wu": w_expert_up.astype(_bf16),
        "wd": w_expert_down.astype(_bf16),
    }


def kernel(x_prompt, x_sample, norm1_w, w_in, lb_fwd_logits, lb_bwd_logits, hgrn_norm_w, q_norm_w, k_norm_w, na_rpb, na_out_norm_w, w_out, norm2_w, w_router_group, b_router_group, w_router_expert, b_router_expert, w_expert_gate, w_expert_up, w_expert_down):
    c = _prepare(norm1_w, w_in, lb_fwd_logits, lb_bwd_logits, hgrn_norm_w, q_norm_w, k_norm_w, na_rpb,
                 na_out_norm_w, w_out, norm2_w, w_router_group, b_router_group, w_router_expert,
                 b_router_expert, w_expert_gate, w_expert_up, w_expert_down)
    outs = []
    for x in (x_prompt, x_sample):
        for l in range(w_in.shape[0]):
            x = _layer(x, l, c)
        outs.append(x)
    return tuple(outs)
```

```python
import functools

import numpy as np
import jax
import jax.numpy as jnp
from jax import lax
from jax.experimental import pallas as pl
from jax.experimental.pallas import tpu as pltpu

D_MODEL = 1024
EPS = 1e-6
GRID_W = 64
HG_HEADS = 4
HG_D = 128
HG_WIDTH = HG_HEADS * HG_D
HG_CHUNK = 64
HG_LEVELS = 6
HG_MATMUL_LEVELS = 3
NA_HEADS = 8
NA_DH = 64
NA_WIDTH = NA_HEADS * NA_DH
NA_KH = 8
NA_KW = 16
NA_ROWS_PER_STEP = 8
NA_ROWS_PER_ITER = 2
N_GROUPS = 4
EXPERTS_PER_GROUP = 8
N_EXPERTS = N_GROUPS * EXPERTS_PER_GROUP
EXPERT_FF = 512
ROUTER_LANES = 128
EXPERT_LANE0 = N_GROUPS
NEG_BIG = -1e30
LANE = 128
TOKEN_SUBLANES = D_MODEL // LANE
VMEM_LIMIT = 56 * 1024 * 1024

COL_Q, COL_V, COL_G, COL_AQ, COL_AK, COL_AV = range(6)

_f32 = jnp.float32
_bf16 = jnp.bfloat16


def _dot(a, b):
    return jnp.dot(a, b, preferred_element_type=_f32)


def _dot_nt(a, b):
    return lax.dot_general(a, b, (((1,), (1,)), ((), ())), preferred_element_type=_f32)


def _dot_tn(a, b):
    return lax.dot_general(a, b, (((0,), (0,)), ((), ())), preferred_element_type=_f32)


def _split2(x):
    hi = x.astype(_bf16)
    lo = (x - hi.astype(_f32)).astype(_bf16)
    return hi, lo


def _head_pair_rsqrt(a):
    first = lax.broadcasted_iota(jnp.int32, a.shape, 1) < NA_DH
    sq = a * a
    s_first = jnp.sum(jnp.where(first, sq, 0.0), axis=-1, keepdims=True)
    s_second = jnp.sum(jnp.where(first, 0.0, sq), axis=-1, keepdims=True)
    return lax.rsqrt(jnp.where(first, s_first, s_second) * (1.0 / NA_DH) + EPS)


def _params(*sem):
    return pltpu.CompilerParams(dimension_semantics=sem, vmem_limit_bytes=VMEM_LIMIT)


def _const_spec(shape):
    nd = len(shape)
    return pl.BlockSpec(shape, lambda *_: (0,) * nd)


def _in_proj_kernel(x_ref, n1_ref, w_ref, lb_ref, qkw_ref, proj_ref, gate_ref):
    x = x_ref[...]
    ms = jnp.mean(x * x, axis=-1, keepdims=True)
    h = (x * lax.rsqrt(ms + EPS) * n1_ref[...]).astype(_bf16)
    W = HG_WIDTH

    def col(j):
        return _dot(h, w_ref[:, j * W:(j + 1) * W])

    proj_ref[:, COL_Q * W:(COL_Q + 1) * W] = col(0).astype(_bf16)
    for d in range(2):
        z = col(1 + d)
        lb = lb_ref[d:d + 1, :]
        f = lb + (1.0 - lb) * (1.0 / (1.0 + jnp.exp(-z)))
        gate_ref[:, d * W:(d + 1) * W] = jnp.log(f)
    proj_ref[:, COL_V * W:(COL_V + 1) * W] = col(3).astype(_bf16)
    proj_ref[:, COL_G * W:(COL_G + 1) * W] = col(4).astype(_bf16)
    for d, c in ((0, COL_AQ), (1, COL_AK)):
        a = col(5 + d)
        for p in range(NA_HEADS // 2):
            ap = a[:, p * LANE:(p + 1) * LANE]
            lanes = slice(c * W + p * LANE, c * W + (p + 1) * LANE)
            proj_ref[:, lanes] = (ap * _head_pair_rsqrt(ap)
                                  * qkw_ref[d:d + 1, p * LANE:(p + 1) * LANE]).astype(_bf16)
    proj_ref[:, COL_AV * W:(COL_AV + 1) * W] = col(7).astype(_bf16)


def _in_proj(x, n1w, w_in_bf, lb, qkw, tm):
    S, T, D = x.shape
    return pl.pallas_call(
        _in_proj_kernel,
        grid=(S, T // tm),
        in_specs=[
            pl.BlockSpec((None, tm, D), lambda s, i: (s, i, 0)),
            _const_spec((1, D)),
            _const_spec(w_in_bf.shape),
            _const_spec(lb.shape),
            _const_spec(qkw.shape),
        ],
        out_specs=[
            pl.BlockSpec((None, tm, 6 * HG_WIDTH), lambda s, i: (s, i, 0)),
            pl.BlockSpec((None, tm, 2 * HG_WIDTH), lambda s, i: (s, i, 0)),
        ],
        out_shape=[
            jax.ShapeDtypeStruct((S, T, 6 * HG_WIDTH), _bf16),
            jax.ShapeDtypeStruct((S, T, 2 * HG_WIDTH), _f32),
        ],
        compiler_params=_params("parallel", "parallel"),
        name="in_proj",
    )(x, n1w, w_in_bf, lb, qkw)


def _hgrn_constants():
    C = HG_CHUNK
    idx = np.arange(C)
    dmats, masks = [], []
    for direction in (0, 1):
        if direction == 0:
            tri = (idx[None, :] <= idx[:, None]).astype(np.float64)
        else:
            tri = (idx[None, :] >= idx[:, None]).astype(np.float64)
        dm = [tri]
        mk = [np.eye(C)]
        for lvl in range(HG_LEVELS):
            h = 1 << lvl
            start = (idx // (2 * h)) * (2 * h)
            ref = start + h - 1 if direction == 0 else start + h
            if lvl < HG_MATMUL_LEVELS:
                gather = np.zeros((C, C))
                gather[idx, ref] = 1.0
                dm.append(tri - gather @ tri)
            upper = (idx & h) != 0
            later, earlier = (upper, ~upper) if direction == 0 else (~upper, upper)
            same = (idx[:, None] // (2 * h)) == (idx[None, :] // (2 * h))
            mk.append((same & later[:, None] & earlier[None, :]).astype(np.float64))
        dmats.append(np.concatenate(dm, axis=0))
        masks.append(np.stack(mk, axis=0))
    return np.stack(dmats, 0), np.stack(masks, 0)


def _hgrn_kernel(qf_ref, qb_ref, vf_ref, vb_ref, gf_ref, gb_ref, dmat_ref, mask_ref,
                 of_ref, ob_ref, state_ref, *, chunks):
    C = HG_CHUNK

    @pl.when(pl.program_id(1) == 0)
    def _():
        state_ref[...] = jnp.zeros_like(state_ref)

    def chunk_step(c, carry):
        deltas = []
        for direction, g_ref in enumerate((gf_ref, gb_ref)):
            cc = c if direction == 0 else chunks - 1 - c
            rows = pl.ds(pl.multiple_of(cc * C, C), C)
            g = g_ref[rows, :]
            g_hi, g_lo = _split2(g)
            dmat = dmat_ref[direction]
            deltas.append((rows, g, _dot(dmat, g_hi) + _dot(dmat, g_lo)))

        units = []
        for direction, (q_ref, v_ref, o_ref) in enumerate(((qf_ref, vf_ref, of_ref), (qb_ref, vb_ref, ob_ref))):
            rows, g, delta_all = deltas[direction]
            last = C - 1 if direction == 0 else 0
            for hd in range(HG_HEADS):
                cols = slice(hd * HG_D, (hd + 1) * HG_D)
                q = q_ref[rows, cols].astype(_f32)
                v = v_ref[rows, cols]
                k = 1.0 - jnp.exp(g[:, cols])
                b = delta_all[0:C, cols]
                scores = mask_ref[direction, 0] * _dot_nt(q.astype(_bf16), k.astype(_bf16))
                for lvl in range(HG_LEVELS):
                    if lvl < HG_MATMUL_LEVELS:
                        d = delta_all[(lvl + 1) * C:(lvl + 2) * C, cols]
                    else:
                        h = 1 << lvl
                        ref0 = h - 1 if direction == 0 else h
                        d = b - jnp.concatenate(
                            [jnp.broadcast_to(b[s + ref0:s + ref0 + 1, :], (2 * h, HG_D))
                             for s in range(0, C, 2 * h)], axis=0)
                    e = jnp.exp(-jnp.abs(d))
                    p = _dot_nt((q * e).astype(_bf16), (k * e).astype(_bf16))
                    scores = scores + mask_ref[direction, lvl + 1] * p
                st = state_ref[direction, hd]
                b_last = b[last:last + 1, :]
                inter = _dot_nt((q * jnp.exp(b)).astype(_bf16), st.astype(_bf16))
                k_dec = (k * jnp.exp(b_last - b)).astype(_bf16)
                state_ref[direction, hd] = jnp.exp(b_last) * st + _dot_tn(v, k_dec)
                units.append((o_ref, rows, cols, scores, v, inter))

        for o_ref, rows, cols, scores, v, inter in units:
            o_ref[rows, cols] = inter + _dot(scores.astype(_bf16), v)
        return carry

    lax.fori_loop(0, chunks, chunk_step, 0, unroll=2)


def _hgrn2(proj, gates, dmat, mask, tb):
    S, T, _ = proj.shape
    n = T // tb
    W = HG_WIDTH
    fwd = lambda col: (lambda s, i: (s, i, col))
    bwd = lambda col: (lambda s, i: (s, n - 1 - i, col))
    blk = lambda im: pl.BlockSpec((None, tb, W), im)
    return pl.pallas_call(
        functools.partial(_hgrn_kernel, chunks=tb // HG_CHUNK),
        grid=(S, n),
        in_specs=[
            blk(fwd(COL_Q)), blk(bwd(COL_Q)), blk(fwd(COL_V)), blk(bwd(COL_V)),
            blk(fwd(0)), blk(bwd(1)),
            _const_spec(dmat.shape), _const_spec(mask.shape),
        ],
        out_specs=[blk(fwd(0)), blk(bwd(0))],
        out_shape=[jax.ShapeDtypeStruct((S, T, W), _f32)] * 2,
        scratch_shapes=[pltpu.VMEM((2, HG_HEADS, HG_D, HG_D), _f32)],
        compiler_params=_params("parallel", "arbitrary"),
        name="hgrn2",
    )(proj, proj, proj, proj, gates, gates, dmat, mask)


def _natten_bias(rpb):
    W = GRID_W
    case = np.arange(NA_KH)[:, None]
    a = np.arange(NA_KH)[None, :]
    dr = (a - case + (NA_KH - 1)).reshape(-1)
    sel_r = (dr[:, None] == np.arange(2 * NA_KH - 1)[None, :]).astype(np.float32)
    qc = np.arange(W)[:, None]
    kc = np.arange(W)[None, :]
    cs = np.clip(qc - NA_KW // 2, 0, W - NA_KW)
    valid = (kc >= cs) & (kc < cs + NA_KW)
    dc = np.clip(kc - qc + (NA_KW - 1), 0, 2 * NA_KW - 2).reshape(-1)
    sel_c = (dc[:, None] == np.arange(2 * NA_KW - 1)[None, :]).astype(np.float32)
    bias = jnp.einsum("hrc,sr,qc->hsq", rpb.astype(_f32), sel_r, sel_c, precision=lax.Precision.HIGHEST)
    bias = bias.reshape(NA_HEADS, NA_KH, NA_KH, W, W)
    bias = jnp.where(valid[None, None, None], bias, NEG_BIG)
    bias = bias.transpose(1, 0, 3, 2, 4)
    return bias.reshape(NA_KH, NA_HEADS // 2, 2 * W, NA_KH * W)


def _natten_kernel(q_ref, kp_ref, kc_ref, kn_ref, vp_ref, vc_ref, vn_ref, bias_ref, ow_ref,
                   o_ref, kbuf, vbuf, *, grid_rows):
    R = NA_ROWS_PER_STEP
    W = GRID_W
    blk = R * W
    for j, (k_r, v_r) in enumerate(((kp_ref, vp_ref), (kc_ref, vc_ref), (kn_ref, vn_ref))):
        kbuf[j * blk:(j + 1) * blk, :] = k_r[...]
        vbuf[j * blk:(j + 1) * blk, :] = v_r[...]
    r0 = pl.program_id(1) * R
    lane = lax.broadcasted_iota(jnp.int32, (W, LANE), 1)
    first = lane < NA_DH

    def row_group(ig, carry):
        units = []
        for k in range(NA_ROWS_PER_ITER):
            i = ig * NA_ROWS_PER_ITER + k
            r = r0 + i
            rs = jnp.clip(r - NA_KH // 2, 0, grid_rows - NA_KH)
            off = pl.multiple_of((rs - (r0 - R)) * W, W)
            for p in range(NA_HEADS // 2):
                units.append((pl.ds(pl.multiple_of(i * W, W), W), slice(p * LANE, (p + 1) * LANE),
                              pl.ds(off, NA_KH * W), r - rs, p))
        scores = []
        for qrows, cols, win, case, p in units:
            qp = q_ref[qrows, cols]
            zero = jnp.zeros_like(qp)
            lhs = jnp.concatenate([jnp.where(first, qp, zero), jnp.where(first, zero, qp)], axis=0)
            scores.append(_dot_nt(lhs, kbuf[win, cols]))
        probs = []
        for (qrows, cols, win, case, p), s in zip(units, scores):
            s = s + bias_ref[case, p]
            e = jnp.exp(s - jnp.max(s, axis=-1, keepdims=True))
            probs.append((e.astype(_bf16), jnp.sum(e, axis=-1, keepdims=True)))
        for (qrows, cols, win, case, p), (e, l) in zip(units, probs):
            pv = _dot(e, vbuf[win, cols]) / l
            out = jnp.where(first, pv[0:W], pv[W:2 * W])
            o_ref[qrows, cols] = (out * _head_pair_rsqrt(out) * ow_ref[:, cols]).astype(_bf16)
        return carry

    lax.fori_loop(0, R // NA_ROWS_PER_ITER, row_group, 0)


def _natten(proj, bias, out_w):
    S, T, _ = proj.shape
    rows = T // GRID_W
    assert rows >= NA_KH and rows % NA_ROWS_PER_STEP == 0
    nb = rows // NA_ROWS_PER_STEP
    blk_t = NA_ROWS_PER_STEP * GRID_W
    Wd = NA_WIDTH
    cur = lambda col: (lambda s, i: (s, i, col))
    prv = lambda col: (lambda s, i: (s, jnp.maximum(i - 1, 0), col))
    nxt = lambda col: (lambda s, i: (s, jnp.minimum(i + 1, nb - 1), col))
    blk = lambda im: pl.BlockSpec((None, blk_t, Wd), im)
    return pl.pallas_call(
        functools.partial(_natten_kernel, grid_rows=rows),
        grid=(S, nb),
        in_specs=[
            blk(cur(COL_AQ)),
            blk(prv(COL_AK)), blk(cur(COL_AK)), blk(nxt(COL_AK)),
            blk(prv(COL_AV)), blk(cur(COL_AV)), blk(nxt(COL_AV)),
            _const_spec(bias.shape), _const_spec(out_w.shape),
        ],
        out_specs=blk(cur(0)),
        out_shape=jax.ShapeDtypeStruct((S, T, Wd), _bf16),
        scratch_shapes=[pltpu.VMEM((3 * blk_t, Wd), _bf16), pltpu.VMEM((3 * blk_t, Wd), _bf16)],
        compiler_params=_params("parallel", "parallel"),
        name="natten",
    )(proj, proj, proj, proj, proj, proj, proj, bias, out_w)


def _out_proj_kernel(x_ref, of_ref, ob_ref, hg_ref, na_ref, w_ref, hw_ref, n2_ref, rw_ref, rb_ref, tril_ref,
                     x1_ref, h2_ref, route_ref, w1_ref, w2_ref, cnt_ref, carry_ref, *, steps_per_tile):
    @pl.when(pl.program_id(1) % steps_per_tile == 0)
    def _():
        carry_ref[...] = jnp.zeros_like(carry_ref)

    o = of_ref[...] + ob_ref[...]
    parts = []
    for hd in range(HG_HEADS):
        oh = o[:, hd * HG_D:(hd + 1) * HG_D]
        ms = jnp.mean(oh * oh, axis=-1, keepdims=True)
        parts.append(oh * lax.rsqrt(ms + EPS))
    on = jnp.concatenate(parts, axis=-1) * hw_ref[...]
    sig = 1.0 / (1.0 + jnp.exp(-hg_ref[...].astype(_f32)))
    mix_a = (on * sig).astype(_bf16)
    acc = _dot(mix_a, w_ref[0:HG_WIDTH, :]) + _dot(na_ref[...], w_ref[HG_WIDTH:, :])
    x1 = x_ref[...] + acc
    x1_ref[...] = x1
    ms2 = jnp.mean(x1 * x1, axis=-1, keepdims=True)
    h2 = x1 * lax.rsqrt(ms2 + EPS) * n2_ref[...]
    for c in range(TOKEN_SUBLANES):
        h2_ref[pl.ds(c, h2.shape[0], stride=TOKEN_SUBLANES), :] = h2[:, c * LANE:(c + 1) * LANE]

    h_hi, h_lo = _split2(h2)
    w_hi = rw_ref[0]
    w_lo = rw_ref[1]
    lg = _dot(h_hi, w_hi) + _dot(h_lo, w_hi) + _dot(h_hi, w_lo) + rb_ref[...]
    lane_i = lax.broadcasted_iota(jnp.int32, lg.shape, 1)
    lane = lane_i.astype(_f32)
    lane_grp = ((lane_i - EXPERT_LANE0) >> 3).astype(_f32)

    def first_argmax(vals, valid):
        masked = jnp.where(valid, vals, NEG_BIG)
        top = jnp.max(masked, axis=-1, keepdims=True)
        idx = jnp.min(jnp.where(valid & (masked == top), lane, float(ROUTER_LANES)), axis=-1, keepdims=True)
        return top, idx

    is_grp = lane_i < N_GROUPS
    g_top, g_idx = first_argmax(lg, is_grp)
    p_top = 1.0 / jnp.sum(jnp.where(is_grp, jnp.exp(lg - g_top), 0.0), axis=-1, keepdims=True)
    in_grp = (lane_i >= EXPERT_LANE0) & (lane_i < EXPERT_LANE0 + N_EXPERTS) & (lane_grp == g_idx)
    t1, i1 = first_argmax(lg, in_grp)
    t2, i2 = first_argmax(lg, in_grp & (lane != i1))
    e2 = jnp.exp(t2 - t1)
    w1 = p_top / (1.0 + e2)
    w2 = p_top * e2 / (1.0 + e2)
    w1_ref[...] = jnp.broadcast_to(w1, w1_ref.shape)
    w2_ref[...] = jnp.broadcast_to(w2, w2_ref.shape)

    sel1 = lane == i1
    sel2 = lane == i2
    onehot = jnp.where(sel1 | sel2, 1.0, 0.0)
    before = _dot(tril_ref[...], onehot.astype(_bf16)) + carry_ref[...]
    rank1 = jnp.sum(jnp.where(sel1, before, 0.0), axis=-1, keepdims=True)
    rank2 = jnp.sum(jnp.where(sel2, before, 0.0), axis=-1, keepdims=True)
    carry = carry_ref[...] + jnp.sum(onehot, axis=0, keepdims=True)
    carry_ref[...] = carry
    cnt_ref[...] = jnp.broadcast_to(carry, cnt_ref.shape)
    route_ref[...] = jnp.where(lane_i == 0, i1 - EXPERT_LANE0,
                     jnp.where(lane_i == 1, i2 - EXPERT_LANE0,
                     jnp.where(lane_i == 2, rank1, jnp.where(lane_i == 3, rank2, 0.0))))


def _out_proj(x, o_f, o_b, proj, na, w_out_bf, hw, n2w, rw, rb, tril, tm, moe_tm):
    S, T, D = x.shape
    W = HG_WIDTH
    n = T // tm
    row = lambda width, col=0: pl.BlockSpec((None, tm, width), lambda s, i: (s, i, col))
    lanes = jax.ShapeDtypeStruct((S, T, ROUTER_LANES), _f32)
    return pl.pallas_call(
        functools.partial(_out_proj_kernel, steps_per_tile=moe_tm // tm),
        grid=(S, n),
        in_specs=[
            row(D), row(W), row(W), row(W, COL_G), row(W),
            _const_spec(w_out_bf.shape), _const_spec(hw.shape), _const_spec(n2w.shape),
            _const_spec(rw.shape), _const_spec(rb.shape), _const_spec(tril.shape),
        ],
        out_specs=[row(D), pl.BlockSpec((None, tm * TOKEN_SUBLANES, LANE), lambda s, i: (s, i, 0)),
                   row(ROUTER_LANES), row(ROUTER_LANES), row(ROUTER_LANES),
                   pl.BlockSpec((None, None, 8, ROUTER_LANES), lambda s, i: (s, i, 0, 0))],
        out_shape=[
            jax.ShapeDtypeStruct((S, T, D), _f32),
            jax.ShapeDtypeStruct((S, T * TOKEN_SUBLANES, LANE), _f32),
            lanes, lanes, lanes,
            jax.ShapeDtypeStruct((S, n, 8, ROUTER_LANES), _f32),
        ],
        scratch_shapes=[pltpu.VMEM((1, ROUTER_LANES), _f32)],
        compiler_params=_params("parallel", "arbitrary"),
        name="out_proj",
    )(x, o_f, o_b, proj, na, w_out_bf, hw, n2w, rw, rb, tril)


MOE_WINDOWS = (128, 256)
MOE_UNROLL = 8
MOE_COMBINE_BLOCK = 256


def _moe_rows(tm):
    return 2 * tm + MOE_WINDOWS[-1]


def _moe_kernel(pos_ref, seg_ref, x1_ref, h2_ref, w1_ref, w2_ref, wg_ref, wu_ref, wd_ref, y_ref,
                buf_ref, mix_ref, *, tm, cb):
    i = pl.program_id(0)
    e = pl.program_id(1)
    TS = TOKEN_SUBLANES

    def positions(t):
        p = pos_ref[i * tm + t]
        return p & 0xFFFF, p >> 16

    def token_tile(row):
        return pl.ds(pl.multiple_of(row * TS, TS), TS)

    @pl.when(e == 0)
    def _():
        buf_ref[...] = jnp.zeros_like(buf_ref)

        def dispatch(blk, carry):
            for k in range(MOE_UNROLL):
                t = blk * MOE_UNROLL + k
                p1, p2 = positions(t)
                tok = h2_ref[token_tile(t), :]
                buf_ref[token_tile(p1), :] = tok
                buf_ref[token_tile(p2), :] = tok
            return carry

        lax.fori_loop(0, tm // MOE_UNROLL, dispatch, 0)

    start = seg_ref[(i * N_EXPERTS + e) * 2]
    count = seg_ref[(i * N_EXPERTS + e) * 2 + 1]

    def window(rows_w, j):
        r0 = start + j * rows_w
        chunk = lambda c: pl.ds(r0 * TS + c, rows_w, stride=TS)
        xs = [buf_ref[chunk(c), :] for c in range(TS)]
        xb = jnp.concatenate(xs, axis=-1).astype(_bf16)
        a = _dot(xb, wg_ref[...])
        u = _dot(xb, wu_ref[...])
        hmid = a * (1.0 / (1.0 + jnp.exp(-a))) * u
        out = _dot(hmid.astype(_bf16), wd_ref[...])
        valid = lax.broadcasted_iota(jnp.int32, (rows_w, LANE), 0) + j * rows_w < count
        for c in range(TS):
            buf_ref[chunk(c), :] = jnp.where(valid, out[:, c * LANE:(c + 1) * LANE], xs[c])

    small, large = MOE_WINDOWS

    @pl.when((count > 0) & (count <= small))
    def _():
        window(small, 0)

    @pl.when(count > small)
    def _():
        def body(j, carry):
            window(large, j)
            return carry

        lax.fori_loop(0, (count + large - 1) // large, body, 0)

    @pl.when(e == N_EXPERTS - 1)
    def _():
        def combine(blk, carry):
            def tokens(grp, c2):
                for k in range(MOE_UNROLL):
                    tl = grp * MOE_UNROLL + k
                    t = blk * cb + tl
                    p1, p2 = positions(t)
                    g1 = jnp.broadcast_to(w1_ref[pl.ds(t, 1), :], (TS, LANE))
                    g2 = jnp.broadcast_to(w2_ref[pl.ds(t, 1), :], (TS, LANE))
                    mix_ref[token_tile(tl), :] = (g1 * buf_ref[token_tile(p1), :]
                                                  + g2 * buf_ref[token_tile(p2), :])
                return c2

            lax.fori_loop(0, cb // MOE_UNROLL, tokens, 0)
            rows = pl.ds(pl.multiple_of(blk * cb, cb), cb)
            for c in range(TS):
                cols = slice(c * LANE, (c + 1) * LANE)
                y_ref[rows, cols] = x1_ref[rows, cols] + mix_ref[pl.ds(c, cb, stride=TS), :]
            return carry

        lax.fori_loop(0, tm // cb, combine, 0)


def _moe(pos, seg, x1, h2, w1, w2, wg, wu, wd, tm):
    N, D = x1.shape
    once = pl.Buffered(1)
    tile = lambda rows, width: pl.BlockSpec((rows, width), lambda i, e, *_: (i, 0), pipeline_mode=once)
    wspec = lambda a, b: pl.BlockSpec((None, a, b), lambda i, e, *_: (e, 0, 0))
    cb = min(MOE_COMBINE_BLOCK, tm)
    return pl.pallas_call(
        functools.partial(_moe_kernel, tm=tm, cb=cb),
        grid_spec=pltpu.PrefetchScalarGridSpec(
            num_scalar_prefetch=2,
            grid=(N // tm, N_EXPERTS),
            in_specs=[
                tile(tm, D), tile(tm * TOKEN_SUBLANES, LANE), tile(tm, ROUTER_LANES), tile(tm, ROUTER_LANES),
                wspec(D, EXPERT_FF), wspec(D, EXPERT_FF), wspec(EXPERT_FF, D),
            ],
            out_specs=tile(tm, D),
            scratch_shapes=[pltpu.VMEM((_moe_rows(tm) * TOKEN_SUBLANES, LANE), _f32),
                            pltpu.VMEM((cb * TOKEN_SUBLANES, LANE), _f32)],
        ),
        out_shape=jax.ShapeDtypeStruct((N, D), _f32),
        compiler_params=_params("parallel", "arbitrary"),
        name="moe",
    )(pos, seg, x1, h2, w1, w2, wg, wu, wd)


def _moe_plan(route, cnt, tm, steps_per_tile):
    N = route.shape[0]
    n_tiles = N // tm
    counts = cnt.reshape(-1, 8, ROUTER_LANES)[steps_per_tile - 1::steps_per_tile, 0,
                                               EXPERT_LANE0:EXPERT_LANE0 + N_EXPERTS].astype(jnp.int32)
    seg_start = jnp.cumsum(counts, axis=1) - counts
    r = route[:, :4].astype(jnp.int32).reshape(n_tiles, tm, 4)
    experts = jnp.arange(N_EXPERTS, dtype=jnp.int32)

    def row_of(expert, rank):
        onehot = expert[..., None] == experts
        return jnp.sum(jnp.where(onehot, seg_start[:, None, :], 0), axis=-1) + rank

    pos = row_of(r[..., 0], r[..., 2]) | (row_of(r[..., 1], r[..., 3]) << 16)
    seg = jnp.stack([seg_start, counts], axis=-1)
    return pos.reshape(-1), seg.reshape(-1)


def _lower_bound(logits, l):
    return jnp.cumsum(jax.nn.softmax(logits.astype(_f32), axis=0), axis=0)[l]


ROW_TILE = 512
HGRN_BLOCK = 512
MOE_TILE = 2048


def _layer(x, l, c):
    S, T, D = x.shape
    tm = min(ROW_TILE, T)
    moe_tm = min(MOE_TILE, T)
    proj, gates = _in_proj(x, c["n1w"][l], c["w_in"][l], c["lb"][l], c["qkw"][l], tm)
    o_f, o_b = _hgrn2(proj, gates, c["dmat"], c["mask"], min(HGRN_BLOCK, T))
    na = _natten(proj, c["na_bias"][l], c["na_ow"][l])
    x1, h2, route, w1, w2, cnt = _out_proj(x, o_f, o_b, proj, na, c["w_out"][l], c["hw"][l], c["n2w"][l],
                                           c["rw"][l], c["rb"][l], c["tril"][:tm, :tm], tm, moe_tm)
    N = S * T
    flat = lambda a: a.reshape(N, a.shape[-1])
    pos, seg = _moe_plan(flat(route), cnt, moe_tm, moe_tm // tm)
    y = _moe(pos, seg, flat(x1), h2.reshape(N * TOKEN_SUBLANES, LANE), flat(w1), flat(w2),
             c["wg"][l], c["wu"][l], c["wd"][l], moe_tm)
    return y.reshape(S, T, D)


def _prepare(norm1_w, w_in, lb_fwd_logits, lb_bwd_logits, hgrn_norm_w, q_norm_w, k_norm_w, na_rpb,
             na_out_norm_w, w_out, norm2_w, w_router_group, b_router_group, w_router_expert,
             b_router_expert, w_expert_gate, w_expert_up, w_expert_down):
    depth = w_in.shape[0]
    dmat, mask = _hgrn_constants()
    pad = ROUTER_LANES - N_GROUPS - N_EXPERTS
    rw = jnp.pad(jnp.concatenate([w_router_group, w_router_expert], axis=-1).astype(_f32),
                 ((0, 0), (0, 0), (0, pad)))
    rw_hi = rw.astype(_bf16)
    rw_lo = (rw - rw_hi.astype(_f32)).astype(_bf16)
    rb = jnp.pad(jnp.concatenate([b_router_group, b_router_expert], axis=-1).astype(_f32),
                 ((0, 0), (0, pad)))
    return {
        "n1w": norm1_w.astype(_f32)[:, None, :],
        "w_in": w_in.astype(_bf16),
        "lb": jnp.stack([jnp.stack([_lower_bound(lb_fwd_logits, l), _lower_bound(lb_bwd_logits, l)])
                         for l in range(depth)]),
        "qkw": jnp.stack([jnp.tile(q_norm_w.astype(_f32), (1, NA_HEADS)) * (NA_DH ** -0.5),
                          jnp.tile(k_norm_w.astype(_f32), (1, NA_HEADS))], axis=1),
        "dmat": jnp.asarray(dmat, _bf16),
        "mask": jnp.asarray(mask, _f32),
        "na_bias": jnp.stack([_natten_bias(na_rpb[l]) for l in range(depth)]),
        "na_ow": na_out_norm_w.astype(_f32)[:, None, :],
        "w_out": w_out.astype(_bf16),
        "hw": hgrn_norm_w.astype(_f32)[:, None, :],
        "n2w": norm2_w.astype(_f32)[:, None, :],
        "rw": jnp.stack([rw_hi, rw_lo], axis=1),
        "rb": rb[:, None, :],
        "tril": jnp.asarray(np.tril(np.ones((ROW_TILE, ROW_TILE), np.float32), -1), _bf16),
        "wg": w_expert_gate.astype(_bf16),
        "wu": w_expert_up.astype(_bf16),
        "wd": w_expert_down.astype(_bf16),
    }


def kernel(x_prompt, x_sample, norm1_w, w_in, lb_fwd_logits, lb_bwd_logits, hgrn_norm_w, q_norm_w, k_norm_w, na_rpb, na_out_norm_w, w_out, norm2_w, w_router_group, b_router_group, w_router_expert, b_router_expert, w_expert_gate, w_expert_up, w_expert_down):
    c = _prepare(norm1_w, w_in, lb_fwd_logits, lb_bwd_logits, hgrn_norm_w, q_norm_w, k_norm_w, na_rpb,
                 na_out_norm_w, w_out, norm2_w, w_router_group, b_router_group, w_router_expert,
                 b_router_expert, w_expert_gate, w_expert_up, w_expert_down)
    outs = []
    for x in (x_prompt, x_sample):
        for l in range(w_in.shape[0]):
            x = _layer(x, l, c)
        outs.append(x)
    return tuple(outs)
```

```python
import functools

import numpy as np
import jax
import jax.numpy as jnp
from jax import lax
from jax.experimental import pallas as pl
from jax.experimental.pallas import tpu as pltpu

D_MODEL = 1024
EPS = 1e-6
GRID_W = 64
HG_HEADS = 4
HG_D = 128
HG_WIDTH = HG_HEADS * HG_D
HG_CHUNK = 64
HG_LEVELS = 6
HG_MATMUL_LEVELS = 3
NA_HEADS = 8
NA_DH = 64
NA_WIDTH = NA_HEADS * NA_DH
NA_KH = 8
NA_KW = 16
NA_ROWS_PER_STEP = 8
NA_ROWS_PER_ITER = 2
N_GROUPS = 4
EXPERTS_PER_GROUP = 8
N_EXPERTS = N_GROUPS * EXPERTS_PER_GROUP
EXPERT_FF = 512
ROUTER_LANES = 128
EXPERT_LANE0 = N_GROUPS
NEG_BIG = -1e30
LANE = 128
TOKEN_SUBLANES = D_MODEL // LANE
VMEM_LIMIT = 56 * 1024 * 1024

COL_Q, COL_V, COL_G, COL_AQ, COL_AK, COL_AV = range(6)

_f32 = jnp.float32
_bf16 = jnp.bfloat16


def _dot(a, b):
    return jnp.dot(a, b, preferred_element_type=_f32)


def _dot_nt(a, b):
    return lax.dot_general(a, b, (((1,), (1,)), ((), ())), preferred_element_type=_f32)


def _dot_tn(a, b):
    return lax.dot_general(a, b, (((0,), (0,)), ((), ())), preferred_element_type=_f32)


def _split2(x):
    hi = x.astype(_bf16)
    lo = (x - hi.astype(_f32)).astype(_bf16)
    return hi, lo


def _head_pair_rsqrt(a):
    first = lax.broadcasted_iota(jnp.int32, a.shape, 1) < NA_DH
    sq = a * a
    s_first = jnp.sum(jnp.where(first, sq, 0.0), axis=-1, keepdims=True)
    s_second = jnp.sum(jnp.where(first, 0.0, sq), axis=-1, keepdims=True)
    return lax.rsqrt(jnp.where(first, s_first, s_second) * (1.0 / NA_DH) + EPS)


def _params(*sem):
    return pltpu.CompilerParams(dimension_semantics=sem, vmem_limit_bytes=VMEM_LIMIT)


def _const_spec(shape):
    nd = len(shape)
    return pl.BlockSpec(shape, lambda *_: (0,) * nd)


def _in_proj_kernel(x_ref, n1_ref, w_ref, lb_ref, qkw_ref, proj_ref, gate_ref):
    x = x_ref[...]
    ms = jnp.mean(x * x, axis=-1, keepdims=True)
    h = (x * lax.rsqrt(ms + EPS) * n1_ref[...]).astype(_bf16)
    W = HG_WIDTH

    def col(j):
        return _dot(h, w_ref[:, j * W:(j + 1) * W])

    proj_ref[:, COL_Q * W:(COL_Q + 1) * W] = col(0).astype(_bf16)
    for d in range(2):
        z = col(1 + d)
        lb = lb_ref[d:d + 1, :]
        f = lb + (1.0 - lb) * (1.0 / (1.0 + jnp.exp(-z)))
        gate_ref[:, d * W:(d + 1) * W] = jnp.log(f)
    proj_ref[:, COL_V * W:(COL_V + 1) * W] = col(3).astype(_bf16)
    proj_ref[:, COL_G * W:(COL_G + 1) * W] = col(4).astype(_bf16)
    for d, c in ((0, COL_AQ), (1, COL_AK)):
        a = col(5 + d)
        for p in range(NA_HEADS // 2):
            ap = a[:, p * LANE:(p + 1) * LANE]
            lanes = slice(c * W + p * LANE, c * W + (p + 1) * LANE)
            proj_ref[:, lanes] = (ap * _head_pair_rsqrt(ap)
                                  * qkw_ref[d:d + 1, p * LANE:(p + 1) * LANE]).astype(_bf16)
    proj_ref[:, COL_AV * W:(COL_AV + 1) * W] = col(7).astype(_bf16)


def _in_proj(x, n1w, w_in_bf, lb, qkw, tm):
    S, T, D = x.shape
    return pl.pallas_call(
        _in_proj_kernel,
        grid=(S, T // tm),
        in_specs=[
            pl.BlockSpec((None, tm, D), lambda s, i: (s, i, 0)),
            _const_spec((1, D)),
            _const_spec(w_in_bf.shape),
            _const_spec(lb.shape),
            _const_spec(qkw.shape),
        ],
        out_specs=[
            pl.BlockSpec((None, tm, 6 * HG_WIDTH), lambda s, i: (s, i, 0)),
            pl.BlockSpec((None, tm, 2 * HG_WIDTH), lambda s, i: (s, i, 0)),
        ],
        out_shape=[
            jax.ShapeDtypeStruct((S, T, 6 * HG_WIDTH), _bf16),
            jax.ShapeDtypeStruct((S, T, 2 * HG_WIDTH), _f32),
        ],
        compiler_params=_params("parallel", "parallel"),
        name="in_proj",
    )(x, n1w, w_in_bf, lb, qkw)


def _hgrn_constants():
    C = HG_CHUNK
    idx = np.arange(C)
    dmats, masks = [], []
    for direction in (0, 1):
        if direction == 0:
            tri = (idx[None, :] <= idx[:, None]).astype(np.float64)
        else:
            tri = (idx[None, :] >= idx[:, None]).astype(np.float64)
        dm = [tri]
        mk = [np.eye(C)]
        for lvl in range(HG_LEVELS):
            h = 1 << lvl
            start = (idx // (2 * h)) * (2 * h)
            ref = start + h - 1 if direction == 0 else start + h
            if lvl < HG_MATMUL_LEVELS:
                gather = np.zeros((C, C))
                gather[idx, ref] = 1.0
                dm.append(tri - gather @ tri)
            upper = (idx & h) != 0
            later, earlier = (upper, ~upper) if direction == 0 else (~upper, upper)
            same = (idx[:, None] // (2 * h)) == (idx[None, :] // (2 * h))
            mk.append((same & later[:, None] & earlier[None, :]).astype(np.float64))
        dmats.append(np.concatenate(dm, axis=0))
        masks.append(np.stack(mk, axis=0))
    return np.stack(dmats, 0), np.stack(masks, 0)


def _hgrn_kernel(qf_ref, qb_ref, vf_ref, vb_ref, gf_ref, gb_ref, dmat_ref, mask_ref,
                 of_ref, ob_ref, state_ref, *, chunks):
    C = HG_CHUNK

    @pl.when(pl.program_id(1) == 0)
    def _():
        state_ref[...] = jnp.zeros_like(state_ref)

    def chunk_step(c, carry):
        deltas = []
        for direction, g_ref in enumerate((gf_ref, gb_ref)):
            cc = c if direction == 0 else chunks - 1 - c
            rows = pl.ds(pl.multiple_of(cc * C, C), C)
            g = g_ref[rows, :]
            g_hi, g_lo = _split2(g)
            dmat = dmat_ref[direction]
            deltas.append((rows, g, _dot(dmat, g_hi) + _dot(dmat, g_lo)))

        units = []
        for direction, (q_ref, v_ref, o_ref) in enumerate(((qf_ref, vf_ref, of_ref), (qb_ref, vb_ref, ob_ref))):
            rows, g, delta_all = deltas[direction]
            last = C - 1 if direction == 0 else 0
            for hd in range(HG_HEADS):
                cols = slice(hd * HG_D, (hd + 1) * HG_D)
                q = q_ref[rows, cols].astype(_f32)
                v = v_ref[rows, cols]
                k = 1.0 - jnp.exp(g[:, cols])
                b = delta_all[0:C, cols]
                scores = mask_ref[direction, 0] * _dot_nt(q.astype(_bf16), k.astype(_bf16))
                for lvl in range(HG_LEVELS):
                    if lvl < HG_MATMUL_LEVELS:
                        d = delta_all[(lvl + 1) * C:(lvl + 2) * C, cols]
                    else:
                        h = 1 << lvl
                        ref0 = h - 1 if direction == 0 else h
                        d = b - jnp.concatenate(
                            [jnp.broadcast_to(b[s + ref0:s + ref0 + 1, :], (2 * h, HG_D))
                             for s in range(0, C, 2 * h)], axis=0)
                    e = jnp.exp(-jnp.abs(d))
                    p = _dot_nt((q * e).astype(_bf16), (k * e).astype(_bf16))
                    scores = scores + mask_ref[direction, lvl + 1] * p
                st = state_ref[direction, hd]
                b_last = b[last:last + 1, :]
                inter = _dot_nt((q * jnp.exp(b)).astype(_bf16), st.astype(_bf16))
                k_dec = (k * jnp.exp(b_last - b)).astype(_bf16)
                state_ref[direction, hd] = jnp.exp(b_last) * st + _dot_tn(v, k_dec)
                units.append((o_ref, rows, cols, scores, v, inter))

        for o_ref, rows, cols, scores, v, inter in units:
            o_ref[rows, cols] = inter + _dot(scores.astype(_bf16), v)
        return carry

    lax.fori_loop(0, chunks, chunk_step, 0, unroll=2)


def _hgrn2(proj, gates, dmat, mask, tb):
    S, T, _ = proj.shape
    n = T // tb
    W = HG_WIDTH
    fwd = lambda col: (lambda s, i: (s, i, col))
    bwd = lambda col: (lambda s, i: (s, n - 1 - i, col))
    blk = lambda im: pl.BlockSpec((None, tb, W), im)
    return pl.pallas_call(
        functools.partial(_hgrn_kernel, chunks=tb // HG_CHUNK),
        grid=(S, n),
        in_specs=[
            blk(fwd(COL_Q)), blk(bwd(COL_Q)), blk(fwd(COL_V)), blk(bwd(COL_V)),
            blk(fwd(0)), blk(bwd(1)),
            _const_spec(dmat.shape), _const_spec(mask.shape),
        ],
        out_specs=[blk(fwd(0)), blk(bwd(0))],
        out_shape=[jax.ShapeDtypeStruct((S, T, W), _f32)] * 2,
        scratch_shapes=[pltpu.VMEM((2, HG_HEADS, HG_D, HG_D), _f32)],
        compiler_params=_params("parallel", "arbitrary"),
        name="hgrn2",
    )(proj, proj, proj, proj, gates, gates, dmat, mask)


def _natten_bias(rpb):
    W = GRID_W
    case = np.arange(NA_KH)[:, None]
    a = np.arange(NA_KH)[None, :]
    dr = (a - case + (NA_KH - 1)).reshape(-1)
    sel_r = (dr[:, None] == np.arange(2 * NA_KH - 1)[None, :]).astype(np.float32)
    qc = np.arange(W)[:, None]
    kc = np.arange(W)[None, :]
    cs = np.clip(qc - NA_KW // 2, 0, W - NA_KW)
    valid = (kc >= cs) & (kc < cs + NA_KW)
    dc = np.clip(kc - qc + (NA_KW - 1), 0, 2 * NA_KW - 2).reshape(-1)
    sel_c = (dc[:, None] == np.arange(2 * NA_KW - 1)[None, :]).astype(np.float32)
    bias = jnp.einsum("hrc,sr,qc->hsq", rpb.astype(_f32), sel_r, sel_c, precision=lax.Precision.HIGHEST)
    bias = bias.reshape(NA_HEADS, NA_KH, NA_KH, W, W)
    bias = jnp.where(valid[None, None, None], bias, NEG_BIG)
    bias = bias.transpose(1, 0, 3, 2, 4)
    return bias.reshape(NA_KH, NA_HEADS // 2, 2 * W, NA_KH * W)


def _natten_kernel(q_ref, kp_ref, kc_ref, kn_ref, vp_ref, vc_ref, vn_ref, bias_ref, ow_ref,
                   o_ref, kbuf, vbuf, *, grid_rows):
    R = NA_ROWS_PER_STEP
    W = GRID_W
    blk = R * W
    for j, (k_r, v_r) in enumerate(((kp_ref, vp_ref), (kc_ref, vc_ref), (kn_ref, vn_ref))):
        kbuf[j * blk:(j + 1) * blk, :] = k_r[...]
        vbuf[j * blk:(j + 1) * blk, :] = v_r[...]
    r0 = pl.program_id(1) * R
    lane = lax.broadcasted_iota(jnp.int32, (W, LANE), 1)
    first = lane < NA_DH

    def row_group(ig, carry):
        units = []
        for k in range(NA_ROWS_PER_ITER):
            i = ig * NA_ROWS_PER_ITER + k
            r = r0 + i
            rs = jnp.clip(r - NA_KH // 2, 0, grid_rows - NA_KH)
            off = pl.multiple_of((rs - (r0 - R)) * W, W)
            for p in range(NA_HEADS // 2):
                units.append((pl.ds(pl.multiple_of(i * W, W), W), slice(p * LANE, (p + 1) * LANE),
                              pl.ds(off, NA_KH * W), r - rs, p))
        scores = []
        for qrows, cols, win, case, p in units:
            qp = q_ref[qrows, cols]
            zero = jnp.zeros_like(qp)
            lhs = jnp.concatenate([jnp.where(first, qp, zero), jnp.where(first, zero, qp)], axis=0)
            scores.append(_dot_nt(lhs, kbuf[win, cols]))
        probs = []
        for (qrows, cols, win, case, p), s in zip(units, scores):
            s = s + bias_ref[case, p]
            e = jnp.exp(s - jnp.max(s, axis=-1, keepdims=True))
            probs.append((e.astype(_bf16), jnp.sum(e, axis=-1, keepdims=True)))
        for (qrows, cols, win, case, p), (e, l) in zip(units, probs):
            pv = _dot(e, vbuf[win, cols]) / l
            out = jnp.where(first, pv[0:W], pv[W:2 * W])
            o_ref[qrows, cols] = (out * _head_pair_rsqrt(out) * ow_ref[:, cols]).astype(_bf16)
        return carry

    lax.fori_loop(0, R // NA_ROWS_PER_ITER, row_group, 0)


def _natten(proj, bias, out_w):
    S, T, _ = proj.shape
    rows = T // GRID_W
    assert rows >= NA_KH and rows % NA_ROWS_PER_STEP == 0
    nb = rows // NA_ROWS_PER_STEP
    blk_t = NA_ROWS_PER_STEP * GRID_W
    Wd = NA_WIDTH
    cur = lambda col: (lambda s, i: (s, i, col))
    prv = lambda col: (lambda s, i: (s, jnp.maximum(i - 1, 0), col))
    nxt = lambda col: (lambda s, i: (s, jnp.minimum(i + 1, nb - 1), col))
    blk = lambda im: pl.BlockSpec((None, blk_t, Wd), im)
    return pl.pallas_call(
        functools.partial(_natten_kernel, grid_rows=rows),
        grid=(S, nb),
        in_specs=[
            blk(cur(COL_AQ)),
            blk(prv(COL_AK)), blk(cur(COL_AK)), blk(nxt(COL_AK)),
            blk(prv(COL_AV)), blk(cur(COL_AV)), blk(nxt(COL_AV)),
            _const_spec(bias.shape), _const_spec(out_w.shape),
        ],
        out_specs=blk(cur(0)),
        out_shape=jax.ShapeDtypeStruct((S, T, Wd), _bf16),
        scratch_shapes=[pltpu.VMEM((3 * blk_t, Wd), _bf16), pltpu.VMEM((3 * blk_t, Wd), _bf16)],
        compiler_params=_params("parallel", "parallel"),
        name="natten",
    )(proj, proj, proj, proj, proj, proj, proj, bias, out_w)


def _out_proj_kernel(x_ref, of_ref, ob_ref, hg_ref, na_ref, w_ref, hw_ref, n2_ref, rw_ref, rb_ref, tril_ref,
                     x1_ref, h2_ref, route_ref, w1_ref, w2_ref, cnt_ref, carry_ref, *, steps_per_tile):
    @pl.when(pl.program_id(1) % steps_per_tile == 0)
    def _():
        carry_ref[...] = jnp.zeros_like(carry_ref)

    o = of_ref[...] + ob_ref[...]
    parts = []
    for hd in range(HG_HEADS):
        oh = o[:, hd * HG_D:(hd + 1) * HG_D]
        ms = jnp.mean(oh * oh, axis=-1, keepdims=True)
        parts.append(oh * lax.rsqrt(ms + EPS))
    on = jnp.concatenate(parts, axis=-1) * hw_ref[...]
    sig = 1.0 / (1.0 + jnp.exp(-hg_ref[...].astype(_f32)))
    mix_a = (on * sig).astype(_bf16)
    acc = _dot(mix_a, w_ref[0:HG_WIDTH, :]) + _dot(na_ref[...], w_ref[HG_WIDTH:, :])
    x1 = x_ref[...] + acc
    x1_ref[...] = x1
    ms2 = jnp.mean(x1 * x1, axis=-1, keepdims=True)
    h2 = x1 * lax.rsqrt(ms2 + EPS) * n2_ref[...]
    for c in range(TOKEN_SUBLANES):
        h2_ref[pl.ds(c, h2.shape[0], stride=TOKEN_SUBLANES), :] = h2[:, c * LANE:(c + 1) * LANE]

    h_hi, h_lo = _split2(h2)
    w_hi = rw_ref[0]
    w_lo = rw_ref[1]
    lg = _dot(h_hi, w_hi) + _dot(h_lo, w_hi) + _dot(h_hi, w_lo) + rb_ref[...]
    lane_i = lax.broadcasted_iota(jnp.int32, lg.shape, 1)
    lane = lane_i.astype(_f32)
    lane_grp = ((lane_i - EXPERT_LANE0) >> 3).astype(_f32)

    def first_argmax(vals, valid):
        masked = jnp.where(valid, vals, NEG_BIG)
        top = jnp.max(masked, axis=-1, keepdims=True)
        idx = jnp.min(jnp.where(valid & (masked == top), lane, float(ROUTER_LANES)), axis=-1, keepdims=True)
        return top, idx

    is_grp = lane_i < N_GROUPS
    g_top, g_idx = first_argmax(lg, is_grp)
    p_top = 1.0 / jnp.sum(jnp.where(is_grp, jnp.exp(lg - g_top), 0.0), axis=-1, keepdims=True)
    in_grp = (lane_i >= EXPERT_LANE0) & (lane_i < EXPERT_LANE0 + N_EXPERTS) & (lane_grp == g_idx)
    t1, i1 = first_argmax(lg, in_grp)
    t2, i2 = first_argmax(lg, in_grp & (lane != i1))
    e2 = jnp.exp(t2 - t1)
    w1 = p_top / (1.0 + e2)
    w2 = p_top * e2 / (1.0 + e2)
    w1_ref[...] = jnp.broadcast_to(w1, w1_ref.shape)
    w2_ref[...] = jnp.broadcast_to(w2, w2_ref.shape)

    sel1 = lane == i1
    sel2 = lane == i2
    onehot = jnp.where(sel1 | sel2, 1.0, 0.0)
    before = _dot(tril_ref[...], onehot.astype(_bf16)) + carry_ref[...]
    rank1 = jnp.sum(jnp.where(sel1, before, 0.0), axis=-1, keepdims=True)
    rank2 = jnp.sum(jnp.where(sel2, before, 0.0), axis=-1, keepdims=True)
    carry = carry_ref[...] + jnp.sum(onehot, axis=0, keepdims=True)
    carry_ref[...] = carry
    cnt_ref[...] = jnp.broadcast_to(carry, cnt_ref.shape)
    route_ref[...] = jnp.where(lane_i == 0, i1 - EXPERT_LANE0,
                     jnp.where(lane_i == 1, i2 - EXPERT_LANE0,
                     jnp.where(lane_i == 2, rank1, jnp.where(lane_i == 3, rank2, 0.0))))


def _out_proj(x, o_f, o_b, proj, na, w_out_bf, hw, n2w, rw, rb, tril, tm, moe_tm):
    S, T, D = x.shape
    W = HG_WIDTH
    n = T // tm
    row = lambda width, col=0: pl.BlockSpec((None, tm, width), lambda s, i: (s, i, col))
    lanes = jax.ShapeDtypeStruct((S, T, ROUTER_LANES), _f32)
    return pl.pallas_call(
        functools.partial(_out_proj_kernel, steps_per_tile=moe_tm // tm),
        grid=(S, n),
        in_specs=[
            row(D), row(W), row(W), row(W, COL_G), row(W),
            _const_spec(w_out_bf.shape), _const_spec(hw.shape), _const_spec(n2w.shape),
            _const_spec(rw.shape), _const_spec(rb.shape), _const_spec(tril.shape),
        ],
        out_specs=[row(D), pl.BlockSpec((None, tm * TOKEN_SUBLANES, LANE), lambda s, i: (s, i, 0)),
                   row(ROUTER_LANES), row(ROUTER_LANES), row(ROUTER_LANES),
                   pl.BlockSpec((None, None, 8, ROUTER_LANES), lambda s, i: (s, i, 0, 0))],
        out_shape=[
            jax.ShapeDtypeStruct((S, T, D), _f32),
            jax.ShapeDtypeStruct((S, T * TOKEN_SUBLANES, LANE), _f32),
            lanes, lanes, lanes,
            jax.ShapeDtypeStruct((S, n, 8, ROUTER_LANES), _f32),
        ],
        scratch_shapes=[pltpu.VMEM((1, ROUTER_LANES), _f32)],
        compiler_params=_params("parallel", "arbitrary"),
        name="out_proj",
    )(x, o_f, o_b, proj, na, w_out_bf, hw, n2w, rw, rb, tril)


MOE_WINDOWS = (128, 256)
MOE_UNROLL = 8
MOE_STREAM_BLOCK = 256
MOE_SLOTS = 2


def _moe_rows(tm):
    return 2 * tm + MOE_WINDOWS[-1]


def _moe_kernel(pos_ref, seg_ref, x1_hbm, h2_hbm, w1_ref, w2_ref, wg_ref, wu_ref, wd_ref, y_hbm,
                buf_ref, mix_ref, h2_stage, x1_stage, y_stage, h2_sem, x1_sem, y_sem, *, tm, sb):
    i = pl.program_id(0)
    e = pl.program_id(1)
    TS = TOKEN_SUBLANES
    nsb = tm // sb

    def positions(t):
        p = pos_ref[i * tm + t]
        return p & 0xFFFF, p >> 16

    def token_tile(row):
        return pl.ds(pl.multiple_of(row * TS, TS), TS)

    def token_rows(blk):
        return pl.ds(pl.multiple_of(i * tm + blk * sb, sb), sb)

    def h2_copy(blk, slot):
        src = h2_hbm.at[pl.ds(pl.multiple_of((i * tm + blk * sb) * TS, sb * TS), sb * TS), :]
        return pltpu.make_async_copy(src, h2_stage.at[slot], h2_sem.at[slot])

    def x1_copy(blk, slot):
        return pltpu.make_async_copy(x1_hbm.at[token_rows(blk), :], x1_stage.at[slot], x1_sem.at[slot])

    def y_copy(blk, slot):
        return pltpu.make_async_copy(y_stage.at[slot], y_hbm.at[token_rows(blk), :], y_sem.at[slot])

    @pl.when(e == 0)
    def _():
        buf_ref[...] = jnp.zeros_like(buf_ref)
        h2_copy(0, 0).start()

        def dispatch(blk, carry):
            slot = blk % MOE_SLOTS
            h2_copy(blk, slot).wait()

            @pl.when(blk + 1 < nsb)
            def _():
                h2_copy(blk + 1, 1 - slot).start()

            def tokens(grp, c2):
                for k in range(MOE_UNROLL):
                    tl = grp * MOE_UNROLL + k
                    p1, p2 = positions(blk * sb + tl)
                    tok = h2_stage[slot, token_tile(tl), :]
                    buf_ref[token_tile(p1), :] = tok
                    buf_ref[token_tile(p2), :] = tok
                return c2

            lax.fori_loop(0, sb // MOE_UNROLL, tokens, 0)
            return carry

        lax.fori_loop(0, nsb, dispatch, 0)

    start = seg_ref[(i * N_EXPERTS + e) * 2]
    count = seg_ref[(i * N_EXPERTS + e) * 2 + 1]

    def window(rows_w, j):
        r0 = start + j * rows_w
        chunk = lambda c: pl.ds(r0 * TS + c, rows_w, stride=TS)
        xs = [buf_ref[chunk(c), :] for c in range(TS)]
        xb = jnp.concatenate(xs, axis=-1).astype(_bf16)
        a = _dot(xb, wg_ref[...])
        u = _dot(xb, wu_ref[...])
        hmid = a * (1.0 / (1.0 + jnp.exp(-a))) * u
        out = _dot(hmid.astype(_bf16), wd_ref[...])
        valid = lax.broadcasted_iota(jnp.int32, (rows_w, LANE), 0) + j * rows_w < count
        for c in range(TS):
            buf_ref[chunk(c), :] = jnp.where(valid, out[:, c * LANE:(c + 1) * LANE], xs[c])

    small, large = MOE_WINDOWS

    @pl.when((count > 0) & (count <= small))
    def _():
        window(small, 0)

    @pl.when(count > small)
    def _():
        def body(j, carry):
            window(large, j)
            return carry

        lax.fori_loop(0, (count + large - 1) // large, body, 0)

    @pl.when(e == N_EXPERTS - 1)
    def _():
        x1_copy(0, 0).start()

        def combine(blk, carry):
            slot = blk % MOE_SLOTS

            def tokens(grp, c2):
                for k in range(MOE_UNROLL):
                    tl = grp * MOE_UNROLL + k
                    t = blk * sb + tl
                    p1, p2 = positions(t)
                    g1 = jnp.broadcast_to(w1_ref[pl.ds(t, 1), :], (TS, LANE))
                    g2 = jnp.broadcast_to(w2_ref[pl.ds(t, 1), :], (TS, LANE))
                    mix_ref[token_tile(tl), :] = (g1 * buf_ref[token_tile(p1), :]
                                                  + g2 * buf_ref[token_tile(p2), :])
                return c2

            lax.fori_loop(0, sb // MOE_UNROLL, tokens, 0)
            x1_copy(blk, slot).wait()

            @pl.when(blk + 1 < nsb)
            def _():
                x1_copy(blk + 1, 1 - slot).start()

            @pl.when(blk >= MOE_SLOTS)
            def _():
                y_copy(blk - MOE_SLOTS, slot).wait()

            for c in range(TS):
                cols = slice(c * LANE, (c + 1) * LANE)
                y_stage[slot, :, cols] = x1_stage[slot, :, cols] + mix_ref[pl.ds(c, sb, stride=TS), :]
            y_copy(blk, slot).start()
            return carry

        lax.fori_loop(0, nsb, combine, 0)
        for blk in range(nsb - MOE_SLOTS, nsb):
            y_copy(blk, blk % MOE_SLOTS).wait()


def _moe(pos, seg, x1, h2, w1, w2, wg, wu, wd, tm):
    N, D = x1.shape
    sb = min(MOE_STREAM_BLOCK, tm // MOE_SLOTS)
    assert tm % sb == 0 and tm // sb >= MOE_SLOTS
    tile = lambda width: pl.BlockSpec((tm, width), lambda i, e, *_: (i, 0), pipeline_mode=pl.Buffered(1))
    wspec = lambda a, b: pl.BlockSpec((None, a, b), lambda i, e, *_: (e, 0, 0))
    hbm = pl.BlockSpec(memory_space=pl.ANY)
    return pl.pallas_call(
        functools.partial(_moe_kernel, tm=tm, sb=sb),
        grid_spec=pltpu.PrefetchScalarGridSpec(
            num_scalar_prefetch=2,
            grid=(N // tm, N_EXPERTS),
            in_specs=[
                hbm, hbm, tile(ROUTER_LANES), tile(ROUTER_LANES),
                wspec(D, EXPERT_FF), wspec(D, EXPERT_FF), wspec(EXPERT_FF, D),
            ],
            out_specs=hbm,
            scratch_shapes=[
                pltpu.VMEM((_moe_rows(tm) * TOKEN_SUBLANES, LANE), _f32),
                pltpu.VMEM((sb * TOKEN_SUBLANES, LANE), _f32),
                pltpu.VMEM((MOE_SLOTS, sb * TOKEN_SUBLANES, LANE), _f32),
                pltpu.VMEM((MOE_SLOTS, sb, D), _f32),
                pltpu.VMEM((MOE_SLOTS, sb, D), _f32),
                pltpu.SemaphoreType.DMA((MOE_SLOTS,)),
                pltpu.SemaphoreType.DMA((MOE_SLOTS,)),
                pltpu.SemaphoreType.DMA((MOE_SLOTS,)),
            ],
        ),
        out_shape=jax.ShapeDtypeStruct((N, D), _f32),
        compiler_params=_params("parallel", "arbitrary"),
        name="moe",
    )(pos, seg, x1, h2, w1, w2, wg, wu, wd)


def _moe_plan(route, cnt, tm, steps_per_tile):
    N = route.shape[0]
    n_tiles = N // tm
    counts = cnt.reshape(-1, 8, ROUTER_LANES)[steps_per_tile - 1::steps_per_tile, 0,
                                               EXPERT_LANE0:EXPERT_LANE0 + N_EXPERTS].astype(jnp.int32)
    seg_start = jnp.cumsum(counts, axis=1) - counts
    r = route[:, :4].astype(jnp.int32).reshape(n_tiles, tm, 4)
    experts = jnp.arange(N_EXPERTS, dtype=jnp.int32)

    def row_of(expert, rank):
        onehot = expert[..., None] == experts
        return jnp.sum(jnp.where(onehot, seg_start[:, None, :], 0), axis=-1) + rank

    pos = row_of(r[..., 0], r[..., 2]) | (row_of(r[..., 1], r[..., 3]) << 16)
    seg = jnp.stack([seg_start, counts], axis=-1)
    return pos.reshape(-1), seg.reshape(-1)


def _lower_bound(logits, l):
    return jnp.cumsum(jax.nn.softmax(logits.astype(_f32), axis=0), axis=0)[l]


ROW_TILE = 512
HGRN_BLOCK = 512
MOE_TILE = 4096


def _layer(x, l, c):
    S, T, D = x.shape
    tm = min(ROW_TILE, T)
    moe_tm = min(MOE_TILE, T)
    proj, gates = _in_proj(x, c["n1w"][l], c["w_in"][l], c["lb"][l], c["qkw"][l], tm)
    o_f, o_b = _hgrn2(proj, gates, c["dmat"], c["mask"], min(HGRN_BLOCK, T))
    na = _natten(proj, c["na_bias"][l], c["na_ow"][l])
    x1, h2, route, w1, w2, cnt = _out_proj(x, o_f, o_b, proj, na, c["w_out"][l], c["hw"][l], c["n2w"][l],
                                           c["rw"][l], c["rb"][l], c["tril"][:tm, :tm], tm, moe_tm)
    N = S * T
    flat = lambda a: a.reshape(N, a.shape[-1])
    pos, seg = _moe_plan(flat(route), cnt, moe_tm, moe_tm // tm)
    y = _moe(pos, seg, flat(x1), h2.reshape(N * TOKEN_SUBLANES, LANE), flat(w1), flat(w2),
             c["wg"][l], c["wu"][l], c["wd"][l], moe_tm)
    return y.reshape(S, T, D)


def _prepare(norm1_w, w_in, lb_fwd_logits, lb_bwd_logits, hgrn_norm_w, q_norm_w, k_norm_w, na_rpb,
             na_out_norm_w, w_out, norm2_w, w_router_group, b_router_group, w_router_expert,
             b_router_expert, w_expert_gate, w_expert_up, w_expert_down):
    depth = w_in.shape[0]
    dmat, mask = _hgrn_constants()
    pad = ROUTER_LANES - N_GROUPS - N_EXPERTS
    rw = jnp.pad(jnp.concatenate([w_router_group, w_router_expert], axis=-1).astype(_f32),
                 ((0, 0), (0, 0), (0, pad)))
    rw_hi = rw.astype(_bf16)
    rw_lo = (rw - rw_hi.astype(_f32)).astype(_bf16)
    rb = jnp.pad(jnp.concatenate([b_router_group, b_router_expert], axis=-1).astype(_f32),
                 ((0, 0), (0, pad)))
    return {
        "n1w": norm1_w.astype(_f32)[:, None, :],
        "w_in": w_in.astype(_bf16),
        "lb": jnp.stack([jnp.stack([_lower_bound(lb_fwd_logits, l), _lower_bound(lb_bwd_logits, l)])
                         for l in range(depth)]),
        "qkw": jnp.stack([jnp.tile(q_norm_w.astype(_f32), (1, NA_HEADS)) * (NA_DH ** -0.5),
                          jnp.tile(k_norm_w.astype(_f32), (1, NA_HEADS))], axis=1),
        "dmat": jnp.asarray(dmat, _bf16),
        "mask": jnp.asarray(mask, _f32),
        "na_bias": jnp.stack([_natten_bias(na_rpb[l]) for l in range(depth)]),
        "na_ow": na_out_norm_w.astype(_f32)[:, None, :],
        "w_out": w_out.astype(_bf16),
        "hw": hgrn_norm_w.astype(_f32)[:, None, :],
        "n2w": norm2_w.astype(_f32)[:, None, :],
        "rw": jnp.stack([rw_hi, rw_lo], axis=1),
        "rb": rb[:, None, :],
        "tril": jnp.asarray(np.tril(np.ones((ROW_TILE, ROW_TILE), np.float32), -1), _bf16),
        "wg": w_expert_gate.astype(_bf16),
        "wu": w_expert_up.astype(_bf16),
        "wd": w_expert_down.astype(_bf16),
    }


def kernel(x_prompt, x_sample, norm1_w, w_in, lb_fwd_logits, lb_bwd_logits, hgrn_norm_w, q_norm_w, k_norm_w, na_rpb, na_out_norm_w, w_out, norm2_w, w_router_group, b_router_group, w_router_expert, b_router_expert, w_expert_gate, w_expert_up, w_expert_down):
    c = _prepare(norm1_w, w_in, lb_fwd_logits, lb_bwd_logits, hgrn_norm_w, q_norm_w, k_norm_w, na_rpb,
                 na_out_norm_w, w_out, norm2_w, w_router_group, b_router_group, w_router_expert,
                 b_router_expert, w_expert_gate, w_expert_up, w_expert_down)
    outs = []
    for x in (x_prompt, x_sample):
        for l in range(w_in.shape[0]):
            x = _layer(x, l, c)
        outs.append(x)
    return tuple(outs)
```

```python
import functools

import numpy as np
import jax
import jax.numpy as jnp
from jax import lax
from jax.experimental import pallas as pl
from jax.experimental.pallas import tpu as pltpu

D_MODEL = 1024
EPS = 1e-6
GRID_W = 64
HG_HEADS = 4
HG_D = 128
HG_WIDTH = HG_HEADS * HG_D
HG_CHUNK = 64
HG_LEVELS = 6
HG_MATMUL_LEVELS = 3
NA_HEADS = 8
NA_DH = 64
NA_WIDTH = NA_HEADS * NA_DH
NA_KH = 8
NA_KW = 16
NA_ROWS_PER_STEP = 8
NA_ROWS_PER_ITER = 2
N_GROUPS = 4
EXPERTS_PER_GROUP = 8
N_EXPERTS = N_GROUPS * EXPERTS_PER_GROUP
EXPERT_FF = 512
ROUTER_LANES = 128
EXPERT_LANE0 = N_GROUPS
NEG_BIG = -1e30
LOG2_E = 1.4426950408889634
LANE = 128
TOKEN_SUBLANES = D_MODEL // LANE
VMEM_LIMIT = 56 * 1024 * 1024

COL_Q, COL_V, COL_G, COL_AQ, COL_AK, COL_AV = range(6)

_f32 = jnp.float32
_bf16 = jnp.bfloat16


def _dot(a, b):
    return jnp.dot(a, b, preferred_element_type=_f32)


def _dot_nt(a, b):
    return lax.dot_general(a, b, (((1,), (1,)), ((), ())), preferred_element_type=_f32)


def _dot_tn(a, b):
    return lax.dot_general(a, b, (((0,), (0,)), ((), ())), preferred_element_type=_f32)


def _split2(x):
    hi = x.astype(_bf16)
    lo = (x - hi.astype(_f32)).astype(_bf16)
    return hi, lo


def _head_pair_rsqrt(a):
    first = lax.broadcasted_iota(jnp.int32, a.shape, 1) < NA_DH
    sq = a * a
    s_first = jnp.sum(jnp.where(first, sq, 0.0), axis=-1, keepdims=True)
    s_second = jnp.sum(jnp.where(first, 0.0, sq), axis=-1, keepdims=True)
    return lax.rsqrt(jnp.where(first, s_first, s_second) * (1.0 / NA_DH) + EPS)


def _params(*sem):
    return pltpu.CompilerParams(dimension_semantics=sem, vmem_limit_bytes=VMEM_LIMIT)


def _const_spec(shape):
    nd = len(shape)
    return pl.BlockSpec(shape, lambda *_: (0,) * nd)


def _in_proj_kernel(x_ref, n1_ref, w_ref, lb_ref, qkw_ref, proj_ref, gate_ref):
    x = x_ref[...]
    ms = jnp.mean(x * x, axis=-1, keepdims=True)
    h = (x * lax.rsqrt(ms + EPS) * n1_ref[...]).astype(_bf16)
    W = HG_WIDTH

    def col(j):
        return _dot(h, w_ref[:, j * W:(j + 1) * W])

    proj_ref[:, COL_Q * W:(COL_Q + 1) * W] = col(0).astype(_bf16)
    for d in range(2):
        z = col(1 + d)
        lb = lb_ref[d:d + 1, :]
        f = lb + (1.0 - lb) * (1.0 / (1.0 + jnp.exp(-z)))
        gate_ref[:, d * W:(d + 1) * W] = jnp.log(f) * LOG2_E
    proj_ref[:, COL_V * W:(COL_V + 1) * W] = col(3).astype(_bf16)
    proj_ref[:, COL_G * W:(COL_G + 1) * W] = col(4).astype(_bf16)
    for d, c in ((0, COL_AQ), (1, COL_AK)):
        a = col(5 + d)
        for p in range(NA_HEADS // 2):
            ap = a[:, p * LANE:(p + 1) * LANE]
            lanes = slice(c * W + p * LANE, c * W + (p + 1) * LANE)
            proj_ref[:, lanes] = (ap * _head_pair_rsqrt(ap)
                                  * qkw_ref[d:d + 1, p * LANE:(p + 1) * LANE]).astype(_bf16)
    proj_ref[:, COL_AV * W:(COL_AV + 1) * W] = col(7).astype(_bf16)


def _in_proj(x, n1w, w_in_bf, lb, qkw, tm):
    S, T, D = x.shape
    return pl.pallas_call(
        _in_proj_kernel,
        grid=(S, T // tm),
        in_specs=[
            pl.BlockSpec((None, tm, D), lambda s, i: (s, i, 0)),
            _const_spec((1, D)),
            _const_spec(w_in_bf.shape),
            _const_spec(lb.shape),
            _const_spec(qkw.shape),
        ],
        out_specs=[
            pl.BlockSpec((None, tm, 6 * HG_WIDTH), lambda s, i: (s, i, 0)),
            pl.BlockSpec((None, tm, 2 * HG_WIDTH), lambda s, i: (s, i, 0)),
        ],
        out_shape=[
            jax.ShapeDtypeStruct((S, T, 6 * HG_WIDTH), _bf16),
            jax.ShapeDtypeStruct((S, T, 2 * HG_WIDTH), _f32),
        ],
        compiler_params=_params("parallel", "parallel"),
        name="in_proj",
    )(x, n1w, w_in_bf, lb, qkw)


def _hgrn_constants():
    C = HG_CHUNK
    idx = np.arange(C)
    dmats, masks = [], []
    for direction in (0, 1):
        if direction == 0:
            tri = (idx[None, :] <= idx[:, None]).astype(np.float64)
        else:
            tri = (idx[None, :] >= idx[:, None]).astype(np.float64)
        dm = [tri]
        mk = [np.eye(C)]
        for lvl in range(HG_LEVELS):
            h = 1 << lvl
            start = (idx // (2 * h)) * (2 * h)
            ref = start + h - 1 if direction == 0 else start + h
            upper = (idx & h) != 0
            later, earlier = (upper, ~upper) if direction == 0 else (~upper, upper)
            if lvl < HG_MATMUL_LEVELS:
                gather = np.zeros((C, C))
                gather[idx, ref] = 1.0
                dm.append(np.where(later[:, None], 1.0, -1.0) * (tri - gather @ tri))
            same = (idx[:, None] // (2 * h)) == (idx[None, :] // (2 * h))
            mk.append((same & later[:, None] & earlier[None, :]).astype(np.float64))
        dmats.append(np.concatenate(dm, axis=0))
        masks.append(np.stack(mk, axis=0))
    return np.stack(dmats, 0), np.stack(masks, 0)


def _hgrn_kernel(qf_ref, qb_ref, vf_ref, vb_ref, gf_ref, gb_ref, dmat_ref, mask_ref,
                 of_ref, ob_ref, state_ref, *, chunks):
    C = HG_CHUNK

    @pl.when(pl.program_id(1) == 0)
    def _():
        state_ref[...] = jnp.zeros_like(state_ref)

    def chunk_step(c, carry):
        deltas = []
        for direction, g_ref in enumerate((gf_ref, gb_ref)):
            cc = c if direction == 0 else chunks - 1 - c
            rows = pl.ds(pl.multiple_of(cc * C, C), C)
            g = g_ref[rows, :]
            g_hi, g_lo = _split2(g)
            dmat = dmat_ref[direction]
            deltas.append((rows, g, _dot(dmat, g_hi) + _dot(dmat, g_lo)))

        units = []
        for direction, (q_ref, v_ref, o_ref) in enumerate(((qf_ref, vf_ref, of_ref), (qb_ref, vb_ref, ob_ref))):
            rows, g, delta_all = deltas[direction]
            last = C - 1 if direction == 0 else 0
            for hd in range(HG_HEADS):
                cols = slice(hd * HG_D, (hd + 1) * HG_D)
                q = q_ref[rows, cols]
                v = v_ref[rows, cols]
                k = (1.0 - jnp.exp2(g[:, cols])).astype(_bf16)
                b = delta_all[0:C, cols]
                scores = mask_ref[direction, 0] * _dot_nt(q, k)
                for lvl in range(HG_LEVELS):
                    if lvl < HG_MATMUL_LEVELS:
                        d = delta_all[(lvl + 1) * C:(lvl + 2) * C, cols]
                    else:
                        h = 1 << lvl
                        pieces = []
                        for s in range(0, C, 2 * h):
                            lo, hi = b[s:s + h, :], b[s + h:s + 2 * h, :]
                            if direction == 0:
                                ref = jnp.broadcast_to(b[s + h - 1:s + h, :], (h, HG_D))
                                pieces += [ref - lo, hi - ref]
                            else:
                                ref = jnp.broadcast_to(b[s + h:s + h + 1, :], (h, HG_D))
                                pieces += [lo - ref, ref - hi]
                        d = jnp.concatenate(pieces, axis=0)
                    e = jnp.exp2(d).astype(_bf16)
                    scores = scores + mask_ref[direction, lvl + 1] * _dot_nt(q * e, k * e)
                st = state_ref[direction, hd]
                b_last = b[last:last + 1, :]
                inter = _dot_nt(q * jnp.exp2(b).astype(_bf16), st.astype(_bf16))
                k_dec = k * jnp.exp2(b_last - b).astype(_bf16)
                state_ref[direction, hd] = jnp.exp2(b_last) * st + _dot_tn(v, k_dec)
                units.append((o_ref, rows, cols, scores, v, inter))

        for o_ref, rows, cols, scores, v, inter in units:
            o_ref[rows, cols] = inter + _dot(scores.astype(_bf16), v)
        return carry

    lax.fori_loop(0, chunks, chunk_step, 0, unroll=2)


def _hgrn2(proj, gates, dmat, mask, tb):
    S, T, _ = proj.shape
    n = T // tb
    W = HG_WIDTH
    fwd = lambda col: (lambda s, i: (s, i, col))
    bwd = lambda col: (lambda s, i: (s, n - 1 - i, col))
    blk = lambda im: pl.BlockSpec((None, tb, W), im)
    return pl.pallas_call(
        functools.partial(_hgrn_kernel, chunks=tb // HG_CHUNK),
        grid=(S, n),
        in_specs=[
            blk(fwd(COL_Q)), blk(bwd(COL_Q)), blk(fwd(COL_V)), blk(bwd(COL_V)),
            blk(fwd(0)), blk(bwd(1)),
            _const_spec(dmat.shape), _const_spec(mask.shape),
        ],
        out_specs=[blk(fwd(0)), blk(bwd(0))],
        out_shape=[jax.ShapeDtypeStruct((S, T, W), _f32)] * 2,
        scratch_shapes=[pltpu.VMEM((2, HG_HEADS, HG_D, HG_D), _f32)],
        compiler_params=_params("parallel", "arbitrary"),
        name="hgrn2",
    )(proj, proj, proj, proj, gates, gates, dmat, mask)


def _natten_bias(rpb):
    W = GRID_W
    case = np.arange(NA_KH)[:, None]
    a = np.arange(NA_KH)[None, :]
    dr = (a - case + (NA_KH - 1)).reshape(-1)
    sel_r = (dr[:, None] == np.arange(2 * NA_KH - 1)[None, :]).astype(np.float32)
    qc = np.arange(W)[:, None]
    kc = np.arange(W)[None, :]
    cs = np.clip(qc - NA_KW // 2, 0, W - NA_KW)
    valid = (kc >= cs) & (kc < cs + NA_KW)
    dc = np.clip(kc - qc + (NA_KW - 1), 0, 2 * NA_KW - 2).reshape(-1)
    sel_c = (dc[:, None] == np.arange(2 * NA_KW - 1)[None, :]).astype(np.float32)
    bias = jnp.einsum("hrc,sr,qc->hsq", rpb.astype(_f32), sel_r, sel_c, precision=lax.Precision.HIGHEST)
    bias = bias.reshape(NA_HEADS, NA_KH, NA_KH, W, W)
    bias = jnp.where(valid[None, None, None], bias, NEG_BIG)
    bias = bias.transpose(1, 0, 3, 2, 4)
    return bias.reshape(NA_KH, NA_HEADS // 2, 2 * W, NA_KH * W)


def _natten_kernel(q_ref, kp_ref, kc_ref, kn_ref, vp_ref, vc_ref, vn_ref, bias_ref, ow_ref,
                   o_ref, kbuf, vbuf, *, grid_rows):
    R = NA_ROWS_PER_STEP
    W = GRID_W
    blk = R * W
    for j, (k_r, v_r) in enumerate(((kp_ref, vp_ref), (kc_ref, vc_ref), (kn_ref, vn_ref))):
        kbuf[j * blk:(j + 1) * blk, :] = k_r[...]
        vbuf[j * blk:(j + 1) * blk, :] = v_r[...]
    r0 = pl.program_id(1) * R
    lane = lax.broadcasted_iota(jnp.int32, (W, LANE), 1)
    first = lane < NA_DH

    def row_group(ig, carry):
        units = []
        for k in range(NA_ROWS_PER_ITER):
            i = ig * NA_ROWS_PER_ITER + k
            r = r0 + i
            rs = jnp.clip(r - NA_KH // 2, 0, grid_rows - NA_KH)
            off = pl.multiple_of((rs - (r0 - R)) * W, W)
            for p in range(NA_HEADS // 2):
                units.append((pl.ds(pl.multiple_of(i * W, W), W), slice(p * LANE, (p + 1) * LANE),
                              pl.ds(off, NA_KH * W), r - rs, p))
        scores = []
        for qrows, cols, win, case, p in units:
            qp = q_ref[qrows, cols]
            zero = jnp.zeros_like(qp)
            lhs = jnp.concatenate([jnp.where(first, qp, zero), jnp.where(first, zero, qp)], axis=0)
            scores.append(_dot_nt(lhs, kbuf[win, cols]))
        probs = []
        for (qrows, cols, win, case, p), s in zip(units, scores):
            s = s + bias_ref[case, p]
            e = jnp.exp(s - jnp.max(s, axis=-1, keepdims=True))
            probs.append((e.astype(_bf16), jnp.sum(e, axis=-1, keepdims=True)))
        for (qrows, cols, win, case, p), (e, l) in zip(units, probs):
            pv = _dot(e, vbuf[win, cols]) / l
            out = jnp.where(first, pv[0:W], pv[W:2 * W])
            o_ref[qrows, cols] = (out * _head_pair_rsqrt(out) * ow_ref[:, cols]).astype(_bf16)
        return carry

    lax.fori_loop(0, R // NA_ROWS_PER_ITER, row_group, 0)


def _natten(proj, bias, out_w):
    S, T, _ = proj.shape
    rows = T // GRID_W
    assert rows >= NA_KH and rows % NA_ROWS_PER_STEP == 0
    nb = rows // NA_ROWS_PER_STEP
    blk_t = NA_ROWS_PER_STEP * GRID_W
    Wd = NA_WIDTH
    cur = lambda col: (lambda s, i: (s, i, col))
    prv = lambda col: (lambda s, i: (s, jnp.maximum(i - 1, 0), col))
    nxt = lambda col: (lambda s, i: (s, jnp.minimum(i + 1, nb - 1), col))
    blk = lambda im: pl.BlockSpec((None, blk_t, Wd), im)
    return pl.pallas_call(
        functools.partial(_natten_kernel, grid_rows=rows),
        grid=(S, nb),
        in_specs=[
            blk(cur(COL_AQ)),
            blk(prv(COL_AK)), blk(cur(COL_AK)), blk(nxt(COL_AK)),
            blk(prv(COL_AV)), blk(cur(COL_AV)), blk(nxt(COL_AV)),
            _const_spec(bias.shape), _const_spec(out_w.shape),
        ],
        out_specs=blk(cur(0)),
        out_shape=jax.ShapeDtypeStruct((S, T, Wd), _bf16),
        scratch_shapes=[pltpu.VMEM((3 * blk_t, Wd), _bf16), pltpu.VMEM((3 * blk_t, Wd), _bf16)],
        compiler_params=_params("parallel", "parallel"),
        name="natten",
    )(proj, proj, proj, proj, proj, proj, proj, bias, out_w)


def _out_proj_kernel(x_ref, of_ref, ob_ref, hg_ref, na_ref, w_ref, hw_ref, n2_ref, rw_ref, rb_ref, tril_ref,
                     x1_ref, h2_ref, route_ref, w1_ref, w2_ref, cnt_ref, carry_ref, *, steps_per_tile):
    @pl.when(pl.program_id(1) % steps_per_tile == 0)
    def _():
        carry_ref[...] = jnp.zeros_like(carry_ref)

    o = of_ref[...] + ob_ref[...]
    parts = []
    for hd in range(HG_HEADS):
        oh = o[:, hd * HG_D:(hd + 1) * HG_D]
        ms = jnp.mean(oh * oh, axis=-1, keepdims=True)
        parts.append(oh * lax.rsqrt(ms + EPS))
    on = jnp.concatenate(parts, axis=-1) * hw_ref[...]
    sig = 1.0 / (1.0 + jnp.exp(-hg_ref[...].astype(_f32)))
    mix_a = (on * sig).astype(_bf16)
    acc = _dot(mix_a, w_ref[0:HG_WIDTH, :]) + _dot(na_ref[...], w_ref[HG_WIDTH:, :])
    x1 = x_ref[...] + acc
    x1_ref[...] = x1
    ms2 = jnp.mean(x1 * x1, axis=-1, keepdims=True)
    h2 = x1 * lax.rsqrt(ms2 + EPS) * n2_ref[...]
    for c in range(TOKEN_SUBLANES):
        h2_ref[pl.ds(c, h2.shape[0], stride=TOKEN_SUBLANES), :] = h2[:, c * LANE:(c + 1) * LANE]

    h_hi, h_lo = _split2(h2)
    w_hi = rw_ref[0]
    w_lo = rw_ref[1]
    lg = _dot(h_hi, w_hi) + _dot(h_lo, w_hi) + _dot(h_hi, w_lo) + rb_ref[...]
    lane_i = lax.broadcasted_iota(jnp.int32, lg.shape, 1)
    lane = lane_i.astype(_f32)
    lane_grp = ((lane_i - EXPERT_LANE0) >> 3).astype(_f32)

    def first_argmax(vals, valid):
        masked = jnp.where(valid, vals, NEG_BIG)
        top = jnp.max(masked, axis=-1, keepdims=True)
        idx = jnp.min(jnp.where(valid & (masked == top), lane, float(ROUTER_LANES)), axis=-1, keepdims=True)
        return top, idx

    is_grp = lane_i < N_GROUPS
    g_top, g_idx = first_argmax(lg, is_grp)
    p_top = 1.0 / jnp.sum(jnp.where(is_grp, jnp.exp(lg - g_top), 0.0), axis=-1, keepdims=True)
    in_grp = (lane_i >= EXPERT_LANE0) & (lane_i < EXPERT_LANE0 + N_EXPERTS) & (lane_grp == g_idx)
    t1, i1 = first_argmax(lg, in_grp)
    t2, i2 = first_argmax(lg, in_grp & (lane != i1))
    e2 = jnp.exp(t2 - t1)
    w1 = p_top / (1.0 + e2)
    w2 = p_top * e2 / (1.0 + e2)
    w1_ref[...] = jnp.broadcast_to(w1, w1_ref.shape)
    w2_ref[...] = jnp.broadcast_to(w2, w2_ref.shape)

    sel1 = lane == i1
    sel2 = lane == i2
    onehot = jnp.where(sel1 | sel2, 1.0, 0.0)
    before = _dot(tril_ref[...], onehot.astype(_bf16)) + carry_ref[...]
    rank1 = jnp.sum(jnp.where(sel1, before, 0.0), axis=-1, keepdims=True)
    rank2 = jnp.sum(jnp.where(sel2, before, 0.0), axis=-1, keepdims=True)
    carry = carry_ref[...] + jnp.sum(onehot, axis=0, keepdims=True)
    carry_ref[...] = carry
    cnt_ref[...] = jnp.broadcast_to(carry, cnt_ref.shape)
    route_ref[...] = jnp.where(lane_i == 0, i1 - EXPERT_LANE0,
                     jnp.where(lane_i == 1, i2 - EXPERT_LANE0,
                     jnp.where(lane_i == 2, rank1, jnp.where(lane_i == 3, rank2, 0.0))))


def _out_proj(x, o_f, o_b, proj, na, w_out_bf, hw, n2w, rw, rb, tril, tm, moe_tm):
    S, T, D = x.shape
    W = HG_WIDTH
    n = T // tm
    row = lambda width, col=0: pl.BlockSpec((None, tm, width), lambda s, i: (s, i, col))
    lanes = jax.ShapeDtypeStruct((S, T, ROUTER_LANES), _f32)
    return pl.pallas_call(
        functools.partial(_out_proj_kernel, steps_per_tile=moe_tm // tm),
        grid=(S, n),
        in_specs=[
            row(D), row(W), row(W), row(W, COL_G), row(W),
            _const_spec(w_out_bf.shape), _const_spec(hw.shape), _const_spec(n2w.shape),
            _const_spec(rw.shape), _const_spec(rb.shape), _const_spec(tril.shape),
        ],
        out_specs=[row(D), pl.BlockSpec((None, tm * TOKEN_SUBLANES, LANE), lambda s, i: (s, i, 0)),
                   row(ROUTER_LANES), row(ROUTER_LANES), row(ROUTER_LANES),
                   pl.BlockSpec((None, None, 8, ROUTER_LANES), lambda s, i: (s, i, 0, 0))],
        out_shape=[
            jax.ShapeDtypeStruct((S, T, D), _f32),
            jax.ShapeDtypeStruct((S, T * TOKEN_SUBLANES, LANE), _f32),
            lanes, lanes, lanes,
            jax.ShapeDtypeStruct((S, n, 8, ROUTER_LANES), _f32),
        ],
        scratch_shapes=[pltpu.VMEM((1, ROUTER_LANES), _f32)],
        compiler_params=_params("parallel", "arbitrary"),
        name="out_proj",
    )(x, o_f, o_b, proj, na, w_out_bf, hw, n2w, rw, rb, tril)


MOE_WINDOWS = (128, 288)
MOE_UNROLL = 8
MOE_STREAM_BLOCK = 256
MOE_SLOTS = 2


def _moe_rows(tm):
    return 2 * tm + MOE_WINDOWS[-1]


def _moe_kernel(pos1_ref, pos2_ref, seg_ref, x1_hbm, h2_hbm, w1_ref, w2_ref, wg_ref, wu_ref, wd_ref, y_hbm,
                buf_ref, mix_ref, h2_stage, x1_stage, y_stage, h2_sem, x1_sem, y_sem, *, tm, sb):
    i = pl.program_id(0)
    e = pl.program_id(1)
    TS = TOKEN_SUBLANES
    nsb = tm // sb

    def token_tile(row):
        return pl.ds(pl.multiple_of(row * TS, TS), TS)

    def buffer_tiles(t):
        return tuple(pl.ds(pl.multiple_of(ref[i * tm + t], TS), TS) for ref in (pos1_ref, pos2_ref))

    def token_rows(blk):
        return pl.ds(pl.multiple_of(i * tm + blk * sb, sb), sb)

    def h2_copy(blk, slot):
        src = h2_hbm.at[pl.ds(pl.multiple_of((i * tm + blk * sb) * TS, sb * TS), sb * TS), :]
        return pltpu.make_async_copy(src, h2_stage.at[slot], h2_sem.at[slot])

    def x1_copy(blk, slot):
        return pltpu.make_async_copy(x1_hbm.at[token_rows(blk), :], x1_stage.at[slot], x1_sem.at[slot])

    def y_copy(blk, slot):
        return pltpu.make_async_copy(y_stage.at[slot], y_hbm.at[token_rows(blk), :], y_sem.at[slot])

    @pl.when(e == 0)
    def _():
        slack = pl.ds(2 * tm * TS, MOE_WINDOWS[-1] * TS)
        buf_ref[slack, :] = jnp.zeros((MOE_WINDOWS[-1] * TS, LANE), _f32)
        h2_copy(0, 0).start()

        def dispatch(blk, carry):
            slot = blk % MOE_SLOTS
            h2_copy(blk, slot).wait()

            @pl.when(blk + 1 < nsb)
            def _():
                h2_copy(blk + 1, 1 - slot).start()

            def tokens(grp, c2):
                for k in range(MOE_UNROLL):
                    tl = grp * MOE_UNROLL + k
                    p1, p2 = buffer_tiles(blk * sb + tl)
                    tok = h2_stage[slot, token_tile(tl), :]
                    buf_ref[p1, :] = tok
                    buf_ref[p2, :] = tok
                return c2

            lax.fori_loop(0, sb // MOE_UNROLL, tokens, 0)
            return carry

        lax.fori_loop(0, nsb, dispatch, 0)

    start = seg_ref[(i * N_EXPERTS + e) * 2]
    count = seg_ref[(i * N_EXPERTS + e) * 2 + 1]

    def window(rows_w, done):
        r0 = start + done
        chunk = lambda c: pl.ds(r0 * TS + c, rows_w, stride=TS)
        xs = [buf_ref[chunk(c), :] for c in range(TS)]
        xb = jnp.concatenate(xs, axis=-1).astype(_bf16)
        a = _dot(xb, wg_ref[...])
        u = _dot(xb, wu_ref[...])
        hmid = a * (1.0 / (1.0 + jnp.exp(-a))) * u
        out = _dot(hmid.astype(_bf16), wd_ref[...])
        valid = lax.broadcasted_iota(jnp.int32, (rows_w, LANE), 0) + done < count
        for c in range(TS):
            buf_ref[chunk(c), :] = jnp.where(valid, out[:, c * LANE:(c + 1) * LANE], xs[c])

    small, large = MOE_WINDOWS

    n_large = count // large
    tail = count - n_large * large

    def body(j, carry):
        window(large, j * large)
        return carry

    lax.fori_loop(0, n_large, body, 0)

    @pl.when(tail > small)
    def _():
        window(large, n_large * large)

    @pl.when((tail > 0) & (tail <= small))
    def _():
        window(small, n_large * large)

    @pl.when(e == N_EXPERTS - 1)
    def _():
        x1_copy(0, 0).start()

        def combine(blk, carry):
            slot = blk % MOE_SLOTS

            def tokens(grp, c2):
                for k in range(MOE_UNROLL):
                    tl = grp * MOE_UNROLL + k
                    t = blk * sb + tl
                    p1, p2 = buffer_tiles(t)
                    g1 = jnp.broadcast_to(w1_ref[pl.ds(t, 1), :], (TS, LANE))
                    g2 = jnp.broadcast_to(w2_ref[pl.ds(t, 1), :], (TS, LANE))
                    mix_ref[token_tile(tl), :] = g1 * buf_ref[p1, :] + g2 * buf_ref[p2, :]
                return c2

            lax.fori_loop(0, sb // MOE_UNROLL, tokens, 0)
            x1_copy(blk, slot).wait()

            @pl.when(blk + 1 < nsb)
            def _():
                x1_copy(blk + 1, 1 - slot).start()

            @pl.when(blk >= MOE_SLOTS)
            def _():
                y_copy(blk - MOE_SLOTS, slot).wait()

            for c in range(TS):
                cols = slice(c * LANE, (c + 1) * LANE)
                y_stage[slot, :, cols] = x1_stage[slot, :, cols] + mix_ref[pl.ds(c, sb, stride=TS), :]
            y_copy(blk, slot).start()
            return carry

        lax.fori_loop(0, nsb, combine, 0)
        for blk in range(nsb - MOE_SLOTS, nsb):
            y_copy(blk, blk % MOE_SLOTS).wait()


def _moe(pos1, pos2, seg, x1, h2, w1, w2, wg, wu, wd, tm):
    N, D = x1.shape
    sb = min(MOE_STREAM_BLOCK, tm // MOE_SLOTS)
    assert tm % sb == 0 and tm // sb >= MOE_SLOTS
    tile = lambda width: pl.BlockSpec((tm, width), lambda i, e, *_: (i, 0), pipeline_mode=pl.Buffered(1))
    wspec = lambda a, b: pl.BlockSpec((None, a, b), lambda i, e, *_: (e, 0, 0))
    hbm = pl.BlockSpec(memory_space=pl.ANY)
    return pl.pallas_call(
        functools.partial(_moe_kernel, tm=tm, sb=sb),
        grid_spec=pltpu.PrefetchScalarGridSpec(
            num_scalar_prefetch=3,
            grid=(N // tm, N_EXPERTS),
            in_specs=[
                hbm, hbm, tile(ROUTER_LANES), tile(ROUTER_LANES),
                wspec(D, EXPERT_FF), wspec(D, EXPERT_FF), wspec(EXPERT_FF, D),
            ],
            out_specs=hbm,
            scratch_shapes=[
                pltpu.VMEM((_moe_rows(tm) * TOKEN_SUBLANES, LANE), _f32),
                pltpu.VMEM((sb * TOKEN_SUBLANES, LANE), _f32),
                pltpu.VMEM((MOE_SLOTS, sb * TOKEN_SUBLANES, LANE), _f32),
                pltpu.VMEM((MOE_SLOTS, sb, D), _f32),
                pltpu.VMEM((MOE_SLOTS, sb, D), _f32),
                pltpu.SemaphoreType.DMA((MOE_SLOTS,)),
                pltpu.SemaphoreType.DMA((MOE_SLOTS,)),
                pltpu.SemaphoreType.DMA((MOE_SLOTS,)),
            ],
        ),
        out_shape=jax.ShapeDtypeStruct((N, D), _f32),
        compiler_params=_params("parallel", "arbitrary"),
        name="moe",
    )(pos1, pos2, seg, x1, h2, w1, w2, wg, wu, wd)


def _moe_plan(route, cnt, tm, steps_per_tile):
    N = route.shape[0]
    n_tiles = N // tm
    counts = cnt.reshape(-1, 8, ROUTER_LANES)[steps_per_tile - 1::steps_per_tile, 0,
                                               EXPERT_LANE0:EXPERT_LANE0 + N_EXPERTS].astype(jnp.int32)
    seg_start = jnp.cumsum(counts, axis=1) - counts
    r = route[:, :4].astype(jnp.int32).reshape(n_tiles, tm, 4)
    experts = jnp.arange(N_EXPERTS, dtype=jnp.int32)

    def row_of(expert, rank):
        onehot = expert[..., None] == experts
        return jnp.sum(jnp.where(onehot, seg_start[:, None, :], 0), axis=-1) + rank

    pos1 = row_of(r[..., 0], r[..., 2]) * TOKEN_SUBLANES
    pos2 = row_of(r[..., 1], r[..., 3]) * TOKEN_SUBLANES
    seg = jnp.stack([seg_start, counts], axis=-1)
    return pos1.reshape(-1), pos2.reshape(-1), seg.reshape(-1)


def _lower_bound(logits, l):
    return jnp.cumsum(jax.nn.softmax(logits.astype(_f32), axis=0), axis=0)[l]


ROW_TILE = 512
HGRN_BLOCK = 512
MOE_TILE = 4096


def _layer(x, l, c):
    S, T, D = x.shape
    tm = min(ROW_TILE, T)
    moe_tm = min(MOE_TILE, T)
    proj, gates = _in_proj(x, c["n1w"][l], c["w_in"][l], c["lb"][l], c["qkw"][l], tm)
    o_f, o_b = _hgrn2(proj, gates, c["dmat"], c["mask"], min(HGRN_BLOCK, T))
    na = _natten(proj, c["na_bias"][l], c["na_ow"][l])
    x1, h2, route, w1, w2, cnt = _out_proj(x, o_f, o_b, proj, na, c["w_out"][l], c["hw"][l], c["n2w"][l],
                                           c["rw"][l], c["rb"][l], c["tril"][:tm, :tm], tm, moe_tm)
    N = S * T
    flat = lambda a: a.reshape(N, a.shape[-1])
    pos1, pos2, seg = _moe_plan(flat(route), cnt, moe_tm, moe_tm // tm)
    y = _moe(pos1, pos2, seg, flat(x1), h2.reshape(N * TOKEN_SUBLANES, LANE), flat(w1), flat(w2),
             c["wg"][l], c["wu"][l], c["wd"][l], moe_tm)
    return y.reshape(S, T, D)


def _prepare(norm1_w, w_in, lb_fwd_logits, lb_bwd_logits, hgrn_norm_w, q_norm_w, k_norm_w, na_rpb,
             na_out_norm_w, w_out, norm2_w, w_router_group, b_router_group, w_router_expert,
             b_router_expert, w_expert_gate, w_expert_up, w_expert_down):
    depth = w_in.shape[0]
    dmat, mask = _hgrn_constants()
    pad = ROUTER_LANES - N_GROUPS - N_EXPERTS
    rw = jnp.pad(jnp.concatenate([w_router_group, w_router_expert], axis=-1).astype(_f32),
                 ((0, 0), (0, 0), (0, pad)))
    rw_hi = rw.astype(_bf16)
    rw_lo = (rw - rw_hi.astype(_f32)).astype(_bf16)
    rb = jnp.pad(jnp.concatenate([b_router_group, b_router_expert], axis=-1).astype(_f32),
                 ((0, 0), (0, pad)))
    return {
        "n1w": norm1_w.astype(_f32)[:, None, :],
        "w_in": w_in.astype(_bf16),
        "lb": jnp.stack([jnp.stack([_lower_bound(lb_fwd_logits, l), _lower_bound(lb_bwd_logits, l)])
                         for l in range(depth)]),
        "qkw": jnp.stack([jnp.tile(q_norm_w.astype(_f32), (1, NA_HEADS)) * (NA_DH ** -0.5),
                          jnp.tile(k_norm_w.astype(_f32), (1, NA_HEADS))], axis=1),
        "dmat": jnp.asarray(dmat, _bf16),
        "mask": jnp.asarray(mask, _f32),
        "na_bias": jnp.stack([_natten_bias(na_rpb[l]) for l in range(depth)]),
        "na_ow": na_out_norm_w.astype(_f32)[:, None, :],
        "w_out": w_out.astype(_bf16),
        "hw": hgrn_norm_w.astype(_f32)[:, None, :],
        "n2w": norm2_w.astype(_f32)[:, None, :],
        "rw": jnp.stack([rw_hi, rw_lo], axis=1),
        "rb": rb[:, None, :],
        "tril": jnp.asarray(np.tril(np.ones((ROW_TILE, ROW_TILE), np.float32), -1), _bf16),
        "wg": w_expert_gate.astype(_bf16),
        "wu": w_expert_up.astype(_bf16),
        "wd": w_expert_down.astype(_bf16),
    }


def kernel(x_prompt, x_sample, norm1_w, w_in, lb_fwd_logits, lb_bwd_logits, hgrn_norm_w, q_norm_w, k_norm_w, na_rpb, na_out_norm_w, w_out, norm2_w, w_router_group, b_router_group, w_router_expert, b_router_expert, w_expert_gate, w_expert_up, w_expert_down):
    c = _prepare(norm1_w, w_in, lb_fwd_logits, lb_bwd_logits, hgrn_norm_w, q_norm_w, k_norm_w, na_rpb,
                 na_out_norm_w, w_out, norm2_w, w_router_group, b_router_group, w_router_expert,
                 b_router_expert, w_expert_gate, w_expert_up, w_expert_down)
    outs = []
    for x in (x_prompt, x_sample):
        for l in range(w_in.shape[0]):
            x = _layer(x, l, c)
        outs.append(x)
    return tuple(outs)
```

```python
import functools

import numpy as np
import jax
import jax.numpy as jnp
from jax import lax
from jax.experimental import pallas as pl
from jax.experimental.pallas import tpu as pltpu

D_MODEL = 1024
EPS = 1e-6
GRID_W = 64
HG_HEADS = 4
HG_D = 128
HG_WIDTH = HG_HEADS * HG_D
HG_CHUNK = 64
HG_LEVELS = 6
HG_MATMUL_LEVELS = 3
NA_HEADS = 8
NA_DH = 64
NA_WIDTH = NA_HEADS * NA_DH
NA_KH = 8
NA_KW = 16
NA_ROWS_PER_STEP = 8
NA_ROWS_PER_ITER = 2
N_GROUPS = 4
EXPERTS_PER_GROUP = 8
N_EXPERTS = N_GROUPS * EXPERTS_PER_GROUP
EXPERT_FF = 512
ROUTER_LANES = 128
EXPERT_LANE0 = N_GROUPS
NEG_BIG = -1e30
LOG2_E = 1.4426950408889634
LANE = 128
TOKEN_SUBLANES = D_MODEL // LANE
VMEM_LIMIT = 56 * 1024 * 1024

COL_Q, COL_V, COL_G, COL_AQ, COL_AK, COL_AV = range(6)

_f32 = jnp.float32
_bf16 = jnp.bfloat16


def _dot(a, b):
    return jnp.dot(a, b, preferred_element_type=_f32)


def _dot_nt(a, b):
    return lax.dot_general(a, b, (((1,), (1,)), ((), ())), preferred_element_type=_f32)


def _dot_tn(a, b):
    return lax.dot_general(a, b, (((0,), (0,)), ((), ())), preferred_element_type=_f32)


def _split2(x):
    hi = x.astype(_bf16)
    lo = (x - hi.astype(_f32)).astype(_bf16)
    return hi, lo


def _head_pair_rsqrt(a):
    first = lax.broadcasted_iota(jnp.int32, a.shape, 1) < NA_DH
    sq = a * a
    s_first = jnp.sum(jnp.where(first, sq, 0.0), axis=-1, keepdims=True)
    s_second = jnp.sum(jnp.where(first, 0.0, sq), axis=-1, keepdims=True)
    return lax.rsqrt(jnp.where(first, s_first, s_second) * (1.0 / NA_DH) + EPS)


def _params(*sem):
    return pltpu.CompilerParams(dimension_semantics=sem, vmem_limit_bytes=VMEM_LIMIT)


def _const_spec(shape):
    nd = len(shape)
    return pl.BlockSpec(shape, lambda *_: (0,) * nd)


def _in_proj_kernel(x_ref, n1_ref, w_ref, lb_ref, qkw_ref, proj_ref, gate_ref):
    x = x_ref[...]
    ms = jnp.mean(x * x, axis=-1, keepdims=True)
    h = (x * lax.rsqrt(ms + EPS) * n1_ref[...]).astype(_bf16)
    W = HG_WIDTH

    def col(j):
        return _dot(h, w_ref[:, j * W:(j + 1) * W])

    proj_ref[:, COL_Q * W:(COL_Q + 1) * W] = col(0).astype(_bf16)
    for d in range(2):
        z = col(1 + d)
        lb = lb_ref[d:d + 1, :]
        f = lb + (1.0 - lb) * (1.0 / (1.0 + jnp.exp(-z)))
        gate_ref[:, d * W:(d + 1) * W] = jnp.log(f) * LOG2_E
    proj_ref[:, COL_V * W:(COL_V + 1) * W] = col(3).astype(_bf16)
    proj_ref[:, COL_G * W:(COL_G + 1) * W] = col(4).astype(_bf16)
    for d, c in ((0, COL_AQ), (1, COL_AK)):
        a = col(5 + d)
        for p in range(NA_HEADS // 2):
            ap = a[:, p * LANE:(p + 1) * LANE]
            lanes = slice(c * W + p * LANE, c * W + (p + 1) * LANE)
            proj_ref[:, lanes] = (ap * _head_pair_rsqrt(ap)
                                  * qkw_ref[d:d + 1, p * LANE:(p + 1) * LANE]).astype(_bf16)
    proj_ref[:, COL_AV * W:(COL_AV + 1) * W] = col(7).astype(_bf16)


def _in_proj(x, n1w, w_in_bf, lb, qkw, tm):
    S, T, D = x.shape
    return pl.pallas_call(
        _in_proj_kernel,
        grid=(S, T // tm),
        in_specs=[
            pl.BlockSpec((None, tm, D), lambda s, i: (s, i, 0)),
            _const_spec((1, D)),
            _const_spec(w_in_bf.shape),
            _const_spec(lb.shape),
            _const_spec(qkw.shape),
        ],
        out_specs=[
            pl.BlockSpec((None, tm, 6 * HG_WIDTH), lambda s, i: (s, i, 0)),
            pl.BlockSpec((None, tm, 2 * HG_WIDTH), lambda s, i: (s, i, 0)),
        ],
        out_shape=[
            jax.ShapeDtypeStruct((S, T, 6 * HG_WIDTH), _bf16),
            jax.ShapeDtypeStruct((S, T, 2 * HG_WIDTH), _f32),
        ],
        compiler_params=_params("parallel", "parallel"),
        name="in_proj",
    )(x, n1w, w_in_bf, lb, qkw)


def _hgrn_constants():
    C = HG_CHUNK
    idx = np.arange(C)
    dmats, masks = [], []
    for direction in (0, 1):
        if direction == 0:
            tri = (idx[None, :] <= idx[:, None]).astype(np.float64)
        else:
            tri = (idx[None, :] >= idx[:, None]).astype(np.float64)
        dm = [tri]
        mk = [np.eye(C)]
        for lvl in range(HG_LEVELS):
            h = 1 << lvl
            start = (idx // (2 * h)) * (2 * h)
            ref = start + h - 1 if direction == 0 else start + h
            upper = (idx & h) != 0
            later, earlier = (upper, ~upper) if direction == 0 else (~upper, upper)
            if lvl < HG_MATMUL_LEVELS:
                gather = np.zeros((C, C))
                gather[idx, ref] = 1.0
                dm.append(np.where(later[:, None], 1.0, -1.0) * (tri - gather @ tri))
            same = (idx[:, None] // (2 * h)) == (idx[None, :] // (2 * h))
            mk.append((same & later[:, None] & earlier[None, :]).astype(np.float64))
        dmats.append(np.concatenate(dm, axis=0))
        masks.append(np.stack(mk, axis=0))
    return np.stack(dmats, 0), np.stack(masks, 0)


def _hgrn_kernel(qf_ref, qb_ref, vf_ref, vb_ref, gf_ref, gb_ref, dmat_ref, mask_ref,
                 of_ref, ob_ref, state_ref, *, chunks):
    C = HG_CHUNK

    @pl.when(pl.program_id(1) == 0)
    def _():
        state_ref[...] = jnp.zeros_like(state_ref)

    def chunk_step(c, carry):
        deltas = []
        for direction, g_ref in enumerate((gf_ref, gb_ref)):
            cc = c if direction == 0 else chunks - 1 - c
            rows = pl.ds(pl.multiple_of(cc * C, C), C)
            g = g_ref[rows, :]
            g_hi, g_lo = _split2(g)
            dmat = dmat_ref[direction]
            deltas.append((rows, g, _dot(dmat, g_hi) + _dot(dmat, g_lo)))

        units = []
        for direction, (q_ref, v_ref, o_ref) in enumerate(((qf_ref, vf_ref, of_ref), (qb_ref, vb_ref, ob_ref))):
            rows, g, delta_all = deltas[direction]
            last = C - 1 if direction == 0 else 0
            for hd in range(HG_HEADS):
                cols = slice(hd * HG_D, (hd + 1) * HG_D)
                q = q_ref[rows, cols]
                v = v_ref[rows, cols]
                k = (1.0 - jnp.exp2(g[:, cols])).astype(_bf16)
                b = delta_all[0:C, cols]
                scores = mask_ref[direction, 0] * _dot_nt(q, k)
                for lvl in range(HG_LEVELS):
                    if lvl < HG_MATMUL_LEVELS:
                        d = delta_all[(lvl + 1) * C:(lvl + 2) * C, cols]
                    else:
                        h = 1 << lvl
                        pieces = []
                        for s in range(0, C, 2 * h):
                            lo, hi = b[s:s + h, :], b[s + h:s + 2 * h, :]
                            if direction == 0:
                                ref = jnp.broadcast_to(b[s + h - 1:s + h, :], (h, HG_D))
                                pieces += [ref - lo, hi - ref]
                            else:
                                ref = jnp.broadcast_to(b[s + h:s + h + 1, :], (h, HG_D))
                                pieces += [lo - ref, ref - hi]
                        d = jnp.concatenate(pieces, axis=0)
                    e = jnp.exp2(d).astype(_bf16)
                    scores = scores + mask_ref[direction, lvl + 1] * _dot_nt(q * e, k * e)
                st = state_ref[direction, hd]
                b_last = b[last:last + 1, :]
                inter = _dot_nt(q * jnp.exp2(b).astype(_bf16), st.astype(_bf16))
                k_dec = k * jnp.exp2(b_last - b).astype(_bf16)
                state_ref[direction, hd] = jnp.exp2(b_last) * st + _dot_tn(v, k_dec)
                units.append((o_ref, rows, cols, scores, v, inter))

        for o_ref, rows, cols, scores, v, inter in units:
            o_ref[rows, cols] = inter + _dot(scores.astype(_bf16), v)
        return carry

    lax.fori_loop(0, chunks, chunk_step, 0, unroll=2)


def _hgrn2(proj, gates, dmat, mask, tb):
    S, T, _ = proj.shape
    n = T // tb
    W = HG_WIDTH
    fwd = lambda col: (lambda s, i: (s, i, col))
    bwd = lambda col: (lambda s, i: (s, n - 1 - i, col))
    blk = lambda im: pl.BlockSpec((None, tb, W), im)
    return pl.pallas_call(
        functools.partial(_hgrn_kernel, chunks=tb // HG_CHUNK),
        grid=(S, n),
        in_specs=[
            blk(fwd(COL_Q)), blk(bwd(COL_Q)), blk(fwd(COL_V)), blk(bwd(COL_V)),
            blk(fwd(0)), blk(bwd(1)),
            _const_spec(dmat.shape), _const_spec(mask.shape),
        ],
        out_specs=[blk(fwd(0)), blk(bwd(0))],
        out_shape=[jax.ShapeDtypeStruct((S, T, W), _f32)] * 2,
        scratch_shapes=[pltpu.VMEM((2, HG_HEADS, HG_D, HG_D), _f32)],
        compiler_params=_params("parallel", "arbitrary"),
        name="hgrn2",
    )(proj, proj, proj, proj, gates, gates, dmat, mask)


def _natten_bias(rpb):
    W = GRID_W
    case = np.arange(NA_KH)[:, None]
    a = np.arange(NA_KH)[None, :]
    dr = (a - case + (NA_KH - 1)).reshape(-1)
    sel_r = (dr[:, None] == np.arange(2 * NA_KH - 1)[None, :]).astype(np.float32)
    qc = np.arange(W)[:, None]
    kc = np.arange(W)[None, :]
    cs = np.clip(qc - NA_KW // 2, 0, W - NA_KW)
    valid = (kc >= cs) & (kc < cs + NA_KW)
    dc = np.clip(kc - qc + (NA_KW - 1), 0, 2 * NA_KW - 2).reshape(-1)
    sel_c = (dc[:, None] == np.arange(2 * NA_KW - 1)[None, :]).astype(np.float32)
    bias = jnp.einsum("hrc,sr,qc->hsq", rpb.astype(_f32), sel_r, sel_c, precision=lax.Precision.HIGHEST)
    bias = bias.reshape(NA_HEADS, NA_KH, NA_KH, W, W)
    bias = jnp.where(valid[None, None, None], bias, NEG_BIG)
    bias = bias.transpose(1, 0, 3, 2, 4)
    return bias.reshape(NA_KH, NA_HEADS // 2, 2 * W, NA_KH * W)


def _natten_kernel(q_ref, kp_ref, kc_ref, kn_ref, vp_ref, vc_ref, vn_ref, bias_ref, ow_ref,
                   o_ref, kbuf, vbuf, *, grid_rows):
    R = NA_ROWS_PER_STEP
    W = GRID_W
    blk = R * W
    for j, (k_r, v_r) in enumerate(((kp_ref, vp_ref), (kc_ref, vc_ref), (kn_ref, vn_ref))):
        kbuf[j * blk:(j + 1) * blk, :] = k_r[...]
        vbuf[j * blk:(j + 1) * blk, :] = v_r[...]
    r0 = pl.program_id(1) * R
    lane = lax.broadcasted_iota(jnp.int32, (W, LANE), 1)
    first = lane < NA_DH

    def row_group(ig, carry):
        units = []
        for k in range(NA_ROWS_PER_ITER):
            i = ig * NA_ROWS_PER_ITER + k
            r = r0 + i
            rs = jnp.clip(r - NA_KH // 2, 0, grid_rows - NA_KH)
            off = pl.multiple_of((rs - (r0 - R)) * W, W)
            for p in range(NA_HEADS // 2):
                units.append((pl.ds(pl.multiple_of(i * W, W), W), slice(p * LANE, (p + 1) * LANE),
                              pl.ds(off, NA_KH * W), r - rs, p))
        scores = []
        for qrows, cols, win, case, p in units:
            qp = q_ref[qrows, cols]
            zero = jnp.zeros_like(qp)
            lhs = jnp.concatenate([jnp.where(first, qp, zero), jnp.where(first, zero, qp)], axis=0)
            scores.append(_dot_nt(lhs, kbuf[win, cols]))
        probs = []
        for (qrows, cols, win, case, p), s in zip(units, scores):
            s = s + bias_ref[case, p]
            e = jnp.exp(s - jnp.max(s, axis=-1, keepdims=True))
            probs.append((e.astype(_bf16), jnp.sum(e, axis=-1, keepdims=True)))
        for (qrows, cols, win, case, p), (e, l) in zip(units, probs):
            pv = _dot(e, vbuf[win, cols]) / l
            out = jnp.where(first, pv[0:W], pv[W:2 * W])
            o_ref[qrows, cols] = (out * _head_pair_rsqrt(out) * ow_ref[:, cols]).astype(_bf16)
        return carry

    lax.fori_loop(0, R // NA_ROWS_PER_ITER, row_group, 0)


def _natten(proj, bias, out_w):
    S, T, _ = proj.shape
    rows = T // GRID_W
    assert rows >= NA_KH and rows % NA_ROWS_PER_STEP == 0
    nb = rows // NA_ROWS_PER_STEP
    blk_t = NA_ROWS_PER_STEP * GRID_W
    Wd = NA_WIDTH
    cur = lambda col: (lambda s, i: (s, i, col))
    prv = lambda col: (lambda s, i: (s, jnp.maximum(i - 1, 0), col))
    nxt = lambda col: (lambda s, i: (s, jnp.minimum(i + 1, nb - 1), col))
    blk = lambda im: pl.BlockSpec((None, blk_t, Wd), im)
    return pl.pallas_call(
        functools.partial(_natten_kernel, grid_rows=rows),
        grid=(S, nb),
        in_specs=[
            blk(cur(COL_AQ)),
            blk(prv(COL_AK)), blk(cur(COL_AK)), blk(nxt(COL_AK)),
            blk(prv(COL_AV)), blk(cur(COL_AV)), blk(nxt(COL_AV)),
            _const_spec(bias.shape), _const_spec(out_w.shape),
        ],
        out_specs=blk(cur(0)),
        out_shape=jax.ShapeDtypeStruct((S, T, Wd), _bf16),
        scratch_shapes=[pltpu.VMEM((3 * blk_t, Wd), _bf16), pltpu.VMEM((3 * blk_t, Wd), _bf16)],
        compiler_params=_params("parallel", "parallel"),
        name="natten",
    )(proj, proj, proj, proj, proj, proj, proj, bias, out_w)


def _out_proj_kernel(x_ref, of_ref, ob_ref, hg_ref, na_ref, w_ref, hw_ref, n2_ref, rw_ref, rb_ref, tril_ref,
                     x1_ref, h2_ref, route_ref, w1_ref, w2_ref, cnt_ref, carry_ref, *, steps_per_tile):
    @pl.when(pl.program_id(1) % steps_per_tile == 0)
    def _():
        carry_ref[...] = jnp.zeros_like(carry_ref)

    o = of_ref[...] + ob_ref[...]
    parts = []
    for hd in range(HG_HEADS):
        oh = o[:, hd * HG_D:(hd + 1) * HG_D]
        ms = jnp.mean(oh * oh, axis=-1, keepdims=True)
        parts.append(oh * lax.rsqrt(ms + EPS))
    on = jnp.concatenate(parts, axis=-1) * hw_ref[...]
    sig = 1.0 / (1.0 + jnp.exp(-hg_ref[...].astype(_f32)))
    mix_a = (on * sig).astype(_bf16)
    acc = _dot(mix_a, w_ref[0:HG_WIDTH, :]) + _dot(na_ref[...], w_ref[HG_WIDTH:, :])
    x1 = x_ref[...] + acc
    x1_ref[...] = x1
    ms2 = jnp.mean(x1 * x1, axis=-1, keepdims=True)
    h2 = x1 * lax.rsqrt(ms2 + EPS) * n2_ref[...]
    for c in range(TOKEN_SUBLANES):
        h2_ref[pl.ds(c, h2.shape[0], stride=TOKEN_SUBLANES), :] = h2[:, c * LANE:(c + 1) * LANE]

    h_hi, h_lo = _split2(h2)
    w_hi = rw_ref[0]
    w_lo = rw_ref[1]
    lg = _dot(h_hi, w_hi) + _dot(h_lo, w_hi) + _dot(h_hi, w_lo) + rb_ref[...]
    lane_i = lax.broadcasted_iota(jnp.int32, lg.shape, 1)
    lane = lane_i.astype(_f32)
    lane_grp = ((lane_i - EXPERT_LANE0) >> 3).astype(_f32)

    def first_argmax(vals, valid):
        masked = jnp.where(valid, vals, NEG_BIG)
        top = jnp.max(masked, axis=-1, keepdims=True)
        idx = jnp.min(jnp.where(valid & (masked == top), lane, float(ROUTER_LANES)), axis=-1, keepdims=True)
        return top, idx

    is_grp = lane_i < N_GROUPS
    g_top, g_idx = first_argmax(lg, is_grp)
    p_top = 1.0 / jnp.sum(jnp.where(is_grp, jnp.exp(lg - g_top), 0.0), axis=-1, keepdims=True)
    in_grp = (lane_i >= EXPERT_LANE0) & (lane_i < EXPERT_LANE0 + N_EXPERTS) & (lane_grp == g_idx)
    t1, i1 = first_argmax(lg, in_grp)
    t2, i2 = first_argmax(lg, in_grp & (lane != i1))
    e2 = jnp.exp(t2 - t1)
    w1 = p_top / (1.0 + e2)
    w2 = p_top * e2 / (1.0 + e2)
    w1_ref[...] = jnp.broadcast_to(w1, w1_ref.shape)
    w2_ref[...] = jnp.broadcast_to(w2, w2_ref.shape)

    sel1 = lane == i1
    sel2 = lane == i2
    onehot = jnp.where(sel1 | sel2, 1.0, 0.0)
    before = _dot(tril_ref[...], onehot.astype(_bf16)) + carry_ref[...]
    rank1 = jnp.sum(jnp.where(sel1, before, 0.0), axis=-1, keepdims=True)
    rank2 = jnp.sum(jnp.where(sel2, before, 0.0), axis=-1, keepdims=True)
    carry = carry_ref[...] + jnp.sum(onehot, axis=0, keepdims=True)
    carry_ref[...] = carry
    cnt_ref[...] = jnp.broadcast_to(carry, cnt_ref.shape)
    route_ref[...] = jnp.where(lane_i == 0, i1 - EXPERT_LANE0,
                     jnp.where(lane_i == 1, i2 - EXPERT_LANE0,
                     jnp.where(lane_i == 2, rank1, jnp.where(lane_i == 3, rank2, 0.0))))


def _out_proj(x, o_f, o_b, proj, na, w_out_bf, hw, n2w, rw, rb, tril, tm, moe_tm):
    S, T, D = x.shape
    W = HG_WIDTH
    n = T // tm
    row = lambda width, col=0: pl.BlockSpec((None, tm, width), lambda s, i: (s, i, col))
    lanes = jax.ShapeDtypeStruct((S, T, ROUTER_LANES), _f32)
    return pl.pallas_call(
        functools.partial(_out_proj_kernel, steps_per_tile=moe_tm // tm),
        grid=(S, n),
        in_specs=[
            row(D), row(W), row(W), row(W, COL_G), row(W),
            _const_spec(w_out_bf.shape), _const_spec(hw.shape), _const_spec(n2w.shape),
            _const_spec(rw.shape), _const_spec(rb.shape), _const_spec(tril.shape),
        ],
        out_specs=[row(D), pl.BlockSpec((None, tm * TOKEN_SUBLANES, LANE), lambda s, i: (s, i, 0)),
                   row(ROUTER_LANES), row(ROUTER_LANES), row(ROUTER_LANES),
                   pl.BlockSpec((None, None, 8, ROUTER_LANES), lambda s, i: (s, i, 0, 0))],
        out_shape=[
            jax.ShapeDtypeStruct((S, T, D), _f32),
            jax.ShapeDtypeStruct((S, T * TOKEN_SUBLANES, LANE), _f32),
            lanes, lanes, lanes,
            jax.ShapeDtypeStruct((S, n, 8, ROUTER_LANES), _f32),
        ],
        scratch_shapes=[pltpu.VMEM((1, ROUTER_LANES), _f32)],
        compiler_params=_params("parallel", "arbitrary"),
        name="out_proj",
    )(x, o_f, o_b, proj, na, w_out_bf, hw, n2w, rw, rb, tril)


MOE_WINDOWS = (128, 288)
MOE_UNROLL = 16
MOE_STREAM_BLOCK = 256
MOE_SLOTS = 2


def _moe_rows(tm):
    return 2 * tm + MOE_WINDOWS[-1]


def _moe_kernel(pos1_ref, pos2_ref, seg_ref, x1_hbm, h2_hbm, w1_ref, w2_ref, wg_hbm, wu_hbm, wd_hbm, y_hbm,
                buf_ref, mix_ref, h2_stage, x1_stage, y_stage, wg_buf, wu_buf, wd_buf,
                h2_sem, x1_sem, y_sem, w_sem, *, tm, sb):
    i = pl.program_id(0)
    TS = TOKEN_SUBLANES
    nsb = tm // sb

    def token_tile(row):
        return pl.ds(pl.multiple_of(row * TS, TS), TS)

    def buffer_tiles(t):
        return tuple(pl.ds(pl.multiple_of(ref[i * tm + t], TS), TS) for ref in (pos1_ref, pos2_ref))

    def token_rows(blk):
        return pl.ds(pl.multiple_of(i * tm + blk * sb, sb), sb)

    def h2_copy(blk, slot):
        src = h2_hbm.at[pl.ds(pl.multiple_of((i * tm + blk * sb) * TS, sb * TS), sb * TS), :]
        return pltpu.make_async_copy(src, h2_stage.at[slot], h2_sem.at[slot])

    def x1_copy(blk, slot):
        return pltpu.make_async_copy(x1_hbm.at[token_rows(blk), :], x1_stage.at[slot], x1_sem.at[slot])

    def y_copy(blk, slot):
        return pltpu.make_async_copy(y_stage.at[slot], y_hbm.at[token_rows(blk), :], y_sem.at[slot])

    def weight_copies(e, slot):
        return [pltpu.make_async_copy(hbm.at[e], vmem.at[slot], w_sem.at[slot, j])
                for j, (hbm, vmem) in enumerate(((wg_hbm, wg_buf), (wu_hbm, wu_buf), (wd_hbm, wd_buf)))]

    for cp in weight_copies(0, 0):
        cp.start()

    slack = pl.ds(2 * tm * TS, MOE_WINDOWS[-1] * TS)
    buf_ref[slack, :] = jnp.zeros((MOE_WINDOWS[-1] * TS, LANE), _f32)
    h2_copy(0, 0).start()

    def dispatch(blk, carry):
        slot = blk % MOE_SLOTS
        h2_copy(blk, slot).wait()

        @pl.when(blk + 1 < nsb)
        def _():
            h2_copy(blk + 1, 1 - slot).start()

        def tokens(grp, c2):
            for k in range(MOE_UNROLL):
                tl = grp * MOE_UNROLL + k
                p1, p2 = buffer_tiles(blk * sb + tl)
                tok = h2_stage[slot, token_tile(tl), :]
                buf_ref[p1, :] = tok
                buf_ref[p2, :] = tok
            return c2

        lax.fori_loop(0, sb // MOE_UNROLL, tokens, 0)
        return carry

    lax.fori_loop(0, nsb, dispatch, 0)

    small, large = MOE_WINDOWS

    def expert(e, carry):
        slot = e % MOE_SLOTS
        for cp in weight_copies(e, slot):
            cp.wait()

        @pl.when(e + 1 < N_EXPERTS)
        def _():
            for cp in weight_copies(e + 1, 1 - slot):
                cp.start()

        start = seg_ref[(i * N_EXPERTS + e) * 2]
        count = seg_ref[(i * N_EXPERTS + e) * 2 + 1]

        def window(rows_w, done):
            r0 = start + done
            chunk = lambda c: pl.ds(r0 * TS + c, rows_w, stride=TS)
            xs = [buf_ref[chunk(c), :] for c in range(TS)]
            xb = jnp.concatenate(xs, axis=-1).astype(_bf16)
            a = _dot(xb, wg_buf[slot])
            u = _dot(xb, wu_buf[slot])
            hmid = a * (1.0 / (1.0 + jnp.exp(-a))) * u
            out = _dot(hmid.astype(_bf16), wd_buf[slot])
            valid = lax.broadcasted_iota(jnp.int32, (rows_w, LANE), 0) + done < count
            for c in range(TS):
                buf_ref[chunk(c), :] = jnp.where(valid, out[:, c * LANE:(c + 1) * LANE], xs[c])

        n_large = count // large
        tail = count - n_large * large

        def body(j, c2):
            window(large, j * large)
            return c2

        lax.fori_loop(0, n_large, body, 0)

        @pl.when(tail > small)
        def _():
            window(large, n_large * large)

        @pl.when((tail > 0) & (tail <= small))
        def _():
            window(small, n_large * large)

        return carry

    lax.fori_loop(0, N_EXPERTS, expert, 0)

    x1_copy(0, 0).start()

    def combine(blk, carry):
        slot = blk % MOE_SLOTS

        def tokens(grp, c2):
            for k in range(MOE_UNROLL):
                tl = grp * MOE_UNROLL + k
                t = blk * sb + tl
                p1, p2 = buffer_tiles(t)
                g1 = jnp.broadcast_to(w1_ref[pl.ds(t, 1), :], (TS, LANE))
                g2 = jnp.broadcast_to(w2_ref[pl.ds(t, 1), :], (TS, LANE))
                mix_ref[token_tile(tl), :] = g1 * buf_ref[p1, :] + g2 * buf_ref[p2, :]
            return c2

        lax.fori_loop(0, sb // MOE_UNROLL, tokens, 0)
        x1_copy(blk, slot).wait()

        @pl.when(blk + 1 < nsb)
        def _():
            x1_copy(blk + 1, 1 - slot).start()

        @pl.when(blk >= MOE_SLOTS)
        def _():
            y_copy(blk - MOE_SLOTS, slot).wait()

        for c in range(TS):
            cols = slice(c * LANE, (c + 1) * LANE)
            y_stage[slot, :, cols] = x1_stage[slot, :, cols] + mix_ref[pl.ds(c, sb, stride=TS), :]
        y_copy(blk, slot).start()
        return carry

    lax.fori_loop(0, nsb, combine, 0)
    for blk in range(nsb - MOE_SLOTS, nsb):
        y_copy(blk, blk % MOE_SLOTS).wait()


def _moe(pos1, pos2, seg, x1, h2, w1, w2, wg, wu, wd, tm):
    N, D = x1.shape
    sb = min(MOE_STREAM_BLOCK, tm // MOE_SLOTS)
    assert tm % sb == 0 and tm // sb >= MOE_SLOTS
    tile = lambda width: pl.BlockSpec((tm, width), lambda i, *_: (i, 0), pipeline_mode=pl.Buffered(1))
    hbm = pl.BlockSpec(memory_space=pl.ANY)
    return pl.pallas_call(
        functools.partial(_moe_kernel, tm=tm, sb=sb),
        grid_spec=pltpu.PrefetchScalarGridSpec(
            num_scalar_prefetch=3,
            grid=(N // tm,),
            in_specs=[hbm, hbm, tile(ROUTER_LANES), tile(ROUTER_LANES), hbm, hbm, hbm],
            out_specs=hbm,
            scratch_shapes=[
                pltpu.VMEM((_moe_rows(tm) * TOKEN_SUBLANES, LANE), _f32),
                pltpu.VMEM((sb * TOKEN_SUBLANES, LANE), _f32),
                pltpu.VMEM((MOE_SLOTS, sb * TOKEN_SUBLANES, LANE), _f32),
                pltpu.VMEM((MOE_SLOTS, sb, D), _f32),
                pltpu.VMEM((MOE_SLOTS, sb, D), _f32),
                pltpu.VMEM((MOE_SLOTS, D, EXPERT_FF), _bf16),
                pltpu.VMEM((MOE_SLOTS, D, EXPERT_FF), _bf16),
                pltpu.VMEM((MOE_SLOTS, EXPERT_FF, D), _bf16),
                pltpu.SemaphoreType.DMA((MOE_SLOTS,)),
                pltpu.SemaphoreType.DMA((MOE_SLOTS,)),
                pltpu.SemaphoreType.DMA((MOE_SLOTS,)),
                pltpu.SemaphoreType.DMA((MOE_SLOTS, 3)),
            ],
        ),
        out_shape=jax.ShapeDtypeStruct((N, D), _f32),
        compiler_params=_params("parallel"),
        name="moe",
    )(pos1, pos2, seg, x1, h2, w1, w2, wg, wu, wd)


def _moe_plan(route, cnt, tm, steps_per_tile):
    N = route.shape[0]
    n_tiles = N // tm
    counts = cnt.reshape(-1, 8, ROUTER_LANES)[steps_per_tile - 1::steps_per_tile, 0,
                                               EXPERT_LANE0:EXPERT_LANE0 + N_EXPERTS].astype(jnp.int32)
    seg_start = jnp.cumsum(counts, axis=1) - counts
    r = route[:, :4].astype(jnp.int32).reshape(n_tiles, tm, 4)
    experts = jnp.arange(N_EXPERTS, dtype=jnp.int32)

    def row_of(expert, rank):
        onehot = expert[..., None] == experts
        return jnp.sum(jnp.where(onehot, seg_start[:, None, :], 0), axis=-1) + rank

    pos1 = row_of(r[..., 0], r[..., 2]) * TOKEN_SUBLANES
    pos2 = row_of(r[..., 1], r[..., 3]) * TOKEN_SUBLANES
    seg = jnp.stack([seg_start, counts], axis=-1)
    return pos1.reshape(-1), pos2.reshape(-1), seg.reshape(-1)


def _lower_bound(logits, l):
    return jnp.cumsum(jax.nn.softmax(logits.astype(_f32), axis=0), axis=0)[l]


ROW_TILE = 512
HGRN_BLOCK = 512
MOE_TILE = 4096


def _layer(x, l, c):
    S, T, D = x.shape
    tm = min(ROW_TILE, T)
    moe_tm = min(MOE_TILE, T)
    proj, gates = _in_proj(x, c["n1w"][l], c["w_in"][l], c["lb"][l], c["qkw"][l], tm)
    o_f, o_b = _hgrn2(proj, gates, c["dmat"], c["mask"], min(HGRN_BLOCK, T))
    na = _natten(proj, c["na_bias"][l], c["na_ow"][l])
    x1, h2, route, w1, w2, cnt = _out_proj(x, o_f, o_b, proj, na, c["w_out"][l], c["hw"][l], c["n2w"][l],
                                           c["rw"][l], c["rb"][l], c["tril"][:tm, :tm], tm, moe_tm)
    N = S * T
    flat = lambda a: a.reshape(N, a.shape[-1])
    pos1, pos2, seg = _moe_plan(flat(route), cnt, moe_tm, moe_tm // tm)
    y = _moe(pos1, pos2, seg, flat(x1), h2.reshape(N * TOKEN_SUBLANES, LANE), flat(w1), flat(w2),
             c["wg"][l], c["wu"][l], c["wd"][l], moe_tm)
    return y.reshape(S, T, D)


def _prepare(norm1_w, w_in, lb_fwd_logits, lb_bwd_logits, hgrn_norm_w, q_norm_w, k_norm_w, na_rpb,
             na_out_norm_w, w_out, norm2_w, w_router_group, b_router_group, w_router_expert,
             b_router_expert, w_expert_gate, w_expert_up, w_expert_down):
    depth = w_in.shape[0]
    dmat, mask = _hgrn_constants()
    pad = ROUTER_LANES - N_GROUPS - N_EXPERTS
    rw = jnp.pad(jnp.concatenate([w_router_group, w_router_expert], axis=-1).astype(_f32),
                 ((0, 0), (0, 0), (0, pad)))
    rw_hi = rw.astype(_bf16)
    rw_lo = (rw - rw_hi.astype(_f32)).astype(_bf16)
    rb = jnp.pad(jnp.concatenate([b_router_group, b_router_expert], axis=-1).astype(_f32),
                 ((0, 0), (0, pad)))
    return {
        "n1w": norm1_w.astype(_f32)[:, None, :],
        "w_in": w_in.astype(_bf16),
        "lb": jnp.stack([jnp.stack([_lower_bound(lb_fwd_logits, l), _lower_bound(lb_bwd_logits, l)])
                         for l in range(depth)]),
        "qkw": jnp.stack([jnp.tile(q_norm_w.astype(_f32), (1, NA_HEADS)) * (NA_DH ** -0.5),
                          jnp.tile(k_norm_w.astype(_f32), (1, NA_HEADS))], axis=1),
        "dmat": jnp.asarray(dmat, _bf16),
        "mask": jnp.asarray(mask, _f32),
        "na_bias": jnp.stack([_natten_bias(na_rpb[l]) for l in range(depth)]),
        "na_ow": na_out_norm_w.astype(_f32)[:, None, :],
        "w_out": w_out.astype(_bf16),
        "hw": hgrn_norm_w.astype(_f32)[:, None, :],
        "n2w": norm2_w.astype(_f32)[:, None, :],
        "rw": jnp.stack([rw_hi, rw_lo], axis=1),
        "rb": rb[:, None, :],
        "tril": jnp.asarray(np.tril(np.ones((ROW_TILE, ROW_TILE), np.float32), -1), _bf16),
        "wg": w_expert_gate.astype(_bf16),
        "wu": w_expert_up.astype(_bf16),
        "wd": w_expert_down.astype(_bf16),
    }


def kernel(x_prompt, x_sample, norm1_w, w_in, lb_fwd_logits, lb_bwd_logits, hgrn_norm_w, q_norm_w, k_norm_w, na_rpb, na_out_norm_w, w_out, norm2_w, w_router_group, b_router_group, w_router_expert, b_router_expert, w_expert_gate, w_expert_up, w_expert_down):
    c = _prepare(norm1_w, w_in, lb_fwd_logits, lb_bwd_logits, hgrn_norm_w, q_norm_w, k_norm_w, na_rpb,
                 na_out_norm_w, w_out, norm2_w, w_router_group, b_router_group, w_router_expert,
                 b_router_expert, w_expert_gate, w_expert_up, w_expert_down)
    outs = []
    for x in (x_prompt, x_sample):
        for l in range(w_in.shape[0]):
            x = _layer(x, l, c)
        outs.append(x)
    return tuple(outs)
```

```python
import functools

import numpy as np
import jax
import jax.numpy as jnp
from jax import lax
from jax.experimental import pallas as pl
from jax.experimental.pallas import tpu as pltpu

D_MODEL = 1024
EPS = 1e-6
GRID_W = 64
HG_HEADS = 4
HG_D = 128
HG_WIDTH = HG_HEADS * HG_D
HG_CHUNK = 64
HG_LEVELS = 6
HG_MATMUL_LEVELS = 3
HG_CHUNK_UNROLL = 4
NA_HEADS = 8
NA_DH = 64
NA_WIDTH = NA_HEADS * NA_DH
NA_KH = 8
NA_KW = 16
NA_ROWS_PER_STEP = 8
NA_ROWS_PER_ITER = 8
N_GROUPS = 4
EXPERTS_PER_GROUP = 8
N_EXPERTS = N_GROUPS * EXPERTS_PER_GROUP
EXPERT_FF = 512
ROUTER_LANES = 128
EXPERT_LANE0 = N_GROUPS
NEG_BIG = -1e30
LOG2_E = 1.4426950408889634
LANE = 128
TOKEN_SUBLANES = D_MODEL // LANE
VMEM_LIMIT = 56 * 1024 * 1024

COL_Q, COL_V, COL_G, COL_AQ, COL_AK, COL_AV = range(6)

_f32 = jnp.float32
_bf16 = jnp.bfloat16


def _dot(a, b):
    return jnp.dot(a, b, preferred_element_type=_f32)


def _dot_nt(a, b):
    return lax.dot_general(a, b, (((1,), (1,)), ((), ())), preferred_element_type=_f32)


def _dot_tn(a, b):
    return lax.dot_general(a, b, (((0,), (0,)), ((), ())), preferred_element_type=_f32)


def _split2(x):
    hi = x.astype(_bf16)
    lo = (x - hi.astype(_f32)).astype(_bf16)
    return hi, lo


def _head_pair_rsqrt(a):
    first = lax.broadcasted_iota(jnp.int32, a.shape, 1) < NA_DH
    sq = a * a
    s_first = jnp.sum(jnp.where(first, sq, 0.0), axis=-1, keepdims=True)
    s_second = jnp.sum(jnp.where(first, 0.0, sq), axis=-1, keepdims=True)
    return lax.rsqrt(jnp.where(first, s_first, s_second) * (1.0 / NA_DH) + EPS)


def _params(*sem):
    return pltpu.CompilerParams(dimension_semantics=sem, vmem_limit_bytes=VMEM_LIMIT)


def _const_spec(shape):
    nd = len(shape)
    return pl.BlockSpec(shape, lambda *_: (0,) * nd)


def _in_proj_kernel(x_ref, n1_ref, w_ref, lb_ref, qkw_ref, proj_ref, gate_ref):
    x = x_ref[...]
    ms = jnp.mean(x * x, axis=-1, keepdims=True)
    h = (x * lax.rsqrt(ms + EPS) * n1_ref[...]).astype(_bf16)
    W = HG_WIDTH

    def col(j):
        return _dot(h, w_ref[:, j * W:(j + 1) * W])

    proj_ref[:, COL_Q * W:(COL_Q + 1) * W] = col(0).astype(_bf16)
    for d in range(2):
        z = col(1 + d)
        lb = lb_ref[d:d + 1, :]
        f = lb + (1.0 - lb) * (1.0 / (1.0 + jnp.exp(-z)))
        gate_ref[:, d * W:(d + 1) * W] = jnp.log(f) * LOG2_E
    proj_ref[:, COL_V * W:(COL_V + 1) * W] = col(3).astype(_bf16)
    proj_ref[:, COL_G * W:(COL_G + 1) * W] = col(4).astype(_bf16)
    for d, c in ((0, COL_AQ), (1, COL_AK)):
        a = col(5 + d)
        for p in range(NA_HEADS // 2):
            ap = a[:, p * LANE:(p + 1) * LANE]
            lanes = slice(c * W + p * LANE, c * W + (p + 1) * LANE)
            proj_ref[:, lanes] = (ap * _head_pair_rsqrt(ap)
                                  * qkw_ref[d:d + 1, p * LANE:(p + 1) * LANE]).astype(_bf16)
    proj_ref[:, COL_AV * W:(COL_AV + 1) * W] = col(7).astype(_bf16)


def _in_proj(x, n1w, w_in_bf, lb, qkw, tm):
    S, T, D = x.shape
    return pl.pallas_call(
        _in_proj_kernel,
        grid=(S, T // tm),
        in_specs=[
            pl.BlockSpec((None, tm, D), lambda s, i: (s, i, 0)),
            _const_spec((1, D)),
            _const_spec(w_in_bf.shape),
            _const_spec(lb.shape),
            _const_spec(qkw.shape),
        ],
        out_specs=[
            pl.BlockSpec((None, tm, 6 * HG_WIDTH), lambda s, i: (s, i, 0)),
            pl.BlockSpec((None, tm, 2 * HG_WIDTH), lambda s, i: (s, i, 0)),
        ],
        out_shape=[
            jax.ShapeDtypeStruct((S, T, 6 * HG_WIDTH), _bf16),
            jax.ShapeDtypeStruct((S, T, 2 * HG_WIDTH), _f32),
        ],
        compiler_params=_params("parallel", "parallel"),
        name="in_proj",
    )(x, n1w, w_in_bf, lb, qkw)


def _hgrn_constants():
    C = HG_CHUNK
    idx = np.arange(C)
    dmats, masks = [], []
    for direction in (0, 1):
        if direction == 0:
            tri = (idx[None, :] <= idx[:, None]).astype(np.float64)
        else:
            tri = (idx[None, :] >= idx[:, None]).astype(np.float64)
        dm = [tri]
        mk = [np.eye(C)]
        for lvl in range(HG_LEVELS):
            h = 1 << lvl
            start = (idx // (2 * h)) * (2 * h)
            ref = start + h - 1 if direction == 0 else start + h
            upper = (idx & h) != 0
            later, earlier = (upper, ~upper) if direction == 0 else (~upper, upper)
            if lvl < HG_MATMUL_LEVELS:
                gather = np.zeros((C, C))
                gather[idx, ref] = 1.0
                dm.append(np.where(later[:, None], 1.0, -1.0) * (tri - gather @ tri))
            same = (idx[:, None] // (2 * h)) == (idx[None, :] // (2 * h))
            mk.append((same & later[:, None] & earlier[None, :]).astype(np.float64))
        dmats.append(np.concatenate(dm, axis=0))
        masks.append(np.stack(mk, axis=0))
    return np.stack(dmats, 0), np.stack(masks, 0)


def _hgrn_kernel(qf_ref, qb_ref, vf_ref, vb_ref, gf_ref, gb_ref, dmat_ref, mask_ref,
                 of_ref, ob_ref, state_ref, *, chunks):
    C = HG_CHUNK

    @pl.when(pl.program_id(1) == 0)
    def _():
        state_ref[...] = jnp.zeros_like(state_ref)

    def chunk_step(c, carry):
        deltas = []
        for direction, g_ref in enumerate((gf_ref, gb_ref)):
            cc = c if direction == 0 else chunks - 1 - c
            rows = pl.ds(pl.multiple_of(cc * C, C), C)
            g = g_ref[rows, :]
            g_hi, g_lo = _split2(g)
            dmat = dmat_ref[direction]
            deltas.append((rows, g, _dot(dmat, g_hi) + _dot(dmat, g_lo)))

        units = []
        for direction, (q_ref, v_ref, o_ref) in enumerate(((qf_ref, vf_ref, of_ref), (qb_ref, vb_ref, ob_ref))):
            rows, g, delta_all = deltas[direction]
            last = C - 1 if direction == 0 else 0
            for hd in range(HG_HEADS):
                cols = slice(hd * HG_D, (hd + 1) * HG_D)
                q = q_ref[rows, cols]
                v = v_ref[rows, cols]
                k = (1.0 - jnp.exp2(g[:, cols])).astype(_bf16)
                b = delta_all[0:C, cols]
                scores = mask_ref[direction, 0] * _dot_nt(q, k)
                for lvl in range(HG_LEVELS):
                    if lvl < HG_MATMUL_LEVELS:
                        d = delta_all[(lvl + 1) * C:(lvl + 2) * C, cols]
                    else:
                        h = 1 << lvl
                        pieces = []
                        for s in range(0, C, 2 * h):
                            lo, hi = b[s:s + h, :], b[s + h:s + 2 * h, :]
                            if direction == 0:
                                ref = jnp.broadcast_to(b[s + h - 1:s + h, :], (h, HG_D))
                                pieces += [ref - lo, hi - ref]
                            else:
                                ref = jnp.broadcast_to(b[s + h:s + h + 1, :], (h, HG_D))
                                pieces += [lo - ref, ref - hi]
                        d = jnp.concatenate(pieces, axis=0)
                    e = jnp.exp2(d).astype(_bf16)
                    scores = scores + mask_ref[direction, lvl + 1] * _dot_nt(q * e, k * e)
                st = state_ref[direction, hd]
                b_last = b[last:last + 1, :]
                inter = _dot_nt(q * jnp.exp2(b).astype(_bf16), st.astype(_bf16))
                k_dec = k * jnp.exp2(b_last - b).astype(_bf16)
                state_ref[direction, hd] = jnp.exp2(b_last) * st + _dot_tn(v, k_dec)
                units.append((o_ref, rows, cols, scores, v, inter))

        for o_ref, rows, cols, scores, v, inter in units:
            o_ref[rows, cols] = inter + _dot(scores.astype(_bf16), v)
        return carry

    lax.fori_loop(0, chunks, chunk_step, 0, unroll=HG_CHUNK_UNROLL)


def _hgrn2(proj, gates, dmat, mask, tb):
    S, T, _ = proj.shape
    n = T // tb
    W = HG_WIDTH
    fwd = lambda col: (lambda s, i: (s, i, col))
    bwd = lambda col: (lambda s, i: (s, n - 1 - i, col))
    blk = lambda im: pl.BlockSpec((None, tb, W), im)
    return pl.pallas_call(
        functools.partial(_hgrn_kernel, chunks=tb // HG_CHUNK),
        grid=(S, n),
        in_specs=[
            blk(fwd(COL_Q)), blk(bwd(COL_Q)), blk(fwd(COL_V)), blk(bwd(COL_V)),
            blk(fwd(0)), blk(bwd(1)),
            _const_spec(dmat.shape), _const_spec(mask.shape),
        ],
        out_specs=[blk(fwd(0)), blk(bwd(0))],
        out_shape=[jax.ShapeDtypeStruct((S, T, W), _f32)] * 2,
        scratch_shapes=[pltpu.VMEM((2, HG_HEADS, HG_D, HG_D), _f32)],
        compiler_params=_params("parallel", "arbitrary"),
        name="hgrn2",
    )(proj, proj, proj, proj, gates, gates, dmat, mask)


def _natten_bias(rpb):
    W = GRID_W
    case = np.arange(NA_KH)[:, None]
    a = np.arange(NA_KH)[None, :]
    dr = (a - case + (NA_KH - 1)).reshape(-1)
    sel_r = (dr[:, None] == np.arange(2 * NA_KH - 1)[None, :]).astype(np.float32)
    qc = np.arange(W)[:, None]
    kc = np.arange(W)[None, :]
    cs = np.clip(qc - NA_KW // 2, 0, W - NA_KW)
    valid = (kc >= cs) & (kc < cs + NA_KW)
    dc = np.clip(kc - qc + (NA_KW - 1), 0, 2 * NA_KW - 2).reshape(-1)
    sel_c = (dc[:, None] == np.arange(2 * NA_KW - 1)[None, :]).astype(np.float32)
    bias = jnp.einsum("hrc,sr,qc->hsq", rpb.astype(_f32), sel_r, sel_c, precision=lax.Precision.HIGHEST)
    bias = bias.reshape(NA_HEADS, NA_KH, NA_KH, W, W)
    bias = jnp.where(valid[None, None, None], bias, NEG_BIG)
    bias = bias.transpose(1, 0, 3, 2, 4)
    return bias.reshape(NA_KH, NA_HEADS // 2, 2 * W, NA_KH * W)


def _natten_kernel(q_ref, kp_ref, kc_ref, kn_ref, vp_ref, vc_ref, vn_ref, bias_ref, ow_ref,
                   o_ref, kbuf, vbuf, *, grid_rows):
    R = NA_ROWS_PER_STEP
    W = GRID_W
    blk = R * W
    for j, (k_r, v_r) in enumerate(((kp_ref, vp_ref), (kc_ref, vc_ref), (kn_ref, vn_ref))):
        kbuf[j * blk:(j + 1) * blk, :] = k_r[...]
        vbuf[j * blk:(j + 1) * blk, :] = v_r[...]
    r0 = pl.program_id(1) * R
    lane = lax.broadcasted_iota(jnp.int32, (W, LANE), 1)
    first = lane < NA_DH

    def row_group(ig, carry):
        units = []
        for k in range(NA_ROWS_PER_ITER):
            i = ig * NA_ROWS_PER_ITER + k
            r = r0 + i
            rs = jnp.clip(r - NA_KH // 2, 0, grid_rows - NA_KH)
            off = pl.multiple_of((rs - (r0 - R)) * W, W)
            for p in range(NA_HEADS // 2):
                units.append((pl.ds(pl.multiple_of(i * W, W), W), slice(p * LANE, (p + 1) * LANE),
                              pl.ds(off, NA_KH * W), r - rs, p))
        scores = []
        for qrows, cols, win, case, p in units:
            qp = q_ref[qrows, cols]
            zero = jnp.zeros_like(qp)
            lhs = jnp.concatenate([jnp.where(first, qp, zero), jnp.where(first, zero, qp)], axis=0)
            scores.append(_dot_nt(lhs, kbuf[win, cols]))
        probs = []
        for (qrows, cols, win, case, p), s in zip(units, scores):
            s = s + bias_ref[case, p]
            e = jnp.exp(s - jnp.max(s, axis=-1, keepdims=True))
            probs.append((e.astype(_bf16), jnp.sum(e, axis=-1, keepdims=True)))
        for (qrows, cols, win, case, p), (e, l) in zip(units, probs):
            pv = _dot(e, vbuf[win, cols]) / l
            out = jnp.where(first, pv[0:W], pv[W:2 * W])
            o_ref[qrows, cols] = (out * _head_pair_rsqrt(out) * ow_ref[:, cols]).astype(_bf16)
        return carry

    lax.fori_loop(0, R // NA_ROWS_PER_ITER, row_group, 0)


def _natten(proj, bias, out_w):
    S, T, _ = proj.shape
    rows = T // GRID_W
    assert rows >= NA_KH and rows % NA_ROWS_PER_STEP == 0
    nb = rows // NA_ROWS_PER_STEP
    blk_t = NA_ROWS_PER_STEP * GRID_W
    Wd = NA_WIDTH
    cur = lambda col: (lambda s, i: (s, i, col))
    prv = lambda col: (lambda s, i: (s, jnp.maximum(i - 1, 0), col))
    nxt = lambda col: (lambda s, i: (s, jnp.minimum(i + 1, nb - 1), col))
    blk = lambda im: pl.BlockSpec((None, blk_t, Wd), im)
    return pl.pallas_call(
        functools.partial(_natten_kernel, grid_rows=rows),
        grid=(S, nb),
        in_specs=[
            blk(cur(COL_AQ)),
            blk(prv(COL_AK)), blk(cur(COL_AK)), blk(nxt(COL_AK)),
            blk(prv(COL_AV)), blk(cur(COL_AV)), blk(nxt(COL_AV)),
            _const_spec(bias.shape), _const_spec(out_w.shape),
        ],
        out_specs=blk(cur(0)),
        out_shape=jax.ShapeDtypeStruct((S, T, Wd), _bf16),
        scratch_shapes=[pltpu.VMEM((3 * blk_t, Wd), _bf16), pltpu.VMEM((3 * blk_t, Wd), _bf16)],
        compiler_params=_params("parallel", "parallel"),
        name="natten",
    )(proj, proj, proj, proj, proj, proj, proj, bias, out_w)


def _out_proj_kernel(x_ref, of_ref, ob_ref, hg_ref, na_ref, w_ref, hw_ref, n2_ref, rw_ref, rb_ref, tril_ref,
                     x1_ref, h2_ref, route_ref, w1_ref, w2_ref, cnt_ref, carry_ref, *, steps_per_tile):
    @pl.when(pl.program_id(1) % steps_per_tile == 0)
    def _():
        carry_ref[...] = jnp.zeros_like(carry_ref)

    o = of_ref[...] + ob_ref[...]
    parts = []
    for hd in range(HG_HEADS):
        oh = o[:, hd * HG_D:(hd + 1) * HG_D]
        ms = jnp.mean(oh * oh, axis=-1, keepdims=True)
        parts.append(oh * lax.rsqrt(ms + EPS))
    on = jnp.concatenate(parts, axis=-1) * hw_ref[...]
    sig = 1.0 / (1.0 + jnp.exp(-hg_ref[...].astype(_f32)))
    mix_a = (on * sig).astype(_bf16)
    acc = _dot(mix_a, w_ref[0:HG_WIDTH, :]) + _dot(na_ref[...], w_ref[HG_WIDTH:, :])
    x1 = x_ref[...] + acc
    x1_ref[...] = x1
    ms2 = jnp.mean(x1 * x1, axis=-1, keepdims=True)
    h2 = x1 * lax.rsqrt(ms2 + EPS) * n2_ref[...]
    for c in range(TOKEN_SUBLANES):
        h2_ref[pl.ds(c, h2.shape[0], stride=TOKEN_SUBLANES), :] = h2[:, c * LANE:(c + 1) * LANE]

    h_hi, h_lo = _split2(h2)
    w_hi = rw_ref[0]
    w_lo = rw_ref[1]
    lg = _dot(h_hi, w_hi) + _dot(h_lo, w_hi) + _dot(h_hi, w_lo) + rb_ref[...]
    lane_i = lax.broadcasted_iota(jnp.int32, lg.shape, 1)
    lane = lane_i.astype(_f32)
    lane_grp = ((lane_i - EXPERT_LANE0) >> 3).astype(_f32)

    def first_argmax(vals, valid):
        masked = jnp.where(valid, vals, NEG_BIG)
        top = jnp.max(masked, axis=-1, keepdims=True)
        idx = jnp.min(jnp.where(valid & (masked == top), lane, float(ROUTER_LANES)), axis=-1, keepdims=True)
        return top, idx

    is_grp = lane_i < N_GROUPS
    g_top, g_idx = first_argmax(lg, is_grp)
    p_top = 1.0 / jnp.sum(jnp.where(is_grp, jnp.exp(lg - g_top), 0.0), axis=-1, keepdims=True)
    in_grp = (lane_i >= EXPERT_LANE0) & (lane_i < EXPERT_LANE0 + N_EXPERTS) & (lane_grp == g_idx)
    t1, i1 = first_argmax(lg, in_grp)
    t2, i2 = first_argmax(lg, in_grp & (lane != i1))
    e2 = jnp.exp(t2 - t1)
    w1 = p_top / (1.0 + e2)
    w2 = p_top * e2 / (1.0 + e2)
    w1_ref[...] = jnp.broadcast_to(w1, w1_ref.shape)
    w2_ref[...] = jnp.broadcast_to(w2, w2_ref.shape)

    sel1 = lane == i1
    sel2 = lane == i2
    onehot = jnp.where(sel1 | sel2, 1.0, 0.0)
    before = _dot(tril_ref[...], onehot.astype(_bf16)) + carry_ref[...]
    rank1 = jnp.sum(jnp.where(sel1, before, 0.0), axis=-1, keepdims=True)
    rank2 = jnp.sum(jnp.where(sel2, before, 0.0), axis=-1, keepdims=True)
    carry = carry_ref[...] + jnp.sum(onehot, axis=0, keepdims=True)
    carry_ref[...] = carry
    cnt_ref[...] = jnp.broadcast_to(carry, cnt_ref.shape)
    route_ref[...] = jnp.where(lane_i == 0, i1 - EXPERT_LANE0,
                     jnp.where(lane_i == 1, i2 - EXPERT_LANE0,
                     jnp.where(lane_i == 2, rank1, jnp.where(lane_i == 3, rank2, 0.0))))


def _out_proj(x, o_f, o_b, proj, na, w_out_bf, hw, n2w, rw, rb, tril, tm, moe_tm):
    S, T, D = x.shape
    W = HG_WIDTH
    n = T // tm
    row = lambda width, col=0: pl.BlockSpec((None, tm, width), lambda s, i: (s, i, col))
    lanes = jax.ShapeDtypeStruct((S, T, ROUTER_LANES), _f32)
    return pl.pallas_call(
        functools.partial(_out_proj_kernel, steps_per_tile=moe_tm // tm),
        grid=(S, n),
        in_specs=[
            row(D), row(W), row(W), row(W, COL_G), row(W),
            _const_spec(w_out_bf.shape), _const_spec(hw.shape), _const_spec(n2w.shape),
            _const_spec(rw.shape), _const_spec(rb.shape), _const_spec(tril.shape),
        ],
        out_specs=[row(D), pl.BlockSpec((None, tm * TOKEN_SUBLANES, LANE), lambda s, i: (s, i, 0)),
                   row(ROUTER_LANES), row(ROUTER_LANES), row(ROUTER_LANES),
                   pl.BlockSpec((None, None, 8, ROUTER_LANES), lambda s, i: (s, i, 0, 0))],
        out_shape=[
            jax.ShapeDtypeStruct((S, T, D), _f32),
            jax.ShapeDtypeStruct((S, T * TOKEN_SUBLANES, LANE), _f32),
            lanes, lanes, lanes,
            jax.ShapeDtypeStruct((S, n, 8, ROUTER_LANES), _f32),
        ],
        scratch_shapes=[pltpu.VMEM((1, ROUTER_LANES), _f32)],
        compiler_params=_params("parallel", "arbitrary"),
        name="out_proj",
    )(x, o_f, o_b, proj, na, w_out_bf, hw, n2w, rw, rb, tril)


MOE_WINDOWS = (128, 288)
MOE_UNROLL = 16
MOE_STREAM_BLOCK = 256
MOE_SLOTS = 2


def _moe_rows(tm):
    return 2 * tm + MOE_WINDOWS[-1]


def _moe_kernel(pos1_ref, pos2_ref, seg_ref, x1_hbm, h2_hbm, w1_ref, w2_ref, wg_ref, wu_ref, wd_ref, y_hbm,
                buf_ref, mix_ref, h2_stage, x1_stage, y_stage, h2_sem, x1_sem, y_sem, *, tm, sb):
    i = pl.program_id(0)
    e = pl.program_id(1)
    TS = TOKEN_SUBLANES
    nsb = tm // sb

    def token_tile(row):
        return pl.ds(pl.multiple_of(row * TS, TS), TS)

    def buffer_tiles(t):
        return tuple(pl.ds(pl.multiple_of(ref[i * tm + t], TS), TS) for ref in (pos1_ref, pos2_ref))

    def token_rows(blk):
        return pl.ds(pl.multiple_of(i * tm + blk * sb, sb), sb)

    def h2_copy(blk, slot):
        src = h2_hbm.at[pl.ds(pl.multiple_of((i * tm + blk * sb) * TS, sb * TS), sb * TS), :]
        return pltpu.make_async_copy(src, h2_stage.at[slot], h2_sem.at[slot])

    def x1_copy(blk, slot):
        return pltpu.make_async_copy(x1_hbm.at[token_rows(blk), :], x1_stage.at[slot], x1_sem.at[slot])

    def y_copy(blk, slot):
        return pltpu.make_async_copy(y_stage.at[slot], y_hbm.at[token_rows(blk), :], y_sem.at[slot])

    @pl.when(e == 0)
    def _():
        slack = pl.ds(2 * tm * TS, MOE_WINDOWS[-1] * TS)
        buf_ref[slack, :] = jnp.zeros((MOE_WINDOWS[-1] * TS, LANE), _f32)
        h2_copy(0, 0).start()

        def dispatch(blk, carry):
            slot = blk % MOE_SLOTS
            h2_copy(blk, slot).wait()

            @pl.when(blk + 1 < nsb)
            def _():
                h2_copy(blk + 1, 1 - slot).start()

            def tokens(grp, c2):
                for k in range(MOE_UNROLL):
                    tl = grp * MOE_UNROLL + k
                    p1, p2 = buffer_tiles(blk * sb + tl)
                    tok = h2_stage[slot, token_tile(tl), :]
                    buf_ref[p1, :] = tok
                    buf_ref[p2, :] = tok
                return c2

            lax.fori_loop(0, sb // MOE_UNROLL, tokens, 0)
            return carry

        lax.fori_loop(0, nsb, dispatch, 0)

    start = seg_ref[(i * N_EXPERTS + e) * 2]
    count = seg_ref[(i * N_EXPERTS + e) * 2 + 1]

    def window(rows_w, done):
        r0 = start + done
        chunk = lambda c: pl.ds(r0 * TS + c, rows_w, stride=TS)
        xs = [buf_ref[chunk(c), :] for c in range(TS)]
        xb = jnp.concatenate(xs, axis=-1).astype(_bf16)
        a = _dot(xb, wg_ref[...])
        u = _dot(xb, wu_ref[...])
        hmid = a * (1.0 / (1.0 + jnp.exp(-a))) * u
        out = _dot(hmid.astype(_bf16), wd_ref[...])
        valid = lax.broadcasted_iota(jnp.int32, (rows_w, LANE), 0) + done < count
        for c in range(TS):
            buf_ref[chunk(c), :] = jnp.where(valid, out[:, c * LANE:(c + 1) * LANE], xs[c])

    small, large = MOE_WINDOWS

    n_large = count // large
    tail = count - n_large * large

    def body(j, carry):
        window(large, j * large)
        return carry

    lax.fori_loop(0, n_large, body, 0)

    @pl.when(tail > small)
    def _():
        window(large, n_large * large)

    @pl.when((tail > 0) & (tail <= small))
    def _():
        window(small, n_large * large)

    @pl.when(e == N_EXPERTS - 1)
    def _():
        x1_copy(0, 0).start()

        def combine(blk, carry):
            slot = blk % MOE_SLOTS

            def tokens(grp, c2):
                for k in range(MOE_UNROLL):
                    tl = grp * MOE_UNROLL + k
                    t = blk * sb + tl
                    p1, p2 = buffer_tiles(t)
                    g1 = jnp.broadcast_to(w1_ref[pl.ds(t, 1), :], (TS, LANE))
                    g2 = jnp.broadcast_to(w2_ref[pl.ds(t, 1), :], (TS, LANE))
                    mix_ref[token_tile(tl), :] = g1 * buf_ref[p1, :] + g2 * buf_ref[p2, :]
                return c2

            lax.fori_loop(0, sb // MOE_UNROLL, tokens, 0)
            x1_copy(blk, slot).wait()

            @pl.when(blk + 1 < nsb)
            def _():
                x1_copy(blk + 1, 1 - slot).start()

            @pl.when(blk >= MOE_SLOTS)
            def _():
                y_copy(blk - MOE_SLOTS, slot).wait()

            for c in range(TS):
                cols = slice(c * LANE, (c + 1) * LANE)
                y_stage[slot, :, cols] = x1_stage[slot, :, cols] + mix_ref[pl.ds(c, sb, stride=TS), :]
            y_copy(blk, slot).start()
            return carry

        lax.fori_loop(0, nsb, combine, 0)
        for blk in range(nsb - MOE_SLOTS, nsb):
            y_copy(blk, blk % MOE_SLOTS).wait()


def _moe(pos1, pos2, seg, x1, h2, w1, w2, wg, wu, wd, tm):
    N, D = x1.shape
    sb = min(MOE_STREAM_BLOCK, tm // MOE_SLOTS)
    assert tm % sb == 0 and tm // sb >= MOE_SLOTS
    tile = lambda width: pl.BlockSpec((tm, width), lambda i, e, *_: (i, 0), pipeline_mode=pl.Buffered(1))
    wspec = lambda a, b: pl.BlockSpec((None, a, b), lambda i, e, *_: (e, 0, 0))
    hbm = pl.BlockSpec(memory_space=pl.ANY)
    return pl.pallas_call(
        functools.partial(_moe_kernel, tm=tm, sb=sb),
        grid_spec=pltpu.PrefetchScalarGridSpec(
            num_scalar_prefetch=3,
            grid=(N // tm, N_EXPERTS),
            in_specs=[
                hbm, hbm, tile(ROUTER_LANES), tile(ROUTER_LANES),
                wspec(D, EXPERT_FF), wspec(D, EXPERT_FF), wspec(EXPERT_FF, D),
            ],
            out_specs=hbm,
            scratch_shapes=[
                pltpu.VMEM((_moe_rows(tm) * TOKEN_SUBLANES, LANE), _f32),
                pltpu.VMEM((sb * TOKEN_SUBLANES, LANE), _f32),
                pltpu.VMEM((MOE_SLOTS, sb * TOKEN_SUBLANES, LANE), _f32),
                pltpu.VMEM((MOE_SLOTS, sb, D), _f32),
                pltpu.VMEM((MOE_SLOTS, sb, D), _f32),
                pltpu.SemaphoreType.DMA((MOE_SLOTS,)),
                pltpu.SemaphoreType.DMA((MOE_SLOTS,)),
                pltpu.SemaphoreType.DMA((MOE_SLOTS,)),
            ],
        ),
        out_shape=jax.ShapeDtypeStruct((N, D), _f32),
        compiler_params=_params("parallel", "arbitrary"),
        name="moe",
    )(pos1, pos2, seg, x1, h2, w1, w2, wg, wu, wd)


def _moe_plan(route, cnt, tm, steps_per_tile):
    N = route.shape[0]
    n_tiles = N // tm
    counts = cnt.reshape(-1, 8, ROUTER_LANES)[steps_per_tile - 1::steps_per_tile, 0,
                                               EXPERT_LANE0:EXPERT_LANE0 + N_EXPERTS].astype(jnp.int32)
    seg_start = jnp.cumsum(counts, axis=1) - counts
    r = route[:, :4].astype(jnp.int32).reshape(n_tiles, tm, 4)
    experts = jnp.arange(N_EXPERTS, dtype=jnp.int32)

    def row_of(expert, rank):
        onehot = expert[..., None] == experts
        return jnp.sum(jnp.where(onehot, seg_start[:, None, :], 0), axis=-1) + rank

    pos1 = row_of(r[..., 0], r[..., 2]) * TOKEN_SUBLANES
    pos2 = row_of(r[..., 1], r[..., 3]) * TOKEN_SUBLANES
    seg = jnp.stack([seg_start, counts], axis=-1)
    return pos1.reshape(-1), pos2.reshape(-1), seg.reshape(-1)


def _lower_bound(logits, l):
    return jnp.cumsum(jax.nn.softmax(logits.astype(_f32), axis=0), axis=0)[l]


ROW_TILE = 512
HGRN_BLOCK = 512
MOE_TILE = 4096


def _layer(x, l, c):
    S, T, D = x.shape
    tm = min(ROW_TILE, T)
    moe_tm = min(MOE_TILE, T)
    proj, gates = _in_proj(x, c["n1w"][l], c["w_in"][l], c["lb"][l], c["qkw"][l], tm)
    o_f, o_b = _hgrn2(proj, gates, c["dmat"], c["mask"], min(HGRN_BLOCK, T))
    na = _natten(proj, c["na_bias"][l], c["na_ow"][l])
    x1, h2, route, w1, w2, cnt = _out_proj(x, o_f, o_b, proj, na, c["w_out"][l], c["hw"][l], c["n2w"][l],
                                           c["rw"][l], c["rb"][l], c["tril"][:tm, :tm], tm, moe_tm)
    N = S * T
    flat = lambda a: a.reshape(N, a.shape[-1])
    pos1, pos2, seg = _moe_plan(flat(route), cnt, moe_tm, moe_tm // tm)
    y = _moe(pos1, pos2, seg, flat(x1), h2.reshape(N * TOKEN_SUBLANES, LANE), flat(w1), flat(w2),
             c["wg"][l], c["wu"][l], c["wd"][l], moe_tm)
    return y.reshape(S, T, D)


def _prepare(norm1_w, w_in, lb_fwd_logits, lb_bwd_logits, hgrn_norm_w, q_norm_w, k_norm_w, na_rpb,
             na_out_norm_w, w_out, norm2_w, w_router_group, b_router_group, w_router_expert,
             b_router_expert, w_expert_gate, w_expert_up, w_expert_down):
    depth = w_in.shape[0]
    dmat, mask = _hgrn_constants()
    pad = ROUTER_LANES - N_GROUPS - N_EXPERTS
    rw = jnp.pad(jnp.concatenate([w_router_group, w_router_expert], axis=-1).astype(_f32),
                 ((0, 0), (0, 0), (0, pad)))
    rw_hi = rw.astype(_bf16)
    rw_lo = (rw - rw_hi.astype(_f32)).astype(_bf16)
    rb = jnp.pad(jnp.concatenate([b_router_group, b_router_expert], axis=-1).astype(_f32),
                 ((0, 0), (0, pad)))
    return {
        "n1w": norm1_w.astype(_f32)[:, None, :],
        "w_in": w_in.astype(_bf16),
        "lb": jnp.stack([jnp.stack([_lower_bound(lb_fwd_logits, l), _lower_bound(lb_bwd_logits, l)])
                         for l in range(depth)]),
        "qkw": jnp.stack([jnp.tile(q_norm_w.astype(_f32), (1, NA_HEADS)) * (NA_DH ** -0.5),
                          jnp.tile(k_norm_w.astype(_f32), (1, NA_HEADS))], axis=1),
        "dmat": jnp.asarray(dmat, _bf16),
        "mask": jnp.asarray(mask, _f32),
        "na_bias": jnp.stack([_natten_bias(na_rpb[l]) for l in range(depth)]),
        "na_ow": na_out_norm_w.astype(_f32)[:, None, :],
        "w_out": w_out.astype(_bf16),
        "hw": hgrn_norm_w.astype(_f32)[:, None, :],
        "n2w": norm2_w.astype(_f32)[:, None, :],
        "rw": jnp.stack([rw_hi, rw_lo], axis=1),
        "rb": rb[:, None, :],
        "tril": jnp.asarray(np.tril(np.ones((ROW_TILE, ROW_TILE), np.float32), -1), _bf16),
        "wg": w_expert_gate.astype(_bf16),
        "wu": w_expert_up.astype(_bf16),
        "wd": w_expert_down.astype(_bf16),
    }


def kernel(x_prompt, x_sample, norm1_w, w_in, lb_fwd_logits, lb_bwd_logits, hgrn_norm_w, q_norm_w, k_norm_w, na_rpb, na_out_norm_w, w_out, norm2_w, w_router_group, b_router_group, w_router_expert, b_router_expert, w_expert_gate, w_expert_up, w_expert_down):
    c = _prepare(norm1_w, w_in, lb_fwd_logits, lb_bwd_logits, hgrn_norm_w, q_norm_w, k_norm_w, na_rpb,
                 na_out_norm_w, w_out, norm2_w, w_router_group, b_router_group, w_router_expert,
                 b_router_expert, w_expert_gate, w_expert_up, w_expert_down)
    outs = []
    for x in (x_prompt, x_sample):
        for l in range(w_in.shape[0]):
            x = _layer(x, l, c)
        outs.append(x)
    return tuple(outs)
```

```python
import functools

import numpy as np
import jax
import jax.numpy as jnp
from jax import lax
from jax.experimental import pallas as pl
from jax.experimental.pallas import tpu as pltpu

D_MODEL = 1024
EPS = 1e-6
GRID_W = 64
HG_HEADS = 4
HG_D = 128
HG_WIDTH = HG_HEADS * HG_D
HG_CHUNK = 64
HG_LEVELS = 6
HG_MATMUL_LEVELS = 3
HG_CHUNK_UNROLL = 4
NA_HEADS = 8
NA_DH = 64
NA_WIDTH = NA_HEADS * NA_DH
NA_KH = 8
NA_KW = 16
NA_ROWS_PER_STEP = 8
NA_ROWS_PER_ITER = 8
N_GROUPS = 4
EXPERTS_PER_GROUP = 8
N_EXPERTS = N_GROUPS * EXPERTS_PER_GROUP
EXPERT_FF = 512
ROUTER_LANES = 128
EXPERT_LANE0 = N_GROUPS
NEG_BIG = -1e30
LOG2_E = 1.4426950408889634
LANE = 128
TOKEN_SUBLANES = D_MODEL // LANE
VMEM_LIMIT = 56 * 1024 * 1024

COL_Q, COL_V, COL_G, COL_AQ, COL_AK, COL_AV = range(6)

_f32 = jnp.float32
_bf16 = jnp.bfloat16


def _dot(a, b):
    return jnp.dot(a, b, preferred_element_type=_f32)


def _dot_nt(a, b):
    return lax.dot_general(a, b, (((1,), (1,)), ((), ())), preferred_element_type=_f32)


def _dot_tn(a, b):
    return lax.dot_general(a, b, (((0,), (0,)), ((), ())), preferred_element_type=_f32)


def _split2(x):
    hi = x.astype(_bf16)
    lo = (x - hi.astype(_f32)).astype(_bf16)
    return hi, lo


def _head_pair_rsqrt(a):
    first = lax.broadcasted_iota(jnp.int32, a.shape, 1) < NA_DH
    sq = a * a
    s_first = jnp.sum(jnp.where(first, sq, 0.0), axis=-1, keepdims=True)
    s_second = jnp.sum(jnp.where(first, 0.0, sq), axis=-1, keepdims=True)
    return lax.rsqrt(jnp.where(first, s_first, s_second) * (1.0 / NA_DH) + EPS)


def _params(*sem):
    return pltpu.CompilerParams(dimension_semantics=sem, vmem_limit_bytes=VMEM_LIMIT)


def _const_spec(shape):
    nd = len(shape)
    return pl.BlockSpec(shape, lambda *_: (0,) * nd)


def _in_proj_kernel(x_ref, n1_ref, w_ref, lb_ref, qkw_ref, proj_ref, gate_ref):
    x = x_ref[...]
    ms = jnp.mean(x * x, axis=-1, keepdims=True)
    h = (x * lax.rsqrt(ms + EPS) * n1_ref[...]).astype(_bf16)
    W = HG_WIDTH

    def col(j):
        return _dot(h, w_ref[:, j * W:(j + 1) * W])

    proj_ref[:, COL_Q * W:(COL_Q + 1) * W] = col(0).astype(_bf16)
    for d in range(2):
        z = col(1 + d)
        lb = lb_ref[d:d + 1, :]
        f = lb + (1.0 - lb) * (1.0 / (1.0 + jnp.exp(-z)))
        gate_ref[:, d * W:(d + 1) * W] = jnp.log(f) * LOG2_E
    proj_ref[:, COL_V * W:(COL_V + 1) * W] = col(3).astype(_bf16)
    proj_ref[:, COL_G * W:(COL_G + 1) * W] = col(4).astype(_bf16)
    for d, c in ((0, COL_AQ), (1, COL_AK)):
        a = col(5 + d)
        for p in range(NA_HEADS // 2):
            ap = a[:, p * LANE:(p + 1) * LANE]
            lanes = slice(c * W + p * LANE, c * W + (p + 1) * LANE)
            proj_ref[:, lanes] = (ap * _head_pair_rsqrt(ap)
                                  * qkw_ref[d:d + 1, p * LANE:(p + 1) * LANE]).astype(_bf16)
    proj_ref[:, COL_AV * W:(COL_AV + 1) * W] = col(7).astype(_bf16)


def _in_proj(x, n1w, w_in_bf, lb, qkw, tm):
    S, T, D = x.shape
    return pl.pallas_call(
        _in_proj_kernel,
        grid=(S, T // tm),
        in_specs=[
            pl.BlockSpec((None, tm, D), lambda s, i: (s, i, 0)),
            _const_spec((1, D)),
            _const_spec(w_in_bf.shape),
            _const_spec(lb.shape),
            _const_spec(qkw.shape),
        ],
        out_specs=[
            pl.BlockSpec((None, tm, 6 * HG_WIDTH), lambda s, i: (s, i, 0)),
            pl.BlockSpec((None, tm, 2 * HG_WIDTH), lambda s, i: (s, i, 0)),
        ],
        out_shape=[
            jax.ShapeDtypeStruct((S, T, 6 * HG_WIDTH), _bf16),
            jax.ShapeDtypeStruct((S, T, 2 * HG_WIDTH), _f32),
        ],
        compiler_params=_params("parallel", "parallel"),
        name="in_proj",
    )(x, n1w, w_in_bf, lb, qkw)


def _hgrn_constants():
    C = HG_CHUNK
    idx = np.arange(C)
    dmats, masks = [], []
    for direction in (0, 1):
        if direction == 0:
            tri = (idx[None, :] <= idx[:, None]).astype(np.float64)
        else:
            tri = (idx[None, :] >= idx[:, None]).astype(np.float64)
        dm = [tri]
        mk = []
        for lvl in range(HG_LEVELS):
            h = 1 << lvl
            start = (idx // (2 * h)) * (2 * h)
            ref = start + h - 1 if direction == 0 else start + h
            upper = (idx & h) != 0
            later, earlier = (upper, ~upper) if direction == 0 else (~upper, upper)
            if lvl < HG_MATMUL_LEVELS:
                gather = np.zeros((C, C))
                gather[idx, ref] = 1.0
                dm.append(np.where(later[:, None], 1.0, -1.0) * (tri - gather @ tri))
            same = (idx[:, None] // (2 * h)) == (idx[None, :] // (2 * h))
            mk.append((same & later[:, None] & earlier[None, :]).astype(np.float64))
        dmats.append(np.concatenate(dm, axis=0))
        masks.append(np.stack(mk, axis=0))
    return np.stack(dmats, 0), np.stack(masks, 0)


def _hgrn_kernel(qf_ref, qb_ref, vf_ref, vb_ref, gf_ref, gb_ref, dmat_ref, mask_ref,
                 of_ref, ob_ref, state_ref, *, chunks):
    C = HG_CHUNK

    @pl.when(pl.program_id(1) == 0)
    def _():
        state_ref[...] = jnp.zeros_like(state_ref)

    def chunk_step(c, carry):
        deltas = []
        for direction, g_ref in enumerate((gf_ref, gb_ref)):
            cc = c if direction == 0 else chunks - 1 - c
            rows = pl.ds(pl.multiple_of(cc * C, C), C)
            g = g_ref[rows, :]
            g_hi, g_lo = _split2(g)
            dmat = dmat_ref[direction]
            deltas.append((rows, g, _dot(dmat, g_hi) + _dot(dmat, g_lo)))

        units = []
        for direction, (q_ref, v_ref, o_ref) in enumerate(((qf_ref, vf_ref, of_ref), (qb_ref, vb_ref, ob_ref))):
            rows, g, delta_all = deltas[direction]
            last = C - 1 if direction == 0 else 0
            for hd in range(HG_HEADS):
                cols = slice(hd * HG_D, (hd + 1) * HG_D)
                q = q_ref[rows, cols]
                v = v_ref[rows, cols]
                k = (1.0 - jnp.exp2(g[:, cols])).astype(_bf16)
                b = delta_all[0:C, cols]
                v32 = v.astype(_f32)
                diag = jnp.sum(q.astype(_f32) * k.astype(_f32), axis=-1, keepdims=True) * v32
                scores = None
                for lvl in range(HG_LEVELS):
                    if lvl < HG_MATMUL_LEVELS:
                        d = delta_all[(lvl + 1) * C:(lvl + 2) * C, cols]
                    else:
                        h = 1 << lvl
                        pieces = []
                        for s in range(0, C, 2 * h):
                            lo, hi = b[s:s + h, :], b[s + h:s + 2 * h, :]
                            if direction == 0:
                                ref = jnp.broadcast_to(b[s + h - 1:s + h, :], (h, HG_D))
                                pieces += [ref - lo, hi - ref]
                            else:
                                ref = jnp.broadcast_to(b[s + h:s + h + 1, :], (h, HG_D))
                                pieces += [lo - ref, ref - hi]
                        d = jnp.concatenate(pieces, axis=0)
                    e = jnp.exp2(d).astype(_bf16)
                    level = mask_ref[direction, lvl] * _dot_nt(q * e, k * e)
                    scores = level if scores is None else scores + level
                st = state_ref[direction, hd]
                b_last = b[last:last + 1, :]
                inter = diag + _dot_nt(q * jnp.exp2(b).astype(_bf16), st.astype(_bf16))
                k_dec = k * jnp.exp2(b_last - b).astype(_bf16)
                state_ref[direction, hd] = jnp.exp2(b_last) * st + _dot_tn(v, k_dec)
                units.append((o_ref, rows, cols, scores, v, inter))

        for o_ref, rows, cols, scores, v, inter in units:
            o_ref[rows, cols] = inter + _dot(scores.astype(_bf16), v)
        return carry

    lax.fori_loop(0, chunks, chunk_step, 0, unroll=HG_CHUNK_UNROLL)


def _hgrn2(proj, gates, dmat, mask, tb):
    S, T, _ = proj.shape
    n = T // tb
    W = HG_WIDTH
    fwd = lambda col: (lambda s, i: (s, i, col))
    bwd = lambda col: (lambda s, i: (s, n - 1 - i, col))
    blk = lambda im: pl.BlockSpec((None, tb, W), im)
    return pl.pallas_call(
        functools.partial(_hgrn_kernel, chunks=tb // HG_CHUNK),
        grid=(S, n),
        in_specs=[
            blk(fwd(COL_Q)), blk(bwd(COL_Q)), blk(fwd(COL_V)), blk(bwd(COL_V)),
            blk(fwd(0)), blk(bwd(1)),
            _const_spec(dmat.shape), _const_spec(mask.shape),
        ],
        out_specs=[blk(fwd(0)), blk(bwd(0))],
        out_shape=[jax.ShapeDtypeStruct((S, T, W), _f32)] * 2,
        scratch_shapes=[pltpu.VMEM((2, HG_HEADS, HG_D, HG_D), _f32)],
        compiler_params=_params("parallel", "arbitrary"),
        name="hgrn2",
    )(proj, proj, proj, proj, gates, gates, dmat, mask)


def _natten_bias(rpb):
    W = GRID_W
    case = np.arange(NA_KH)[:, None]
    a = np.arange(NA_KH)[None, :]
    dr = (a - case + (NA_KH - 1)).reshape(-1)
    sel_r = (dr[:, None] == np.arange(2 * NA_KH - 1)[None, :]).astype(np.float32)
    qc = np.arange(W)[:, None]
    kc = np.arange(W)[None, :]
    cs = np.clip(qc - NA_KW // 2, 0, W - NA_KW)
    valid = (kc >= cs) & (kc < cs + NA_KW)
    dc = np.clip(kc - qc + (NA_KW - 1), 0, 2 * NA_KW - 2).reshape(-1)
    sel_c = (dc[:, None] == np.arange(2 * NA_KW - 1)[None, :]).astype(np.float32)
    bias = jnp.einsum("hrc,sr,qc->hsq", rpb.astype(_f32), sel_r, sel_c, precision=lax.Precision.HIGHEST)
    bias = bias.reshape(NA_HEADS, NA_KH, NA_KH, W, W)
    bias = jnp.where(valid[None, None, None], bias, NEG_BIG)
    bias = bias.transpose(1, 0, 3, 2, 4)
    return bias.reshape(NA_KH, NA_HEADS // 2, 2 * W, NA_KH * W)


def _natten_kernel(q_ref, kp_ref, kc_ref, kn_ref, vp_ref, vc_ref, vn_ref, bias_ref, ow_ref,
                   o_ref, kbuf, vbuf, *, grid_rows):
    R = NA_ROWS_PER_STEP
    W = GRID_W
    blk = R * W
    for j, (k_r, v_r) in enumerate(((kp_ref, vp_ref), (kc_ref, vc_ref), (kn_ref, vn_ref))):
        kbuf[j * blk:(j + 1) * blk, :] = k_r[...]
        vbuf[j * blk:(j + 1) * blk, :] = v_r[...]
    r0 = pl.program_id(1) * R
    lane = lax.broadcasted_iota(jnp.int32, (W, LANE), 1)
    first = lane < NA_DH

    def row_group(ig, carry):
        units = []
        for k in range(NA_ROWS_PER_ITER):
            i = ig * NA_ROWS_PER_ITER + k
            r = r0 + i
            rs = jnp.clip(r - NA_KH // 2, 0, grid_rows - NA_KH)
            off = pl.multiple_of((rs - (r0 - R)) * W, W)
            for p in range(NA_HEADS // 2):
                units.append((pl.ds(pl.multiple_of(i * W, W), W), slice(p * LANE, (p + 1) * LANE),
                              pl.ds(off, NA_KH * W), r - rs, p))
        scores = []
        for qrows, cols, win, case, p in units:
            qp = q_ref[qrows, cols]
            zero = jnp.zeros_like(qp)
            lhs = jnp.concatenate([jnp.where(first, qp, zero), jnp.where(first, zero, qp)], axis=0)
            scores.append(_dot_nt(lhs, kbuf[win, cols]))
        probs = []
        for (qrows, cols, win, case, p), s in zip(units, scores):
            s = s + bias_ref[case, p]
            e = jnp.exp(s - jnp.max(s, axis=-1, keepdims=True))
            probs.append((e.astype(_bf16), jnp.sum(e, axis=-1, keepdims=True)))
        for (qrows, cols, win, case, p), (e, l) in zip(units, probs):
            pv = _dot(e, vbuf[win, cols]) / l
            out = jnp.where(first, pv[0:W], pv[W:2 * W])
            o_ref[qrows, cols] = (out * _head_pair_rsqrt(out) * ow_ref[:, cols]).astype(_bf16)
        return carry

    lax.fori_loop(0, R // NA_ROWS_PER_ITER, row_group, 0)


def _natten(proj, bias, out_w):
    S, T, _ = proj.shape
    rows = T // GRID_W
    assert rows >= NA_KH and rows % NA_ROWS_PER_STEP == 0
    nb = rows // NA_ROWS_PER_STEP
    blk_t = NA_ROWS_PER_STEP * GRID_W
    Wd = NA_WIDTH
    cur = lambda col: (lambda s, i: (s, i, col))
    prv = lambda col: (lambda s, i: (s, jnp.maximum(i - 1, 0), col))
    nxt = lambda col: (lambda s, i: (s, jnp.minimum(i + 1, nb - 1), col))
    blk = lambda im: pl.BlockSpec((None, blk_t, Wd), im)
    return pl.pallas_call(
        functools.partial(_natten_kernel, grid_rows=rows),
        grid=(S, nb),
        in_specs=[
            blk(cur(COL_AQ)),
            blk(prv(COL_AK)), blk(cur(COL_AK)), blk(nxt(COL_AK)),
            blk(prv(COL_AV)), blk(cur(COL_AV)), blk(nxt(COL_AV)),
            _const_spec(bias.shape), _const_spec(out_w.shape),
        ],
        out_specs=blk(cur(0)),
        out_shape=jax.ShapeDtypeStruct((S, T, Wd), _bf16),
        scratch_shapes=[pltpu.VMEM((3 * blk_t, Wd), _bf16), pltpu.VMEM((3 * blk_t, Wd), _bf16)],
        compiler_params=_params("parallel", "parallel"),
        name="natten",
    )(proj, proj, proj, proj, proj, proj, proj, bias, out_w)


def _out_proj_kernel(x_ref, of_ref, ob_ref, hg_ref, na_ref, w_ref, hw_ref, n2_ref, rw_ref, rb_ref, tril_ref,
                     x1_ref, h2_ref, route_ref, w1_ref, w2_ref, cnt_ref, carry_ref, *, steps_per_tile):
    @pl.when(pl.program_id(1) % steps_per_tile == 0)
    def _():
        carry_ref[...] = jnp.zeros_like(carry_ref)

    o = of_ref[...] + ob_ref[...]
    parts = []
    for hd in range(HG_HEADS):
        oh = o[:, hd * HG_D:(hd + 1) * HG_D]
        ms = jnp.mean(oh * oh, axis=-1, keepdims=True)
        parts.append(oh * lax.rsqrt(ms + EPS))
    on = jnp.concatenate(parts, axis=-1) * hw_ref[...]
    sig = 1.0 / (1.0 + jnp.exp(-hg_ref[...].astype(_f32)))
    mix_a = (on * sig).astype(_bf16)
    acc = _dot(mix_a, w_ref[0:HG_WIDTH, :]) + _dot(na_ref[...], w_ref[HG_WIDTH:, :])
    x1 = x_ref[...] + acc
    x1_ref[...] = x1
    ms2 = jnp.mean(x1 * x1, axis=-1, keepdims=True)
    h2 = x1 * lax.rsqrt(ms2 + EPS) * n2_ref[...]
    for c in range(TOKEN_SUBLANES):
        h2_ref[pl.ds(c, h2.shape[0], stride=TOKEN_SUBLANES), :] = h2[:, c * LANE:(c + 1) * LANE]

    h_hi, h_lo = _split2(h2)
    w_hi = rw_ref[0]
    w_lo = rw_ref[1]
    lg = _dot(h_hi, w_hi) + _dot(h_lo, w_hi) + _dot(h_hi, w_lo) + rb_ref[...]
    lane_i = lax.broadcasted_iota(jnp.int32, lg.shape, 1)
    lane = lane_i.astype(_f32)
    lane_grp = ((lane_i - EXPERT_LANE0) >> 3).astype(_f32)

    def first_argmax(vals, valid):
        masked = jnp.where(valid, vals, NEG_BIG)
        top = jnp.max(masked, axis=-1, keepdims=True)
        idx = jnp.min(jnp.where(valid & (masked == top), lane, float(ROUTER_LANES)), axis=-1, keepdims=True)
        return top, idx

    is_grp = lane_i < N_GROUPS
    g_top, g_idx = first_argmax(lg, is_grp)
    p_top = 1.0 / jnp.sum(jnp.where(is_grp, jnp.exp(lg - g_top), 0.0), axis=-1, keepdims=True)
    in_grp = (lane_i >= EXPERT_LANE0) & (lane_i < EXPERT_LANE0 + N_EXPERTS) & (lane_grp == g_idx)
    t1, i1 = first_argmax(lg, in_grp)
    t2, i2 = first_argmax(lg, in_grp & (lane != i1))
    e2 = jnp.exp(t2 - t1)
    w1 = p_top / (1.0 + e2)
    w2 = p_top * e2 / (1.0 + e2)
    w1_ref[...] = jnp.broadcast_to(w1, w1_ref.shape)
    w2_ref[...] = jnp.broadcast_to(w2, w2_ref.shape)

    sel1 = lane == i1
    sel2 = lane == i2
    onehot = jnp.where(sel1 | sel2, 1.0, 0.0)
    before = _dot(tril_ref[...], onehot.astype(_bf16)) + carry_ref[...]
    rank1 = jnp.sum(jnp.where(sel1, before, 0.0), axis=-1, keepdims=True)
    rank2 = jnp.sum(jnp.where(sel2, before, 0.0), axis=-1, keepdims=True)
    carry = carry_ref[...] + jnp.sum(onehot, axis=0, keepdims=True)
    carry_ref[...] = carry
    cnt_ref[...] = jnp.broadcast_to(carry, cnt_ref.shape)
    route_ref[...] = jnp.where(lane_i == 0, i1 - EXPERT_LANE0,
                     jnp.where(lane_i == 1, i2 - EXPERT_LANE0,
                     jnp.where(lane_i == 2, rank1, jnp.where(lane_i == 3, rank2, 0.0))))


def _out_proj(x, o_f, o_b, proj, na, w_out_bf, hw, n2w, rw, rb, tril, tm, moe_tm):
    S, T, D = x.shape
    W = HG_WIDTH
    n = T // tm
    row = lambda width, col=0: pl.BlockSpec((None, tm, width), lambda s, i: (s, i, col))
    lanes = jax.ShapeDtypeStruct((S, T, ROUTER_LANES), _f32)
    return pl.pallas_call(
        functools.partial(_out_proj_kernel, steps_per_tile=moe_tm // tm),
        grid=(S, n),
        in_specs=[
            row(D), row(W), row(W), row(W, COL_G), row(W),
            _const_spec(w_out_bf.shape), _const_spec(hw.shape), _const_spec(n2w.shape),
            _const_spec(rw.shape), _const_spec(rb.shape), _const_spec(tril.shape),
        ],
        out_specs=[row(D), pl.BlockSpec((None, tm * TOKEN_SUBLANES, LANE), lambda s, i: (s, i, 0)),
                   row(ROUTER_LANES), row(ROUTER_LANES), row(ROUTER_LANES),
                   pl.BlockSpec((None, None, 8, ROUTER_LANES), lambda s, i: (s, i, 0, 0))],
        out_shape=[
            jax.ShapeDtypeStruct((S, T, D), _f32),
            jax.ShapeDtypeStruct((S, T * TOKEN_SUBLANES, LANE), _f32),
            lanes, lanes, lanes,
            jax.ShapeDtypeStruct((S, n, 8, ROUTER_LANES), _f32),
        ],
        scratch_shapes=[pltpu.VMEM((1, ROUTER_LANES), _f32)],
        compiler_params=_params("parallel", "arbitrary"),
        name="out_proj",
    )(x, o_f, o_b, proj, na, w_out_bf, hw, n2w, rw, rb, tril)


MOE_WINDOWS = (128, 288)
MOE_UNROLL = 16
MOE_STREAM_BLOCK = 256
MOE_SLOTS = 3
MOE_AHEAD = MOE_SLOTS - 1


def _moe_rows(tm):
    return 2 * tm + MOE_WINDOWS[-1]


def _moe_kernel(pos1_ref, pos2_ref, seg_ref, x1_hbm, h2_hbm, w1_ref, w2_ref, wg_ref, wu_ref, wd_ref, y_hbm,
                buf_ref, mix_ref, h2_stage, x1_stage, y_stage, h2_sem, x1_sem, y_sem, *, tm, sb):
    i = pl.program_id(0)
    e = pl.program_id(1)
    TS = TOKEN_SUBLANES
    nsb = tm // sb

    def token_tile(row):
        return pl.ds(pl.multiple_of(row * TS, TS), TS)

    def buffer_tiles(t):
        return tuple(pl.ds(pl.multiple_of(ref[i * tm + t], TS), TS) for ref in (pos1_ref, pos2_ref))

    def token_rows(blk):
        return pl.ds(pl.multiple_of(i * tm + blk * sb, sb), sb)

    def h2_copy(blk, slot):
        src = h2_hbm.at[pl.ds(pl.multiple_of((i * tm + blk * sb) * TS, sb * TS), sb * TS), :]
        return pltpu.make_async_copy(src, h2_stage.at[slot], h2_sem.at[slot])

    def x1_copy(blk, slot):
        return pltpu.make_async_copy(x1_hbm.at[token_rows(blk), :], x1_stage.at[slot], x1_sem.at[slot])

    def y_copy(blk, slot):
        return pltpu.make_async_copy(y_stage.at[slot], y_hbm.at[token_rows(blk), :], y_sem.at[slot])

    @pl.when(e == 0)
    def _():
        slack = pl.ds(2 * tm * TS, MOE_WINDOWS[-1] * TS)
        buf_ref[slack, :] = jnp.zeros((MOE_WINDOWS[-1] * TS, LANE), _f32)
        for blk in range(MOE_AHEAD):
            h2_copy(blk, blk).start()

        def dispatch(blk, carry):
            slot = blk % MOE_SLOTS
            h2_copy(blk, slot).wait()

            @pl.when(blk + MOE_AHEAD < nsb)
            def _():
                h2_copy(blk + MOE_AHEAD, (blk + MOE_AHEAD) % MOE_SLOTS).start()

            def tokens(grp, c2):
                for k in range(MOE_UNROLL):
                    tl = grp * MOE_UNROLL + k
                    p1, p2 = buffer_tiles(blk * sb + tl)
                    tok = h2_stage[slot, token_tile(tl), :]
                    buf_ref[p1, :] = tok
                    buf_ref[p2, :] = tok
                return c2

            lax.fori_loop(0, sb // MOE_UNROLL, tokens, 0)
            return carry

        lax.fori_loop(0, nsb, dispatch, 0)

    start = seg_ref[(i * N_EXPERTS + e) * 2]
    count = seg_ref[(i * N_EXPERTS + e) * 2 + 1]

    def window(rows_w, done):
        r0 = start + done
        chunk = lambda c: pl.ds(r0 * TS + c, rows_w, stride=TS)
        xs = [buf_ref[chunk(c), :] for c in range(TS)]
        xb = jnp.concatenate(xs, axis=-1).astype(_bf16)
        a = _dot(xb, wg_ref[...])
        u = _dot(xb, wu_ref[...])
        hmid = a * (1.0 / (1.0 + jnp.exp(-a))) * u
        out = _dot(hmid.astype(_bf16), wd_ref[...])
        valid = lax.broadcasted_iota(jnp.int32, (rows_w, LANE), 0) + done < count
        for c in range(TS):
            buf_ref[chunk(c), :] = jnp.where(valid, out[:, c * LANE:(c + 1) * LANE], xs[c])

    small, large = MOE_WINDOWS

    n_large = count // large
    tail = count - n_large * large

    def body(j, carry):
        window(large, j * large)
        return carry

    lax.fori_loop(0, n_large, body, 0)

    @pl.when(tail > small)
    def _():
        window(large, n_large * large)

    @pl.when((tail > 0) & (tail <= small))
    def _():
        window(small, n_large * large)

    @pl.when(e == N_EXPERTS - 1)
    def _():
        for blk in range(MOE_AHEAD):
            x1_copy(blk, blk).start()

        def combine(blk, carry):
            slot = blk % MOE_SLOTS

            @pl.when(blk + MOE_AHEAD < nsb)
            def _():
                x1_copy(blk + MOE_AHEAD, (blk + MOE_AHEAD) % MOE_SLOTS).start()

            def tokens(grp, c2):
                for k in range(MOE_UNROLL):
                    tl = grp * MOE_UNROLL + k
                    t = blk * sb + tl
                    p1, p2 = buffer_tiles(t)
                    g1 = jnp.broadcast_to(w1_ref[pl.ds(t, 1), :], (TS, LANE))
                    g2 = jnp.broadcast_to(w2_ref[pl.ds(t, 1), :], (TS, LANE))
                    mix_ref[token_tile(tl), :] = g1 * buf_ref[p1, :] + g2 * buf_ref[p2, :]
                return c2

            lax.fori_loop(0, sb // MOE_UNROLL, tokens, 0)
            x1_copy(blk, slot).wait()

            @pl.when(blk >= MOE_SLOTS)
            def _():
                y_copy(blk - MOE_SLOTS, slot).wait()

            for c in range(TS):
                cols = slice(c * LANE, (c + 1) * LANE)
                y_stage[slot, :, cols] = x1_stage[slot, :, cols] + mix_ref[pl.ds(c, sb, stride=TS), :]
            y_copy(blk, slot).start()
            return carry

        lax.fori_loop(0, nsb, combine, 0)
        for blk in range(nsb - MOE_SLOTS, nsb):
            y_copy(blk, blk % MOE_SLOTS).wait()


def _moe(pos1, pos2, seg, x1, h2, w1, w2, wg, wu, wd, tm):
    N, D = x1.shape
    sb = min(MOE_STREAM_BLOCK, tm // 4)
    assert tm % sb == 0 and tm // sb >= MOE_SLOTS and sb % MOE_UNROLL == 0
    tile = lambda width: pl.BlockSpec((tm, width), lambda i, e, *_: (i, 0), pipeline_mode=pl.Buffered(1))
    wspec = lambda a, b: pl.BlockSpec((None, a, b), lambda i, e, *_: (e, 0, 0))
    hbm = pl.BlockSpec(memory_space=pl.ANY)
    return pl.pallas_call(
        functools.partial(_moe_kernel, tm=tm, sb=sb),
        grid_spec=pltpu.PrefetchScalarGridSpec(
            num_scalar_prefetch=3,
            grid=(N // tm, N_EXPERTS),
            in_specs=[
                hbm, hbm, tile(ROUTER_LANES), tile(ROUTER_LANES),
                wspec(D, EXPERT_FF), wspec(D, EXPERT_FF), wspec(EXPERT_FF, D),
            ],
            out_specs=hbm,
            scratch_shapes=[
                pltpu.VMEM((_moe_rows(tm) * TOKEN_SUBLANES, LANE), _f32),
                pltpu.VMEM((sb * TOKEN_SUBLANES, LANE), _f32),
                pltpu.VMEM((MOE_SLOTS, sb * TOKEN_SUBLANES, LANE), _f32),
                pltpu.VMEM((MOE_SLOTS, sb, D), _f32),
                pltpu.VMEM((MOE_SLOTS, sb, D), _f32),
                pltpu.SemaphoreType.DMA((MOE_SLOTS,)),
                pltpu.SemaphoreType.DMA((MOE_SLOTS,)),
                pltpu.SemaphoreType.DMA((MOE_SLOTS,)),
            ],
        ),
        out_shape=jax.ShapeDtypeStruct((N, D), _f32),
        compiler_params=_params("parallel", "arbitrary"),
        name="moe",
    )(pos1, pos2, seg, x1, h2, w1, w2, wg, wu, wd)


def _moe_plan(route, cnt, tm, steps_per_tile):
    N = route.shape[0]
    n_tiles = N // tm
    counts = cnt.reshape(-1, 8, ROUTER_LANES)[steps_per_tile - 1::steps_per_tile, 0,
                                               EXPERT_LANE0:EXPERT_LANE0 + N_EXPERTS].astype(jnp.int32)
    seg_start = jnp.cumsum(counts, axis=1) - counts
    r = route[:, :4].astype(jnp.int32).reshape(n_tiles, tm, 4)
    experts = jnp.arange(N_EXPERTS, dtype=jnp.int32)

    def row_of(expert, rank):
        onehot = expert[..., None] == experts
        return jnp.sum(jnp.where(onehot, seg_start[:, None, :], 0), axis=-1) + rank

    pos1 = row_of(r[..., 0], r[..., 2]) * TOKEN_SUBLANES
    pos2 = row_of(r[..., 1], r[..., 3]) * TOKEN_SUBLANES
    seg = jnp.stack([seg_start, counts], axis=-1)
    return pos1.reshape(-1), pos2.reshape(-1), seg.reshape(-1)


def _lower_bound(logits, l):
    return jnp.cumsum(jax.nn.softmax(logits.astype(_f32), axis=0), axis=0)[l]


ROW_TILE = 512
HGRN_BLOCK = 512
MOE_TILE = 4096


def _layer(x, l, c):
    S, T, D = x.shape
    tm = min(ROW_TILE, T)
    moe_tm = min(MOE_TILE, T)
    proj, gates = _in_proj(x, c["n1w"][l], c["w_in"][l], c["lb"][l], c["qkw"][l], tm)
    o_f, o_b = _hgrn2(proj, gates, c["dmat"], c["mask"], min(HGRN_BLOCK, T))
    na = _natten(proj, c["na_bias"][l], c["na_ow"][l])
    x1, h2, route, w1, w2, cnt = _out_proj(x, o_f, o_b, proj, na, c["w_out"][l], c["hw"][l], c["n2w"][l],
                                           c["rw"][l], c["rb"][l], c["tril"][:tm, :tm], tm, moe_tm)
    N = S * T
    flat = lambda a: a.reshape(N, a.shape[-1])
    pos1, pos2, seg = _moe_plan(flat(route), cnt, moe_tm, moe_tm // tm)
    y = _moe(pos1, pos2, seg, flat(x1), h2.reshape(N * TOKEN_SUBLANES, LANE), flat(w1), flat(w2),
             c["wg"][l], c["wu"][l], c["wd"][l], moe_tm)
    return y.reshape(S, T, D)


def _prepare(norm1_w, w_in, lb_fwd_logits, lb_bwd_logits, hgrn_norm_w, q_norm_w, k_norm_w, na_rpb,
             na_out_norm_w, w_out, norm2_w, w_router_group, b_router_group, w_router_expert,
             b_router_expert, w_expert_gate, w_expert_up, w_expert_down):
    depth = w_in.shape[0]
    dmat, mask = _hgrn_constants()
    pad = ROUTER_LANES - N_GROUPS - N_EXPERTS
    rw = jnp.pad(jnp.concatenate([w_router_group, w_router_expert], axis=-1).astype(_f32),
                 ((0, 0), (0, 0), (0, pad)))
    rw_hi = rw.astype(_bf16)
    rw_lo = (rw - rw_hi.astype(_f32)).astype(_bf16)
    rb = jnp.pad(jnp.concatenate([b_router_group, b_router_expert], axis=-1).astype(_f32),
                 ((0, 0), (0, pad)))
    return {
        "n1w": norm1_w.astype(_f32)[:, None, :],
        "w_in": w_in.astype(_bf16),
        "lb": jnp.stack([jnp.stack([_lower_bound(lb_fwd_logits, l), _lower_bound(lb_bwd_logits, l)])
                         for l in range(depth)]),
        "qkw": jnp.stack([jnp.tile(q_norm_w.astype(_f32), (1, NA_HEADS)) * (NA_DH ** -0.5),
                          jnp.tile(k_norm_w.astype(_f32), (1, NA_HEADS))], axis=1),
        "dmat": jnp.asarray(dmat, _bf16),
        "mask": jnp.asarray(mask, _f32),
        "na_bias": jnp.stack([_natten_bias(na_rpb[l]) for l in range(depth)]),
        "na_ow": na_out_norm_w.astype(_f32)[:, None, :],
        "w_out": w_out.astype(_bf16),
        "hw": hgrn_norm_w.astype(_f32)[:, None, :],
        "n2w": norm2_w.astype(_f32)[:, None, :],
        "rw": jnp.stack([rw_hi, rw_lo], axis=1),
        "rb": rb[:, None, :],
        "tril": jnp.asarray(np.tril(np.ones((ROW_TILE, ROW_TILE), np.float32), -1), _bf16),
        "wg": w_expert_gate.astype(_bf16),
        "wu": w_expert_up.astype(_bf16),
        "wd": w_expert_down.astype(_bf16),
    }


def kernel(x_prompt, x_sample, norm1_w, w_in, lb_fwd_logits, lb_bwd_logits, hgrn_norm_w, q_norm_w, k_norm_w, na_rpb, na_out_norm_w, w_out, norm2_w, w_router_group, b_router_group, w_router_expert, b_router_expert, w_expert_gate, w_expert_up, w_expert_down):
    c = _prepare(norm1_w, w_in, lb_fwd_logits, lb_bwd_logits, hgrn_norm_w, q_norm_w, k_norm_w, na_rpb,
                 na_out_norm_w, w_out, norm2_w, w_router_group, b_router_group, w_router_expert,
                 b_router_expert, w_expert_gate, w_expert_up, w_expert_down)
    outs = []
    for x in (x_prompt, x_sample):
        for l in range(w_in.shape[0]):
            x = _layer(x, l, c)
        outs.append(x)
    return tuple(outs)
```

```python
import functools

import numpy as np
import jax
import jax.numpy as jnp
from jax import lax
from jax.experimental import pallas as pl
from jax.experimental.pallas import tpu as pltpu

D_MODEL = 1024
EPS = 1e-6
GRID_W = 64
HG_HEADS = 4
HG_D = 128
HG_WIDTH = HG_HEADS * HG_D
HG_CHUNK = 64
HG_LEVELS = 6
HG_MATMUL_LEVELS = 3
HG_CHUNK_UNROLL = 4
OUT_PROJ_PARTS = 2
NA_HEADS = 8
NA_DH = 64
NA_WIDTH = NA_HEADS * NA_DH
NA_KH = 8
NA_KW = 16
NA_ROWS_PER_STEP = 8
NA_ROWS_PER_ITER = 8
N_GROUPS = 4
EXPERTS_PER_GROUP = 8
N_EXPERTS = N_GROUPS * EXPERTS_PER_GROUP
EXPERT_FF = 512
ROUTER_LANES = 128
EXPERT_LANE0 = N_GROUPS
NEG_BIG = -1e30
LOG2_E = 1.4426950408889634
LANE = 128
TOKEN_SUBLANES = D_MODEL // LANE
VMEM_LIMIT = 56 * 1024 * 1024

COL_Q, COL_V, COL_G, COL_AQ, COL_AK, COL_AV = range(6)

_f32 = jnp.float32
_bf16 = jnp.bfloat16


def _dot(a, b):
    return jnp.dot(a, b, preferred_element_type=_f32)


def _dot_nt(a, b):
    return lax.dot_general(a, b, (((1,), (1,)), ((), ())), preferred_element_type=_f32)


def _dot_tn(a, b):
    return lax.dot_general(a, b, (((0,), (0,)), ((), ())), preferred_element_type=_f32)


def _split2(x):
    hi = x.astype(_bf16)
    lo = (x - hi.astype(_f32)).astype(_bf16)
    return hi, lo


def _head_pair_rsqrt(a):
    first = lax.broadcasted_iota(jnp.int32, a.shape, 1) < NA_DH
    sq = a * a
    s_first = jnp.sum(jnp.where(first, sq, 0.0), axis=-1, keepdims=True)
    s_second = jnp.sum(jnp.where(first, 0.0, sq), axis=-1, keepdims=True)
    return lax.rsqrt(jnp.where(first, s_first, s_second) * (1.0 / NA_DH) + EPS)


def _params(*sem):
    return pltpu.CompilerParams(dimension_semantics=sem, vmem_limit_bytes=VMEM_LIMIT)


def _const_spec(shape):
    nd = len(shape)
    return pl.BlockSpec(shape, lambda *_: (0,) * nd)


def _in_proj_kernel(x_ref, n1_ref, w_ref, lb_ref, qkw_ref, proj_ref, gate_ref):
    x = x_ref[...]
    ms = jnp.mean(x * x, axis=-1, keepdims=True)
    h = (x * lax.rsqrt(ms + EPS) * n1_ref[...]).astype(_bf16)
    W = HG_WIDTH

    def col(j):
        return _dot(h, w_ref[:, j * W:(j + 1) * W])

    proj_ref[:, COL_Q * W:(COL_Q + 1) * W] = col(0).astype(_bf16)
    for d in range(2):
        z = col(1 + d)
        lb = lb_ref[d:d + 1, :]
        f = lb + (1.0 - lb) * (1.0 / (1.0 + jnp.exp(-z)))
        gate_ref[:, d * W:(d + 1) * W] = jnp.log(f) * LOG2_E
    proj_ref[:, COL_V * W:(COL_V + 1) * W] = col(3).astype(_bf16)
    proj_ref[:, COL_G * W:(COL_G + 1) * W] = col(4).astype(_bf16)
    for d, c in ((0, COL_AQ), (1, COL_AK)):
        a = col(5 + d)
        for p in range(NA_HEADS // 2):
            ap = a[:, p * LANE:(p + 1) * LANE]
            lanes = slice(c * W + p * LANE, c * W + (p + 1) * LANE)
            proj_ref[:, lanes] = (ap * _head_pair_rsqrt(ap)
                                  * qkw_ref[d:d + 1, p * LANE:(p + 1) * LANE]).astype(_bf16)
    proj_ref[:, COL_AV * W:(COL_AV + 1) * W] = col(7).astype(_bf16)


def _in_proj(x, n1w, w_in_bf, lb, qkw, tm):
    S, T, D = x.shape
    return pl.pallas_call(
        _in_proj_kernel,
        grid=(S, T // tm),
        in_specs=[
            pl.BlockSpec((None, tm, D), lambda s, i: (s, i, 0)),
            _const_spec((1, D)),
            _const_spec(w_in_bf.shape),
            _const_spec(lb.shape),
            _const_spec(qkw.shape),
        ],
        out_specs=[
            pl.BlockSpec((None, tm, 6 * HG_WIDTH), lambda s, i: (s, i, 0)),
            pl.BlockSpec((None, tm, 2 * HG_WIDTH), lambda s, i: (s, i, 0)),
        ],
        out_shape=[
            jax.ShapeDtypeStruct((S, T, 6 * HG_WIDTH), _bf16),
            jax.ShapeDtypeStruct((S, T, 2 * HG_WIDTH), _f32),
        ],
        compiler_params=_params("parallel", "parallel"),
        name="in_proj",
    )(x, n1w, w_in_bf, lb, qkw)


def _hgrn_constants():
    C = HG_CHUNK
    idx = np.arange(C)
    dmats, masks = [], []
    for direction in (0, 1):
        if direction == 0:
            tri = (idx[None, :] <= idx[:, None]).astype(np.float64)
        else:
            tri = (idx[None, :] >= idx[:, None]).astype(np.float64)
        dm = [tri]
        mk = []
        for lvl in range(HG_LEVELS):
            h = 1 << lvl
            start = (idx // (2 * h)) * (2 * h)
            ref = start + h - 1 if direction == 0 else start + h
            upper = (idx & h) != 0
            later, earlier = (upper, ~upper) if direction == 0 else (~upper, upper)
            if lvl < HG_MATMUL_LEVELS:
                gather = np.zeros((C, C))
                gather[idx, ref] = 1.0
                dm.append(np.where(later[:, None], 1.0, -1.0) * (tri - gather @ tri))
            same = (idx[:, None] // (2 * h)) == (idx[None, :] // (2 * h))
            mk.append((same & later[:, None] & earlier[None, :]).astype(np.float64))
        dmats.append(np.concatenate(dm, axis=0))
        masks.append(np.stack(mk, axis=0))
    return np.stack(dmats, 0), np.stack(masks, 0)


def _hgrn_kernel(qf_ref, qb_ref, vf_ref, vb_ref, gf_ref, gb_ref, dmat_ref, mask_ref,
                 of_ref, ob_ref, state_ref, *, chunks):
    C = HG_CHUNK

    @pl.when(pl.program_id(1) == 0)
    def _():
        state_ref[...] = jnp.zeros_like(state_ref)

    def chunk_step(c, carry):
        deltas = []
        for direction, g_ref in enumerate((gf_ref, gb_ref)):
            cc = c if direction == 0 else chunks - 1 - c
            rows = pl.ds(pl.multiple_of(cc * C, C), C)
            g = g_ref[rows, :]
            g_hi, g_lo = _split2(g)
            dmat = dmat_ref[direction]
            deltas.append((rows, g, _dot(dmat, g_hi) + _dot(dmat, g_lo)))

        units = []
        for direction, (q_ref, v_ref, o_ref) in enumerate(((qf_ref, vf_ref, of_ref), (qb_ref, vb_ref, ob_ref))):
            rows, g, delta_all = deltas[direction]
            last = C - 1 if direction == 0 else 0
            for hd in range(HG_HEADS):
                cols = slice(hd * HG_D, (hd + 1) * HG_D)
                q = q_ref[rows, cols]
                v = v_ref[rows, cols]
                k = (1.0 - jnp.exp2(g[:, cols])).astype(_bf16)
                b = delta_all[0:C, cols]
                v32 = v.astype(_f32)
                diag = jnp.sum(q.astype(_f32) * k.astype(_f32), axis=-1, keepdims=True) * v32
                scores = None
                for lvl in range(HG_LEVELS):
                    if lvl < HG_MATMUL_LEVELS:
                        d = delta_all[(lvl + 1) * C:(lvl + 2) * C, cols]
                    else:
                        h = 1 << lvl
                        pieces = []
                        for s in range(0, C, 2 * h):
                            lo, hi = b[s:s + h, :], b[s + h:s + 2 * h, :]
                            if direction == 0:
                                ref = jnp.broadcast_to(b[s + h - 1:s + h, :], (h, HG_D))
                                pieces += [ref - lo, hi - ref]
                            else:
                                ref = jnp.broadcast_to(b[s + h:s + h + 1, :], (h, HG_D))
                                pieces += [lo - ref, ref - hi]
                        d = jnp.concatenate(pieces, axis=0)
                    e = jnp.exp2(d).astype(_bf16)
                    level = mask_ref[direction, lvl] * _dot_nt(q * e, k * e)
                    scores = level if scores is None else scores + level
                st = state_ref[direction, hd]
                b_last = b[last:last + 1, :]
                inter = diag + _dot_nt(q * jnp.exp2(b).astype(_bf16), st.astype(_bf16))
                k_dec = k * jnp.exp2(b_last - b).astype(_bf16)
                state_ref[direction, hd] = jnp.exp2(b_last) * st + _dot_tn(v, k_dec)
                units.append((o_ref, rows, cols, scores, v, inter))

        for o_ref, rows, cols, scores, v, inter in units:
            o_ref[rows, cols] = inter + _dot(scores.astype(_bf16), v)
        return carry

    lax.fori_loop(0, chunks, chunk_step, 0, unroll=HG_CHUNK_UNROLL)


def _hgrn2(proj, gates, dmat, mask, tb):
    S, T, _ = proj.shape
    n = T // tb
    W = HG_WIDTH
    fwd = lambda col: (lambda s, i: (s, i, col))
    bwd = lambda col: (lambda s, i: (s, n - 1 - i, col))
    blk = lambda im: pl.BlockSpec((None, tb, W), im)
    return pl.pallas_call(
        functools.partial(_hgrn_kernel, chunks=tb // HG_CHUNK),
        grid=(S, n),
        in_specs=[
            blk(fwd(COL_Q)), blk(bwd(COL_Q)), blk(fwd(COL_V)), blk(bwd(COL_V)),
            blk(fwd(0)), blk(bwd(1)),
            _const_spec(dmat.shape), _const_spec(mask.shape),
        ],
        out_specs=[blk(fwd(0)), blk(bwd(0))],
        out_shape=[jax.ShapeDtypeStruct((S, T, W), _f32)] * 2,
        scratch_shapes=[pltpu.VMEM((2, HG_HEADS, HG_D, HG_D), _f32)],
        compiler_params=_params("parallel", "arbitrary"),
        name="hgrn2",
    )(proj, proj, proj, proj, gates, gates, dmat, mask)


def _natten_bias(rpb):
    W = GRID_W
    case = np.arange(NA_KH)[:, None]
    a = np.arange(NA_KH)[None, :]
    dr = (a - case + (NA_KH - 1)).reshape(-1)
    sel_r = (dr[:, None] == np.arange(2 * NA_KH - 1)[None, :]).astype(np.float32)
    qc = np.arange(W)[:, None]
    kc = np.arange(W)[None, :]
    cs = np.clip(qc - NA_KW // 2, 0, W - NA_KW)
    valid = (kc >= cs) & (kc < cs + NA_KW)
    dc = np.clip(kc - qc + (NA_KW - 1), 0, 2 * NA_KW - 2).reshape(-1)
    sel_c = (dc[:, None] == np.arange(2 * NA_KW - 1)[None, :]).astype(np.float32)
    bias = jnp.einsum("hrc,sr,qc->hsq", rpb.astype(_f32), sel_r, sel_c, precision=lax.Precision.HIGHEST)
    bias = bias.reshape(NA_HEADS, NA_KH, NA_KH, W, W)
    bias = jnp.where(valid[None, None, None], bias, NEG_BIG)
    bias = bias.transpose(1, 0, 3, 2, 4)
    return bias.reshape(NA_KH, NA_HEADS // 2, 2 * W, NA_KH * W)


def _natten_kernel(q_ref, kp_ref, kc_ref, kn_ref, vp_ref, vc_ref, vn_ref, bias_ref, ow_ref,
                   o_ref, kbuf, vbuf, *, grid_rows):
    R = NA_ROWS_PER_STEP
    W = GRID_W
    blk = R * W
    for j, (k_r, v_r) in enumerate(((kp_ref, vp_ref), (kc_ref, vc_ref), (kn_ref, vn_ref))):
        kbuf[j * blk:(j + 1) * blk, :] = k_r[...]
        vbuf[j * blk:(j + 1) * blk, :] = v_r[...]
    r0 = pl.program_id(1) * R
    lane = lax.broadcasted_iota(jnp.int32, (W, LANE), 1)
    first = lane < NA_DH

    def row_group(ig, carry):
        units = []
        for k in range(NA_ROWS_PER_ITER):
            i = ig * NA_ROWS_PER_ITER + k
            r = r0 + i
            rs = jnp.clip(r - NA_KH // 2, 0, grid_rows - NA_KH)
            off = pl.multiple_of((rs - (r0 - R)) * W, W)
            for p in range(NA_HEADS // 2):
                units.append((pl.ds(pl.multiple_of(i * W, W), W), slice(p * LANE, (p + 1) * LANE),
                              pl.ds(off, NA_KH * W), r - rs, p))
        scores = []
        for qrows, cols, win, case, p in units:
            qp = q_ref[qrows, cols]
            zero = jnp.zeros_like(qp)
            lhs = jnp.concatenate([jnp.where(first, qp, zero), jnp.where(first, zero, qp)], axis=0)
            scores.append(_dot_nt(lhs, kbuf[win, cols]))
        probs = []
        for (qrows, cols, win, case, p), s in zip(units, scores):
            s = s + bias_ref[case, p]
            e = jnp.exp(s - jnp.max(s, axis=-1, keepdims=True))
            probs.append((e.astype(_bf16), jnp.sum(e, axis=-1, keepdims=True)))
        for (qrows, cols, win, case, p), (e, l) in zip(units, probs):
            pv = _dot(e, vbuf[win, cols]) / l
            out = jnp.where(first, pv[0:W], pv[W:2 * W])
            o_ref[qrows, cols] = (out * _head_pair_rsqrt(out) * ow_ref[:, cols]).astype(_bf16)
        return carry

    lax.fori_loop(0, R // NA_ROWS_PER_ITER, row_group, 0)


def _natten(proj, bias, out_w):
    S, T, _ = proj.shape
    rows = T // GRID_W
    assert rows >= NA_KH and rows % NA_ROWS_PER_STEP == 0
    nb = rows // NA_ROWS_PER_STEP
    blk_t = NA_ROWS_PER_STEP * GRID_W
    Wd = NA_WIDTH
    cur = lambda col: (lambda s, i: (s, i, col))
    prv = lambda col: (lambda s, i: (s, jnp.maximum(i - 1, 0), col))
    nxt = lambda col: (lambda s, i: (s, jnp.minimum(i + 1, nb - 1), col))
    blk = lambda im: pl.BlockSpec((None, blk_t, Wd), im)
    return pl.pallas_call(
        functools.partial(_natten_kernel, grid_rows=rows),
        grid=(S, nb),
        in_specs=[
            blk(cur(COL_AQ)),
            blk(prv(COL_AK)), blk(cur(COL_AK)), blk(nxt(COL_AK)),
            blk(prv(COL_AV)), blk(cur(COL_AV)), blk(nxt(COL_AV)),
            _const_spec(bias.shape), _const_spec(out_w.shape),
        ],
        out_specs=blk(cur(0)),
        out_shape=jax.ShapeDtypeStruct((S, T, Wd), _bf16),
        scratch_shapes=[pltpu.VMEM((3 * blk_t, Wd), _bf16), pltpu.VMEM((3 * blk_t, Wd), _bf16)],
        compiler_params=_params("parallel", "parallel"),
        name="natten",
    )(proj, proj, proj, proj, proj, proj, proj, bias, out_w)


def _out_proj_kernel(x_ref, of_ref, ob_ref, hg_ref, na_ref, w_ref, hw_ref, n2_ref, rw_ref, rb_ref, tril_ref,
                     x1_ref, h2_ref, route_ref, w1_ref, w2_ref, cnt_ref, carry_ref, *, steps_per_tile):
    @pl.when(pl.program_id(1) % steps_per_tile == 0)
    def _():
        carry_ref[...] = jnp.zeros_like(carry_ref)

    tm = x_ref.shape[0]
    pr = tm // OUT_PROJ_PARTS
    parts_rows = [slice(j * pr, (j + 1) * pr) for j in range(OUT_PROJ_PARTS)]

    mixes = []
    for rows in parts_rows:
        o = of_ref[rows, :] + ob_ref[rows, :]
        heads = []
        for hd in range(HG_HEADS):
            oh = o[:, hd * HG_D:(hd + 1) * HG_D]
            ms = jnp.mean(oh * oh, axis=-1, keepdims=True)
            heads.append(oh * lax.rsqrt(ms + EPS))
        on = jnp.concatenate(heads, axis=-1) * hw_ref[...]
        sig = 1.0 / (1.0 + jnp.exp(-hg_ref[rows, :].astype(_f32)))
        mixes.append((on * sig).astype(_bf16))
    accs = [_dot(mix_a, w_ref[0:HG_WIDTH, :]) + _dot(na_ref[rows, :], w_ref[HG_WIDTH:, :])
            for rows, mix_a in zip(parts_rows, mixes)]
    splits = []
    for j, (rows, acc) in enumerate(zip(parts_rows, accs)):
        x1 = x_ref[rows, :] + acc
        x1_ref[rows, :] = x1
        ms2 = jnp.mean(x1 * x1, axis=-1, keepdims=True)
        h2 = x1 * lax.rsqrt(ms2 + EPS) * n2_ref[...]
        for c in range(TOKEN_SUBLANES):
            h2_ref[pl.ds(j * pr * TOKEN_SUBLANES + c, pr, stride=TOKEN_SUBLANES), :] = h2[:, c * LANE:(c + 1) * LANE]
        splits.append(_split2(h2))

    w_hi = rw_ref[0]
    w_lo = rw_ref[1]
    lg = jnp.concatenate([_dot(h_hi, w_hi) + _dot(h_lo, w_hi) + _dot(h_hi, w_lo) for h_hi, h_lo in splits],
                         axis=0) + rb_ref[...]
    lane_i = lax.broadcasted_iota(jnp.int32, lg.shape, 1)
    lane = lane_i.astype(_f32)
    lane_grp = ((lane_i - EXPERT_LANE0) >> 3).astype(_f32)

    def first_argmax(vals, valid):
        masked = jnp.where(valid, vals, NEG_BIG)
        top = jnp.max(masked, axis=-1, keepdims=True)
        idx = jnp.min(jnp.where(valid & (masked == top), lane, float(ROUTER_LANES)), axis=-1, keepdims=True)
        return top, idx

    is_grp = lane_i < N_GROUPS
    g_top, g_idx = first_argmax(lg, is_grp)
    p_top = 1.0 / jnp.sum(jnp.where(is_grp, jnp.exp(lg - g_top), 0.0), axis=-1, keepdims=True)
    in_grp = (lane_i >= EXPERT_LANE0) & (lane_i < EXPERT_LANE0 + N_EXPERTS) & (lane_grp == g_idx)
    t1, i1 = first_argmax(lg, in_grp)
    t2, i2 = first_argmax(lg, in_grp & (lane != i1))
    e2 = jnp.exp(t2 - t1)
    w1 = p_top / (1.0 + e2)
    w2 = p_top * e2 / (1.0 + e2)
    w1_ref[...] = jnp.broadcast_to(w1, w1_ref.shape)
    w2_ref[...] = jnp.broadcast_to(w2, w2_ref.shape)

    sel1 = lane == i1
    sel2 = lane == i2
    onehot = jnp.where(sel1 | sel2, 1.0, 0.0)
    before = _dot(tril_ref[...], onehot.astype(_bf16)) + carry_ref[...]
    rank1 = jnp.sum(jnp.where(sel1, before, 0.0), axis=-1, keepdims=True)
    rank2 = jnp.sum(jnp.where(sel2, before, 0.0), axis=-1, keepdims=True)
    carry = carry_ref[...] + jnp.sum(onehot, axis=0, keepdims=True)
    carry_ref[...] = carry
    cnt_ref[...] = jnp.broadcast_to(carry, cnt_ref.shape)
    route_ref[...] = jnp.where(lane_i == 0, i1 - EXPERT_LANE0,
                     jnp.where(lane_i == 1, i2 - EXPERT_LANE0,
                     jnp.where(lane_i == 2, rank1, jnp.where(lane_i == 3, rank2, 0.0))))


def _out_proj(x, o_f, o_b, proj, na, w_out_bf, hw, n2w, rw, rb, tril, tm, moe_tm):
    S, T, D = x.shape
    W = HG_WIDTH
    n = T // tm
    row = lambda width, col=0: pl.BlockSpec((None, tm, width), lambda s, i: (s, i, col))
    lanes = jax.ShapeDtypeStruct((S, T, ROUTER_LANES), _f32)
    return pl.pallas_call(
        functools.partial(_out_proj_kernel, steps_per_tile=moe_tm // tm),
        grid=(S, n),
        in_specs=[
            row(D), row(W), row(W), row(W, COL_G), row(W),
            _const_spec(w_out_bf.shape), _const_spec(hw.shape), _const_spec(n2w.shape),
            _const_spec(rw.shape), _const_spec(rb.shape), _const_spec(tril.shape),
        ],
        out_specs=[row(D), pl.BlockSpec((None, tm * TOKEN_SUBLANES, LANE), lambda s, i: (s, i, 0)),
                   row(ROUTER_LANES), row(ROUTER_LANES), row(ROUTER_LANES),
                   pl.BlockSpec((None, None, 8, ROUTER_LANES), lambda s, i: (s, i, 0, 0))],
        out_shape=[
            jax.ShapeDtypeStruct((S, T, D), _f32),
            jax.ShapeDtypeStruct((S, T * TOKEN_SUBLANES, LANE), _f32),
            lanes, lanes, lanes,
            jax.ShapeDtypeStruct((S, n, 8, ROUTER_LANES), _f32),
        ],
        scratch_shapes=[pltpu.VMEM((1, ROUTER_LANES), _f32)],
        compiler_params=_params("parallel", "arbitrary"),
        name="out_proj",
    )(x, o_f, o_b, proj, na, w_out_bf, hw, n2w, rw, rb, tril)


MOE_WINDOWS = (128, 288)
MOE_UNROLL = 16
MOE_STREAM_BLOCK = 256
MOE_SLOTS = 3
MOE_AHEAD = MOE_SLOTS - 1


def _moe_rows(tm):
    return 2 * tm + MOE_WINDOWS[-1]


def _moe_kernel(pos1_ref, pos2_ref, seg_ref, x1_hbm, h2_hbm, w1_ref, w2_ref, wg_ref, wu_ref, wd_ref, y_hbm,
                buf_ref, mix_ref, h2_stage, x1_stage, y_stage, h2_sem, x1_sem, y_sem, *, tm, sb):
    i = pl.program_id(0)
    e = pl.program_id(1)
    TS = TOKEN_SUBLANES
    nsb = tm // sb

    def token_tile(row):
        return pl.ds(pl.multiple_of(row * TS, TS), TS)

    def buffer_tiles(t):
        return tuple(pl.ds(pl.multiple_of(ref[i * tm + t], TS), TS) for ref in (pos1_ref, pos2_ref))

    def token_rows(blk):
        return pl.ds(pl.multiple_of(i * tm + blk * sb, sb), sb)

    def h2_copy(blk, slot, tile=i):
        src = h2_hbm.at[pl.ds(pl.multiple_of((tile * tm + blk * sb) * TS, sb * TS), sb * TS), :]
        return pltpu.make_async_copy(src, h2_stage.at[slot], h2_sem.at[slot])

    def x1_copy(blk, slot):
        return pltpu.make_async_copy(x1_hbm.at[token_rows(blk), :], x1_stage.at[slot], x1_sem.at[slot])

    def y_copy(blk, slot):
        return pltpu.make_async_copy(y_stage.at[slot], y_hbm.at[token_rows(blk), :], y_sem.at[slot])

    @pl.when(e == 0)
    def _():
        slack = pl.ds(2 * tm * TS, MOE_WINDOWS[-1] * TS)
        buf_ref[slack, :] = jnp.zeros((MOE_WINDOWS[-1] * TS, LANE), _f32)

        @pl.when(i == 0)
        def _():
            for blk in range(MOE_AHEAD):
                h2_copy(blk, blk).start()

        def dispatch(blk, carry):
            slot = blk % MOE_SLOTS
            h2_copy(blk, slot).wait()

            @pl.when(blk + MOE_AHEAD < nsb)
            def _():
                h2_copy(blk + MOE_AHEAD, (blk + MOE_AHEAD) % MOE_SLOTS).start()

            def tokens(grp, c2):
                for k in range(MOE_UNROLL):
                    tl = grp * MOE_UNROLL + k
                    p1, p2 = buffer_tiles(blk * sb + tl)
                    tok = h2_stage[slot, token_tile(tl), :]
                    buf_ref[p1, :] = tok
                    buf_ref[p2, :] = tok
                return c2

            lax.fori_loop(0, sb // MOE_UNROLL, tokens, 0)
            return carry

        lax.fori_loop(0, nsb, dispatch, 0)

    start = seg_ref[(i * N_EXPERTS + e) * 2]
    count = seg_ref[(i * N_EXPERTS + e) * 2 + 1]

    def window(rows_w, done):
        r0 = start + done
        chunk = lambda c: pl.ds(r0 * TS + c, rows_w, stride=TS)
        xs = [buf_ref[chunk(c), :] for c in range(TS)]
        xb = jnp.concatenate(xs, axis=-1).astype(_bf16)
        a = _dot(xb, wg_ref[...])
        u = _dot(xb, wu_ref[...])
        hmid = a * (1.0 / (1.0 + jnp.exp(-a))) * u
        out = _dot(hmid.astype(_bf16), wd_ref[...])
        valid = lax.broadcasted_iota(jnp.int32, (rows_w, LANE), 0) + done < count
        for c in range(TS):
            buf_ref[chunk(c), :] = jnp.where(valid, out[:, c * LANE:(c + 1) * LANE], xs[c])

    @pl.when(e == N_EXPERTS - 1)
    def _():
        for blk in range(MOE_AHEAD):
            x1_copy(blk, blk).start()

        @pl.when(i + 1 < pl.num_programs(0))
        def _():
            for blk in range(MOE_AHEAD):
                h2_copy(blk, blk, tile=i + 1).start()

    small, large = MOE_WINDOWS

    n_large = count // large
    tail = count - n_large * large

    def body(j, carry):
        window(large, j * large)
        return carry

    lax.fori_loop(0, n_large, body, 0)

    @pl.when(tail > small)
    def _():
        window(large, n_large * large)

    @pl.when((tail > 0) & (tail <= small))
    def _():
        window(small, n_large * large)

    @pl.when(e == N_EXPERTS - 1)
    def _():
        def combine(blk, carry):
            slot = blk % MOE_SLOTS

            @pl.when(blk + MOE_AHEAD < nsb)
            def _():
                x1_copy(blk + MOE_AHEAD, (blk + MOE_AHEAD) % MOE_SLOTS).start()

            def tokens(grp, c2):
                for k in range(MOE_UNROLL):
                    tl = grp * MOE_UNROLL + k
                    t = blk * sb + tl
                    p1, p2 = buffer_tiles(t)
                    g1 = jnp.broadcast_to(w1_ref[pl.ds(t, 1), :], (TS, LANE))
                    g2 = jnp.broadcast_to(w2_ref[pl.ds(t, 1), :], (TS, LANE))
                    mix_ref[token_tile(tl), :] = g1 * buf_ref[p1, :] + g2 * buf_ref[p2, :]
                return c2

            lax.fori_loop(0, sb // MOE_UNROLL, tokens, 0)
            x1_copy(blk, slot).wait()

            @pl.when((blk >= MOE_SLOTS) | (i > 0))
            def _():
                y_copy(blk, slot).wait()

            for c in range(TS):
                cols = slice(c * LANE, (c + 1) * LANE)
                y_stage[slot, :, cols] = x1_stage[slot, :, cols] + mix_ref[pl.ds(c, sb, stride=TS), :]
            y_copy(blk, slot).start()
            return carry

        lax.fori_loop(0, nsb, combine, 0)

        @pl.when(i + 1 == pl.num_programs(0))
        def _():
            for blk in range(nsb - MOE_SLOTS, nsb):
                y_copy(blk, blk % MOE_SLOTS).wait()


def _moe(pos1, pos2, seg, x1, h2, w1, w2, wg, wu, wd, tm):
    N, D = x1.shape
    sb = min(MOE_STREAM_BLOCK, tm // 4)
    assert tm % sb == 0 and tm // sb >= MOE_SLOTS and sb % MOE_UNROLL == 0
    tile = lambda width: pl.BlockSpec((tm, width), lambda i, e, *_: (i, 0), pipeline_mode=pl.Buffered(1))
    wspec = lambda a, b: pl.BlockSpec((None, a, b), lambda i, e, *_: (e, 0, 0))
    hbm = pl.BlockSpec(memory_space=pl.ANY)
    return pl.pallas_call(
        functools.partial(_moe_kernel, tm=tm, sb=sb),
        grid_spec=pltpu.PrefetchScalarGridSpec(
            num_scalar_prefetch=3,
            grid=(N // tm, N_EXPERTS),
            in_specs=[
                hbm, hbm, tile(ROUTER_LANES), tile(ROUTER_LANES),
                wspec(D, EXPERT_FF), wspec(D, EXPERT_FF), wspec(EXPERT_FF, D),
            ],
            out_specs=hbm,
            scratch_shapes=[
                pltpu.VMEM((_moe_rows(tm) * TOKEN_SUBLANES, LANE), _f32),
                pltpu.VMEM((sb * TOKEN_SUBLANES, LANE), _f32),
                pltpu.VMEM((MOE_SLOTS, sb * TOKEN_SUBLANES, LANE), _f32),
                pltpu.VMEM((MOE_SLOTS, sb, D), _f32),
                pltpu.VMEM((MOE_SLOTS, sb, D), _f32),
                pltpu.SemaphoreType.DMA((MOE_SLOTS,)),
                pltpu.SemaphoreType.DMA((MOE_SLOTS,)),
                pltpu.SemaphoreType.DMA((MOE_SLOTS,)),
            ],
        ),
        out_shape=jax.ShapeDtypeStruct((N, D), _f32),
        compiler_params=_params("arbitrary", "arbitrary"),
        name="moe",
    )(pos1, pos2, seg, x1, h2, w1, w2, wg, wu, wd)


def _moe_plan(route, cnt, tm, steps_per_tile):
    N = route.shape[0]
    n_tiles = N // tm
    counts = cnt.reshape(-1, 8, ROUTER_LANES)[steps_per_tile - 1::steps_per_tile, 0,
                                               EXPERT_LANE0:EXPERT_LANE0 + N_EXPERTS].astype(jnp.int32)
    seg_start = jnp.cumsum(counts, axis=1) - counts
    r = route[:, :4].astype(jnp.int32).reshape(n_tiles, tm, 4)
    experts = jnp.arange(N_EXPERTS, dtype=jnp.int32)

    def row_of(expert, rank):
        onehot = expert[..., None] == experts
        return jnp.sum(jnp.where(onehot, seg_start[:, None, :], 0), axis=-1) + rank

    pos1 = row_of(r[..., 0], r[..., 2]) * TOKEN_SUBLANES
    pos2 = row_of(r[..., 1], r[..., 3]) * TOKEN_SUBLANES
    seg = jnp.stack([seg_start, counts], axis=-1)
    return pos1.reshape(-1), pos2.reshape(-1), seg.reshape(-1)


def _lower_bound(logits, l):
    return jnp.cumsum(jax.nn.softmax(logits.astype(_f32), axis=0), axis=0)[l]


ROW_TILE = 512
HGRN_BLOCK = 512
MOE_TILE = 4096


def _layer(x, l, c):
    S, T, D = x.shape
    tm = min(ROW_TILE, T)
    moe_tm = min(MOE_TILE, T)
    proj, gates = _in_proj(x, c["n1w"][l], c["w_in"][l], c["lb"][l], c["qkw"][l], tm)
    o_f, o_b = _hgrn2(proj, gates, c["dmat"], c["mask"], min(HGRN_BLOCK, T))
    na = _natten(proj, c["na_bias"][l], c["na_ow"][l])
    x1, h2, route, w1, w2, cnt = _out_proj(x, o_f, o_b, proj, na, c["w_out"][l], c["hw"][l], c["n2w"][l],
                                           c["rw"][l], c["rb"][l], c["tril"][:tm, :tm], tm, moe_tm)
    N = S * T
    flat = lambda a: a.reshape(N, a.shape[-1])
    pos1, pos2, seg = _moe_plan(flat(route), cnt, moe_tm, moe_tm // tm)
    y = _moe(pos1, pos2, seg, flat(x1), h2.reshape(N * TOKEN_SUBLANES, LANE), flat(w1), flat(w2),
             c["wg"][l], c["wu"][l], c["wd"][l], moe_tm)
    return y.reshape(S, T, D)


def _prepare(norm1_w, w_in, lb_fwd_logits, lb_bwd_logits, hgrn_norm_w, q_norm_w, k_norm_w, na_rpb,
             na_out_norm_w, w_out, norm2_w, w_router_group, b_router_group, w_router_expert,
             b_router_expert, w_expert_gate, w_expert_up, w_expert_down):
    depth = w_in.shape[0]
    dmat, mask = _hgrn_constants()
    pad = ROUTER_LANES - N_GROUPS - N_EXPERTS
    rw = jnp.pad(jnp.concatenate([w_router_group, w_router_expert], axis=-1).astype(_f32),
                 ((0, 0), (0, 0), (0, pad)))
    rw_hi = rw.astype(_bf16)
    rw_lo = (rw - rw_hi.astype(_f32)).astype(_bf16)
    rb = jnp.pad(jnp.concatenate([b_router_group, b_router_expert], axis=-1).astype(_f32),
                 ((0, 0), (0, pad)))
    return {
        "n1w": norm1_w.astype(_f32)[:, None, :],
        "w_in": w_in.astype(_bf16),
        "lb": jnp.stack([jnp.stack([_lower_bound(lb_fwd_logits, l), _lower_bound(lb_bwd_logits, l)])
                         for l in range(depth)]),
        "qkw": jnp.stack([jnp.tile(q_norm_w.astype(_f32), (1, NA_HEADS)) * (NA_DH ** -0.5),
                          jnp.tile(k_norm_w.astype(_f32), (1, NA_HEADS))], axis=1),
        "dmat": jnp.asarray(dmat, _bf16),
        "mask": jnp.asarray(mask, _f32),
        "na_bias": jnp.stack([_natten_bias(na_rpb[l]) for l in range(depth)]),
        "na_ow": na_out_norm_w.astype(_f32)[:, None, :],
        "w_out": w_out.astype(_bf16),
        "hw": hgrn_norm_w.astype(_f32)[:, None, :],
        "n2w": norm2_w.astype(_f32)[:, None, :],
        "rw": jnp.stack([rw_hi, rw_lo], axis=1),
        "rb": rb[:, None, :],
        "tril": jnp.asarray(np.tril(np.ones((ROW_TILE, ROW_TILE), np.float32), -1), _bf16),
        "wg": w_expert_gate.astype(_bf16),
        "wu": w_expert_up.astype(_bf16),
        "wd": w_expert_down.astype(_bf16),
    }


def kernel(x_prompt, x_sample, norm1_w, w_in, lb_fwd_logits, lb_bwd_logits, hgrn_norm_w, q_norm_w, k_norm_w, na_rpb, na_out_norm_w, w_out, norm2_w, w_router_group, b_router_group, w_router_expert, b_router_expert, w_expert_gate, w_expert_up, w_expert_down):
    c = _prepare(norm1_w, w_in, lb_fwd_logits, lb_bwd_logits, hgrn_norm_w, q_norm_w, k_norm_w, na_rpb,
                 na_out_norm_w, w_out, norm2_w, w_router_group, b_router_group, w_router_expert,
                 b_router_expert, w_expert_gate, w_expert_up, w_expert_down)
    outs = []
    for x in (x_prompt, x_sample):
        for l in range(w_in.shape[0]):
            x = _layer(x, l, c)
        outs.append(x)
    return tuple(outs)
```

```python
import functools

import numpy as np
import jax
import jax.numpy as jnp
from jax import lax
from jax.experimental import pallas as pl
from jax.experimental.pallas import tpu as pltpu

D_MODEL = 1024
EPS = 1e-6
GRID_W = 64
HG_HEADS = 4
HG_D = 128
HG_WIDTH = HG_HEADS * HG_D
HG_CHUNK = 64
HG_LEVELS = 6
HG_MATMUL_LEVELS = 3
HG_CHUNK_UNROLL = 4
OUT_PROJ_PARTS = 2
NA_HEADS = 8
NA_DH = 64
NA_WIDTH = NA_HEADS * NA_DH
NA_KH = 8
NA_KW = 16
NA_ROWS_PER_STEP = 8
NA_ROWS_PER_ITER = 8
N_GROUPS = 4
EXPERTS_PER_GROUP = 8
N_EXPERTS = N_GROUPS * EXPERTS_PER_GROUP
EXPERT_FF = 512
ROUTER_LANES = 128
EXPERT_LANE0 = N_GROUPS
NEG_BIG = -1e30
LOG2_E = 1.4426950408889634
LANE = 128
TOKEN_SUBLANES = D_MODEL // LANE
VMEM_LIMIT = 56 * 1024 * 1024

COL_Q, COL_V, COL_G, COL_AQ, COL_AK, COL_AV = range(6)

_f32 = jnp.float32
_bf16 = jnp.bfloat16


def _dot(a, b):
    return jnp.dot(a, b, preferred_element_type=_f32)


def _dot_nt(a, b):
    return lax.dot_general(a, b, (((1,), (1,)), ((), ())), preferred_element_type=_f32)


def _dot_tn(a, b):
    return lax.dot_general(a, b, (((0,), (0,)), ((), ())), preferred_element_type=_f32)


def _split2(x):
    hi = x.astype(_bf16)
    lo = (x - hi.astype(_f32)).astype(_bf16)
    return hi, lo


def _head_pair_rsqrt(a):
    first = lax.broadcasted_iota(jnp.int32, a.shape, 1) < NA_DH
    sq = a * a
    s_first = jnp.sum(jnp.where(first, sq, 0.0), axis=-1, keepdims=True)
    s_second = jnp.sum(jnp.where(first, 0.0, sq), axis=-1, keepdims=True)
    return lax.rsqrt(jnp.where(first, s_first, s_second) * (1.0 / NA_DH) + EPS)


def _params(*sem):
    return pltpu.CompilerParams(dimension_semantics=sem, vmem_limit_bytes=VMEM_LIMIT)


def _const_spec(shape):
    nd = len(shape)
    return pl.BlockSpec(shape, lambda *_: (0,) * nd)


def _in_proj_kernel(x_ref, n1_ref, w_ref, lb_ref, qkw_ref, proj_ref, gate_ref):
    x = x_ref[...]
    ms = jnp.mean(x * x, axis=-1, keepdims=True)
    h = (x * lax.rsqrt(ms + EPS) * n1_ref[...]).astype(_bf16)
    W = HG_WIDTH

    def col(j):
        return _dot(h, w_ref[:, j * W:(j + 1) * W])

    proj_ref[:, COL_Q * W:(COL_Q + 1) * W] = col(0).astype(_bf16)
    for d in range(2):
        z = col(1 + d)
        lb = lb_ref[d:d + 1, :]
        f = lb + (1.0 - lb) * (1.0 / (1.0 + jnp.exp(-z)))
        gate_ref[:, d * W:(d + 1) * W] = jnp.log(f) * LOG2_E
    proj_ref[:, COL_V * W:(COL_V + 1) * W] = col(3).astype(_bf16)
    proj_ref[:, COL_G * W:(COL_G + 1) * W] = col(4).astype(_bf16)
    for d, c in ((0, COL_AQ), (1, COL_AK)):
        a = col(5 + d)
        for p in range(NA_HEADS // 2):
            ap = a[:, p * LANE:(p + 1) * LANE]
            lanes = slice(c * W + p * LANE, c * W + (p + 1) * LANE)
            proj_ref[:, lanes] = (ap * _head_pair_rsqrt(ap)
                                  * qkw_ref[d:d + 1, p * LANE:(p + 1) * LANE]).astype(_bf16)
    proj_ref[:, COL_AV * W:(COL_AV + 1) * W] = col(7).astype(_bf16)


def _in_proj(x, n1w, w_in_bf, lb, qkw, tm):
    S, T, D = x.shape
    return pl.pallas_call(
        _in_proj_kernel,
        grid=(S, T // tm),
        in_specs=[
            pl.BlockSpec((None, tm, D), lambda s, i: (s, i, 0)),
            _const_spec((1, D)),
            _const_spec(w_in_bf.shape),
            _const_spec(lb.shape),
            _const_spec(qkw.shape),
        ],
        out_specs=[
            pl.BlockSpec((None, tm, 6 * HG_WIDTH), lambda s, i: (s, i, 0)),
            pl.BlockSpec((None, tm, 2 * HG_WIDTH), lambda s, i: (s, i, 0)),
        ],
        out_shape=[
            jax.ShapeDtypeStruct((S, T, 6 * HG_WIDTH), _bf16),
            jax.ShapeDtypeStruct((S, T, 2 * HG_WIDTH), _f32),
        ],
        compiler_params=_params("parallel", "parallel"),
        name="in_proj",
    )(x, n1w, w_in_bf, lb, qkw)


def _hgrn_constants():
    C = HG_CHUNK
    idx = np.arange(C)
    dmats, masks = [], []
    for direction in (0, 1):
        if direction == 0:
            tri = (idx[None, :] <= idx[:, None]).astype(np.float64)
        else:
            tri = (idx[None, :] >= idx[:, None]).astype(np.float64)
        dm = [tri]
        mk = []
        for lvl in range(HG_LEVELS):
            h = 1 << lvl
            start = (idx // (2 * h)) * (2 * h)
            ref = start + h - 1 if direction == 0 else start + h
            upper = (idx & h) != 0
            later, earlier = (upper, ~upper) if direction == 0 else (~upper, upper)
            if lvl < HG_MATMUL_LEVELS:
                gather = np.zeros((C, C))
                gather[idx, ref] = 1.0
                dm.append(np.where(later[:, None], 1.0, -1.0) * (tri - gather @ tri))
            same = (idx[:, None] // (2 * h)) == (idx[None, :] // (2 * h))
            mk.append((same & later[:, None] & earlier[None, :]).astype(np.float64))
        dmats.append(np.concatenate(dm, axis=0))
        masks.append(np.stack(mk, axis=0))
    return np.stack(dmats, 0), np.stack(masks, 0)


def _hgrn_kernel(qf_ref, qb_ref, vf_ref, vb_ref, gf_ref, gb_ref, dmat_ref, mask_ref,
                 of_ref, ob_ref, state_ref, *, chunks):
    C = HG_CHUNK

    @pl.when(pl.program_id(1) == 0)
    def _():
        state_ref[...] = jnp.zeros_like(state_ref)

    def chunk_step(c, carry):
        deltas = []
        for direction, g_ref in enumerate((gf_ref, gb_ref)):
            cc = c if direction == 0 else chunks - 1 - c
            rows = pl.ds(pl.multiple_of(cc * C, C), C)
            g = g_ref[rows, :]
            g_hi, g_lo = _split2(g)
            dmat = dmat_ref[direction]
            deltas.append((rows, g, _dot(dmat, g_hi) + _dot(dmat, g_lo)))

        units = []
        for direction, (q_ref, v_ref, o_ref) in enumerate(((qf_ref, vf_ref, of_ref), (qb_ref, vb_ref, ob_ref))):
            rows, g, delta_all = deltas[direction]
            last = C - 1 if direction == 0 else 0
            for hd in range(HG_HEADS):
                cols = slice(hd * HG_D, (hd + 1) * HG_D)
                q = q_ref[rows, cols]
                v = v_ref[rows, cols]
                k = (1.0 - jnp.exp2(g[:, cols])).astype(_bf16)
                b = delta_all[0:C, cols]
                v32 = v.astype(_f32)
                diag = jnp.sum(q.astype(_f32) * k.astype(_f32), axis=-1, keepdims=True) * v32
                scores = None
                for lvl in range(HG_LEVELS):
                    if lvl < HG_MATMUL_LEVELS:
                        d = delta_all[(lvl + 1) * C:(lvl + 2) * C, cols]
                    else:
                        h = 1 << lvl
                        pieces = []
                        for s in range(0, C, 2 * h):
                            lo, hi = b[s:s + h, :], b[s + h:s + 2 * h, :]
                            if direction == 0:
                                ref = jnp.broadcast_to(b[s + h - 1:s + h, :], (h, HG_D))
                                pieces += [ref - lo, hi - ref]
                            else:
                                ref = jnp.broadcast_to(b[s + h:s + h + 1, :], (h, HG_D))
                                pieces += [lo - ref, ref - hi]
                        d = jnp.concatenate(pieces, axis=0)
                    e = jnp.exp2(d).astype(_bf16)
                    level = mask_ref[direction, lvl] * _dot_nt(q * e, k * e)
                    scores = level if scores is None else scores + level
                st = state_ref[direction, hd]
                b_last = b[last:last + 1, :]
                inter = diag + _dot_nt(q * jnp.exp2(b).astype(_bf16), st.astype(_bf16))
                k_dec = k * jnp.exp2(b_last - b).astype(_bf16)
                state_ref[direction, hd] = jnp.exp2(b_last) * st + _dot_tn(v, k_dec)
                units.append((o_ref, rows, cols, scores, v, inter))

        for o_ref, rows, cols, scores, v, inter in units:
            o_ref[rows, cols] = inter + _dot(scores.astype(_bf16), v)
        return carry

    lax.fori_loop(0, chunks, chunk_step, 0, unroll=HG_CHUNK_UNROLL)


def _hgrn2(proj, gates, dmat, mask, tb):
    S, T, _ = proj.shape
    n = T // tb
    W = HG_WIDTH
    fwd = lambda col: (lambda s, i: (s, i, col))
    bwd = lambda col: (lambda s, i: (s, n - 1 - i, col))
    blk = lambda im: pl.BlockSpec((None, tb, W), im)
    return pl.pallas_call(
        functools.partial(_hgrn_kernel, chunks=tb // HG_CHUNK),
        grid=(S, n),
        in_specs=[
            blk(fwd(COL_Q)), blk(bwd(COL_Q)), blk(fwd(COL_V)), blk(bwd(COL_V)),
            blk(fwd(0)), blk(bwd(1)),
            _const_spec(dmat.shape), _const_spec(mask.shape),
        ],
        out_specs=[blk(fwd(0)), blk(bwd(0))],
        out_shape=[jax.ShapeDtypeStruct((S, T, W), _f32)] * 2,
        scratch_shapes=[pltpu.VMEM((2, HG_HEADS, HG_D, HG_D), _f32)],
        compiler_params=_params("parallel", "arbitrary"),
        name="hgrn2",
    )(proj, proj, proj, proj, gates, gates, dmat, mask)


def _natten_bias(rpb):
    W = GRID_W
    case = np.arange(NA_KH)[:, None]
    a = np.arange(NA_KH)[None, :]
    dr = (a - case + (NA_KH - 1)).reshape(-1)
    sel_r = (dr[:, None] == np.arange(2 * NA_KH - 1)[None, :]).astype(np.float32)
    qc = np.arange(W)[:, None]
    kc = np.arange(W)[None, :]
    cs = np.clip(qc - NA_KW // 2, 0, W - NA_KW)
    valid = (kc >= cs) & (kc < cs + NA_KW)
    dc = np.clip(kc - qc + (NA_KW - 1), 0, 2 * NA_KW - 2).reshape(-1)
    sel_c = (dc[:, None] == np.arange(2 * NA_KW - 1)[None, :]).astype(np.float32)
    bias = jnp.einsum("hrc,sr,qc->hsq", rpb.astype(_f32), sel_r, sel_c, precision=lax.Precision.HIGHEST)
    bias = bias.reshape(NA_HEADS, NA_KH, NA_KH, W, W)
    bias = jnp.where(valid[None, None, None], bias * LOG2_E, NEG_BIG)
    bias = bias.transpose(1, 0, 3, 2, 4)
    return bias.reshape(NA_KH, NA_HEADS // 2, 2 * W, NA_KH * W)


def _natten_kernel(q_ref, kp_ref, kc_ref, kn_ref, vp_ref, vc_ref, vn_ref, bias_ref, ow_ref,
                   o_ref, kbuf, vbuf, *, grid_rows):
    R = NA_ROWS_PER_STEP
    W = GRID_W
    blk = R * W
    for j, (k_r, v_r) in enumerate(((kp_ref, vp_ref), (kc_ref, vc_ref), (kn_ref, vn_ref))):
        kbuf[j * blk:(j + 1) * blk, :] = k_r[...]
        vbuf[j * blk:(j + 1) * blk, :] = v_r[...]
    r0 = pl.program_id(1) * R
    lane = lax.broadcasted_iota(jnp.int32, (W, LANE), 1)
    first = lane < NA_DH

    def row_group(ig, carry):
        units = []
        for k in range(NA_ROWS_PER_ITER):
            i = ig * NA_ROWS_PER_ITER + k
            r = r0 + i
            rs = jnp.clip(r - NA_KH // 2, 0, grid_rows - NA_KH)
            off = pl.multiple_of((rs - (r0 - R)) * W, W)
            for p in range(NA_HEADS // 2):
                units.append((pl.ds(pl.multiple_of(i * W, W), W), slice(p * LANE, (p + 1) * LANE),
                              pl.ds(off, NA_KH * W), r - rs, p))
        scores = []
        for qrows, cols, win, case, p in units:
            qp = q_ref[qrows, cols]
            zero = jnp.zeros_like(qp)
            lhs = jnp.concatenate([jnp.where(first, qp, zero), jnp.where(first, zero, qp)], axis=0)
            scores.append(_dot_nt(lhs, kbuf[win, cols]))
        probs = []
        for (qrows, cols, win, case, p), s in zip(units, scores):
            s = s + bias_ref[case, p]
            e = jnp.exp2(s - jnp.max(s, axis=-1, keepdims=True))
            probs.append((e.astype(_bf16), jnp.sum(e, axis=-1, keepdims=True)))
        for (qrows, cols, win, case, p), (e, l) in zip(units, probs):
            pv = _dot(e, vbuf[win, cols]) / l
            out = jnp.where(first, pv[0:W], pv[W:2 * W])
            o_ref[qrows, cols] = (out * _head_pair_rsqrt(out) * ow_ref[:, cols]).astype(_bf16)
        return carry

    lax.fori_loop(0, R // NA_ROWS_PER_ITER, row_group, 0)


def _natten(proj, bias, out_w):
    S, T, _ = proj.shape
    rows = T // GRID_W
    assert rows >= NA_KH and rows % NA_ROWS_PER_STEP == 0
    nb = rows // NA_ROWS_PER_STEP
    blk_t = NA_ROWS_PER_STEP * GRID_W
    Wd = NA_WIDTH
    cur = lambda col: (lambda s, i: (s, i, col))
    prv = lambda col: (lambda s, i: (s, jnp.maximum(i - 1, 0), col))
    nxt = lambda col: (lambda s, i: (s, jnp.minimum(i + 1, nb - 1), col))
    blk = lambda im: pl.BlockSpec((None, blk_t, Wd), im)
    return pl.pallas_call(
        functools.partial(_natten_kernel, grid_rows=rows),
        grid=(S, nb),
        in_specs=[
            blk(cur(COL_AQ)),
            blk(prv(COL_AK)), blk(cur(COL_AK)), blk(nxt(COL_AK)),
            blk(prv(COL_AV)), blk(cur(COL_AV)), blk(nxt(COL_AV)),
            _const_spec(bias.shape), _const_spec(out_w.shape),
        ],
        out_specs=blk(cur(0)),
        out_shape=jax.ShapeDtypeStruct((S, T, Wd), _bf16),
        scratch_shapes=[pltpu.VMEM((3 * blk_t, Wd), _bf16), pltpu.VMEM((3 * blk_t, Wd), _bf16)],
        compiler_params=_params("parallel", "parallel"),
        name="natten",
    )(proj, proj, proj, proj, proj, proj, proj, bias, out_w)


def _out_proj_kernel(x_ref, of_ref, ob_ref, hg_ref, na_ref, w_ref, hw_ref, n2_ref, rw_ref, rb_ref, tril_ref,
                     x1_ref, h2_ref, route_ref, w1_ref, w2_ref, cnt_ref, carry_ref, *, steps_per_tile):
    @pl.when(pl.program_id(1) % steps_per_tile == 0)
    def _():
        carry_ref[...] = jnp.zeros_like(carry_ref)

    tm = x_ref.shape[0]
    pr = tm // OUT_PROJ_PARTS
    parts_rows = [slice(j * pr, (j + 1) * pr) for j in range(OUT_PROJ_PARTS)]

    mixes = []
    for rows in parts_rows:
        o = of_ref[rows, :] + ob_ref[rows, :]
        heads = []
        for hd in range(HG_HEADS):
            oh = o[:, hd * HG_D:(hd + 1) * HG_D]
            ms = jnp.mean(oh * oh, axis=-1, keepdims=True)
            heads.append(oh * lax.rsqrt(ms + EPS))
        on = jnp.concatenate(heads, axis=-1) * hw_ref[...]
        sig = 1.0 / (1.0 + jnp.exp2(hg_ref[rows, :].astype(_f32)))
        mixes.append((on * sig).astype(_bf16))
    accs = [_dot(mix_a, w_ref[0:HG_WIDTH, :]) + _dot(na_ref[rows, :], w_ref[HG_WIDTH:, :])
            for rows, mix_a in zip(parts_rows, mixes)]
    splits = []
    for j, (rows, acc) in enumerate(zip(parts_rows, accs)):
        x1 = x_ref[rows, :] + acc
        x1_ref[rows, :] = x1
        ms2 = jnp.mean(x1 * x1, axis=-1, keepdims=True)
        h2 = x1 * lax.rsqrt(ms2 + EPS) * n2_ref[...]
        for c in range(TOKEN_SUBLANES):
            h2_ref[pl.ds(j * pr * TOKEN_SUBLANES + c, pr, stride=TOKEN_SUBLANES), :] = h2[:, c * LANE:(c + 1) * LANE]
        splits.append(_split2(h2))

    w_hi = rw_ref[0]
    w_lo = rw_ref[1]
    lg = jnp.concatenate([_dot(h_hi, w_hi) + _dot(h_lo, w_hi) + _dot(h_hi, w_lo) for h_hi, h_lo in splits],
                         axis=0) + rb_ref[...]
    lane_i = lax.broadcasted_iota(jnp.int32, lg.shape, 1)
    lane = lane_i.astype(_f32)
    lane_grp = ((lane_i - EXPERT_LANE0) >> 3).astype(_f32)

    def first_argmax(vals, valid):
        masked = jnp.where(valid, vals, NEG_BIG)
        top = jnp.max(masked, axis=-1, keepdims=True)
        idx = jnp.min(jnp.where(valid & (masked == top), lane, float(ROUTER_LANES)), axis=-1, keepdims=True)
        return top, idx

    is_grp = lane_i < N_GROUPS
    g_top, g_idx = first_argmax(lg, is_grp)
    p_top = 1.0 / jnp.sum(jnp.where(is_grp, jnp.exp(lg - g_top), 0.0), axis=-1, keepdims=True)
    in_grp = (lane_i >= EXPERT_LANE0) & (lane_i < EXPERT_LANE0 + N_EXPERTS) & (lane_grp == g_idx)
    t1, i1 = first_argmax(lg, in_grp)
    t2, i2 = first_argmax(lg, in_grp & (lane != i1))
    e2 = jnp.exp(t2 - t1)
    w1 = p_top / (1.0 + e2)
    w2 = p_top * e2 / (1.0 + e2)
    w1_ref[...] = jnp.broadcast_to(w1, w1_ref.shape)
    w2_ref[...] = jnp.broadcast_to(w2, w2_ref.shape)

    sel1 = lane == i1
    sel2 = lane == i2
    onehot = jnp.where(sel1 | sel2, 1.0, 0.0)
    before = _dot(tril_ref[...], onehot.astype(_bf16)) + carry_ref[...]
    rank1 = jnp.sum(jnp.where(sel1, before, 0.0), axis=-1, keepdims=True)
    rank2 = jnp.sum(jnp.where(sel2, before, 0.0), axis=-1, keepdims=True)
    carry = carry_ref[...] + jnp.sum(onehot, axis=0, keepdims=True)
    carry_ref[...] = carry
    cnt_ref[...] = jnp.broadcast_to(carry, cnt_ref.shape)
    route_ref[...] = jnp.where(lane_i == 0, i1 - EXPERT_LANE0,
                     jnp.where(lane_i == 1, i2 - EXPERT_LANE0,
                     jnp.where(lane_i == 2, rank1, jnp.where(lane_i == 3, rank2, 0.0))))


def _out_proj(x, o_f, o_b, proj, na, w_out_bf, hw, n2w, rw, rb, tril, tm, moe_tm):
    S, T, D = x.shape
    W = HG_WIDTH
    n = T // tm
    row = lambda width, col=0: pl.BlockSpec((None, tm, width), lambda s, i: (s, i, col))
    lanes = jax.ShapeDtypeStruct((S, T, ROUTER_LANES), _f32)
    return pl.pallas_call(
        functools.partial(_out_proj_kernel, steps_per_tile=moe_tm // tm),
        grid=(S, n),
        in_specs=[
            row(D), row(W), row(W), row(W, COL_G), row(W),
            _const_spec(w_out_bf.shape), _const_spec(hw.shape), _const_spec(n2w.shape),
            _const_spec(rw.shape), _const_spec(rb.shape), _const_spec(tril.shape),
        ],
        out_specs=[row(D), pl.BlockSpec((None, tm * TOKEN_SUBLANES, LANE), lambda s, i: (s, i, 0)),
                   row(ROUTER_LANES), row(ROUTER_LANES), row(ROUTER_LANES),
                   pl.BlockSpec((None, None, 8, ROUTER_LANES), lambda s, i: (s, i, 0, 0))],
        out_shape=[
            jax.ShapeDtypeStruct((S, T, D), _f32),
            jax.ShapeDtypeStruct((S, T * TOKEN_SUBLANES, LANE), _f32),
            lanes, lanes, lanes,
            jax.ShapeDtypeStruct((S, n, 8, ROUTER_LANES), _f32),
        ],
        scratch_shapes=[pltpu.VMEM((1, ROUTER_LANES), _f32)],
        compiler_params=_params("parallel", "arbitrary"),
        name="out_proj",
    )(x, o_f, o_b, proj, na, w_out_bf, hw, n2w, rw, rb, tril)


MOE_WINDOWS = (128, 288)
MOE_UNROLL = 16
MOE_STREAM_BLOCK = 256
MOE_SLOTS = 3
MOE_AHEAD = MOE_SLOTS - 1


def _moe_rows(tm):
    return 2 * tm + MOE_WINDOWS[-1]


def _moe_kernel(pos1_ref, pos2_ref, seg_ref, x1_hbm, h2_hbm, w1_ref, w2_ref, wg_ref, wu_ref, wd_ref, y_hbm,
                buf_ref, mix_ref, h2_stage, x1_stage, y_stage, h2_sem, x1_sem, y_sem, *, tm, sb):
    i = pl.program_id(0)
    e = pl.program_id(1)
    TS = TOKEN_SUBLANES
    nsb = tm // sb

    def token_tile(row):
        return pl.ds(pl.multiple_of(row * TS, TS), TS)

    def buffer_tiles(t):
        return tuple(pl.ds(pl.multiple_of(ref[i * tm + t], TS), TS) for ref in (pos1_ref, pos2_ref))

    def token_rows(blk):
        return pl.ds(pl.multiple_of(i * tm + blk * sb, sb), sb)

    def h2_copy(blk, slot, tile=i):
        src = h2_hbm.at[pl.ds(pl.multiple_of((tile * tm + blk * sb) * TS, sb * TS), sb * TS), :]
        return pltpu.make_async_copy(src, h2_stage.at[slot], h2_sem.at[slot])

    def x1_copy(blk, slot):
        return pltpu.make_async_copy(x1_hbm.at[token_rows(blk), :], x1_stage.at[slot], x1_sem.at[slot])

    def y_copy(blk, slot):
        return pltpu.make_async_copy(y_stage.at[slot], y_hbm.at[token_rows(blk), :], y_sem.at[slot])

    @pl.when(e == 0)
    def _():
        slack = pl.ds(2 * tm * TS, MOE_WINDOWS[-1] * TS)
        buf_ref[slack, :] = jnp.zeros((MOE_WINDOWS[-1] * TS, LANE), _f32)

        @pl.when(i == 0)
        def _():
            for blk in range(MOE_AHEAD):
                h2_copy(blk, blk).start()

        def dispatch(blk, carry):
            slot = blk % MOE_SLOTS
            h2_copy(blk, slot).wait()

            @pl.when(blk + MOE_AHEAD < nsb)
            def _():
                h2_copy(blk + MOE_AHEAD, (blk + MOE_AHEAD) % MOE_SLOTS).start()

            def tokens(grp, c2):
                for k in range(MOE_UNROLL):
                    tl = grp * MOE_UNROLL + k
                    p1, p2 = buffer_tiles(blk * sb + tl)
                    tok = h2_stage[slot, token_tile(tl), :]
                    buf_ref[p1, :] = tok
                    buf_ref[p2, :] = tok
                return c2

            lax.fori_loop(0, sb // MOE_UNROLL, tokens, 0)
            return carry

        lax.fori_loop(0, nsb, dispatch, 0)

    start = seg_ref[(i * N_EXPERTS + e) * 2]
    count = seg_ref[(i * N_EXPERTS + e) * 2 + 1]

    def window(rows_w, done):
        r0 = start + done
        chunk = lambda c: pl.ds(r0 * TS + c, rows_w, stride=TS)
        xs = [buf_ref[chunk(c), :] for c in range(TS)]
        xb = jnp.concatenate(xs, axis=-1).astype(_bf16)
        a = _dot(xb, wg_ref[...])
        u = _dot(xb, wu_ref[...])
        hmid = a * (1.0 / (1.0 + jnp.exp(-a))) * u
        out = _dot(hmid.astype(_bf16), wd_ref[...])
        valid = lax.broadcasted_iota(jnp.int32, (rows_w, LANE), 0) + done < count
        for c in range(TS):
            buf_ref[chunk(c), :] = jnp.where(valid, out[:, c * LANE:(c + 1) * LANE], xs[c])

    @pl.when(e == N_EXPERTS - 1)
    def _():
        for blk in range(MOE_AHEAD):
            x1_copy(blk, blk).start()

        @pl.when(i + 1 < pl.num_programs(0))
        def _():
            for blk in range(MOE_AHEAD):
                h2_copy(blk, blk, tile=i + 1).start()

    small, large = MOE_WINDOWS

    n_large = count // large
    tail = count - n_large * large

    def body(j, carry):
        window(large, j * large)
        return carry

    lax.fori_loop(0, n_large, body, 0)

    @pl.when(tail > small)
    def _():
        window(large, n_large * large)

    @pl.when((tail > 0) & (tail <= small))
    def _():
        window(small, n_large * large)

    @pl.when(e == N_EXPERTS - 1)
    def _():
        def combine(blk, carry):
            slot = blk % MOE_SLOTS

            @pl.when(blk + MOE_AHEAD < nsb)
            def _():
                x1_copy(blk + MOE_AHEAD, (blk + MOE_AHEAD) % MOE_SLOTS).start()

            def tokens(grp, c2):
                for k in range(MOE_UNROLL):
                    tl = grp * MOE_UNROLL + k
                    t = blk * sb + tl
                    p1, p2 = buffer_tiles(t)
                    g1 = jnp.broadcast_to(w1_ref[pl.ds(t, 1), :], (TS, LANE))
                    g2 = jnp.broadcast_to(w2_ref[pl.ds(t, 1), :], (TS, LANE))
                    mix_ref[token_tile(tl), :] = g1 * buf_ref[p1, :] + g2 * buf_ref[p2, :]
                return c2

            lax.fori_loop(0, sb // MOE_UNROLL, tokens, 0)
            x1_copy(blk, slot).wait()

            @pl.when((blk >= MOE_SLOTS) | (i > 0))
            def _():
                y_copy(blk, slot).wait()

            for c in range(TS):
                cols = slice(c * LANE, (c + 1) * LANE)
                y_stage[slot, :, cols] = x1_stage[slot, :, cols] + mix_ref[pl.ds(c, sb, stride=TS), :]
            y_copy(blk, slot).start()
            return carry

        lax.fori_loop(0, nsb, combine, 0)

        @pl.when(i + 1 == pl.num_programs(0))
        def _():
            for blk in range(nsb - MOE_SLOTS, nsb):
                y_copy(blk, blk % MOE_SLOTS).wait()


def _moe(pos1, pos2, seg, x1, h2, w1, w2, wg, wu, wd, tm):
    N, D = x1.shape
    sb = min(MOE_STREAM_BLOCK, tm // 4)
    assert tm % sb == 0 and tm // sb >= MOE_SLOTS and sb % MOE_UNROLL == 0
    tile = lambda width: pl.BlockSpec((tm, width), lambda i, e, *_: (i, 0), pipeline_mode=pl.Buffered(1))
    wspec = lambda a, b: pl.BlockSpec((None, a, b), lambda i, e, *_: (e, 0, 0))
    hbm = pl.BlockSpec(memory_space=pl.ANY)
    return pl.pallas_call(
        functools.partial(_moe_kernel, tm=tm, sb=sb),
        grid_spec=pltpu.PrefetchScalarGridSpec(
            num_scalar_prefetch=3,
            grid=(N // tm, N_EXPERTS),
            in_specs=[
                hbm, hbm, tile(ROUTER_LANES), tile(ROUTER_LANES),
                wspec(D, EXPERT_FF), wspec(D, EXPERT_FF), wspec(EXPERT_FF, D),
            ],
            out_specs=hbm,
            scratch_shapes=[
                pltpu.VMEM((_moe_rows(tm) * TOKEN_SUBLANES, LANE), _f32),
                pltpu.VMEM((sb * TOKEN_SUBLANES, LANE), _f32),
                pltpu.VMEM((MOE_SLOTS, sb * TOKEN_SUBLANES, LANE), _f32),
                pltpu.VMEM((MOE_SLOTS, sb, D), _f32),
                pltpu.VMEM((MOE_SLOTS, sb, D), _f32),
                pltpu.SemaphoreType.DMA((MOE_SLOTS,)),
                pltpu.SemaphoreType.DMA((MOE_SLOTS,)),
                pltpu.SemaphoreType.DMA((MOE_SLOTS,)),
            ],
        ),
        out_shape=jax.ShapeDtypeStruct((N, D), _f32),
        compiler_params=_params("arbitrary", "arbitrary"),
        name="moe",
    )(pos1, pos2, seg, x1, h2, w1, w2, wg, wu, wd)


def _moe_plan(route, cnt, tm, steps_per_tile):
    N = route.shape[0]
    n_tiles = N // tm
    counts = cnt.reshape(-1, 8, ROUTER_LANES)[steps_per_tile - 1::steps_per_tile, 0,
                                               EXPERT_LANE0:EXPERT_LANE0 + N_EXPERTS].astype(jnp.int32)
    seg_start = jnp.cumsum(counts, axis=1) - counts
    r = route[:, :4].astype(jnp.int32).reshape(n_tiles, tm, 4)
    experts = jnp.arange(N_EXPERTS, dtype=jnp.int32)

    def row_of(expert, rank):
        onehot = expert[..., None] == experts
        return jnp.sum(jnp.where(onehot, seg_start[:, None, :], 0), axis=-1) + rank

    pos1 = row_of(r[..., 0], r[..., 2]) * TOKEN_SUBLANES
    pos2 = row_of(r[..., 1], r[..., 3]) * TOKEN_SUBLANES
    seg = jnp.stack([seg_start, counts], axis=-1)
    return pos1.reshape(-1), pos2.reshape(-1), seg.reshape(-1)


def _lower_bound(logits, l):
    return jnp.cumsum(jax.nn.softmax(logits.astype(_f32), axis=0), axis=0)[l]


ROW_TILE = 512
HGRN_BLOCK = 512
MOE_TILE = 4096


def _layer(x, l, c):
    S, T, D = x.shape
    tm = min(ROW_TILE, T)
    moe_tm = min(MOE_TILE, T)
    proj, gates = _in_proj(x, c["n1w"][l], c["w_in"][l], c["lb"][l], c["qkw"][l], tm)
    o_f, o_b = _hgrn2(proj, gates, c["dmat"], c["mask"], min(HGRN_BLOCK, T))
    na = _natten(proj, c["na_bias"][l], c["na_ow"][l])
    x1, h2, route, w1, w2, cnt = _out_proj(x, o_f, o_b, proj, na, c["w_out"][l], c["hw"][l], c["n2w"][l],
                                           c["rw"][l], c["rb"][l], c["tril"][:tm, :tm], tm, moe_tm)
    N = S * T
    flat = lambda a: a.reshape(N, a.shape[-1])
    pos1, pos2, seg = _moe_plan(flat(route), cnt, moe_tm, moe_tm // tm)
    y = _moe(pos1, pos2, seg, flat(x1), h2.reshape(N * TOKEN_SUBLANES, LANE), flat(w1), flat(w2),
             c["wg"][l], c["wu"][l], c["wd"][l], moe_tm)
    return y.reshape(S, T, D)


def _prepare(norm1_w, w_in, lb_fwd_logits, lb_bwd_logits, hgrn_norm_w, q_norm_w, k_norm_w, na_rpb,
             na_out_norm_w, w_out, norm2_w, w_router_group, b_router_group, w_router_expert,
             b_router_expert, w_expert_gate, w_expert_up, w_expert_down):
    depth = w_in.shape[0]
    dmat, mask = _hgrn_constants()
    col_scale = np.ones((1, 1, w_in.shape[-1]), np.float32)
    col_scale[..., 4 * HG_WIDTH:5 * HG_WIDTH] = -LOG2_E
    pad = ROUTER_LANES - N_GROUPS - N_EXPERTS
    rw = jnp.pad(jnp.concatenate([w_router_group, w_router_expert], axis=-1).astype(_f32),
                 ((0, 0), (0, 0), (0, pad)))
    rw_hi = rw.astype(_bf16)
    rw_lo = (rw - rw_hi.astype(_f32)).astype(_bf16)
    rb = jnp.pad(jnp.concatenate([b_router_group, b_router_expert], axis=-1).astype(_f32),
                 ((0, 0), (0, pad)))
    return {
        "n1w": norm1_w.astype(_f32)[:, None, :],
        "w_in": (w_in.astype(_f32) * col_scale).astype(_bf16),
        "lb": jnp.stack([jnp.stack([_lower_bound(lb_fwd_logits, l), _lower_bound(lb_bwd_logits, l)])
                         for l in range(depth)]),
        "qkw": jnp.stack([jnp.tile(q_norm_w.astype(_f32), (1, NA_HEADS)) * (NA_DH ** -0.5 * LOG2_E),
                          jnp.tile(k_norm_w.astype(_f32), (1, NA_HEADS))], axis=1),
        "dmat": jnp.asarray(dmat, _bf16),
        "mask": jnp.asarray(mask, _f32),
        "na_bias": jnp.stack([_natten_bias(na_rpb[l]) for l in range(depth)]),
        "na_ow": na_out_norm_w.astype(_f32)[:, None, :],
        "w_out": w_out.astype(_bf16),
        "hw": hgrn_norm_w.astype(_f32)[:, None, :],
        "n2w": norm2_w.astype(_f32)[:, None, :],
        "rw": jnp.stack([rw_hi, rw_lo], axis=1),
        "rb": rb[:, None, :],
        "tril": jnp.asarray(np.tril(np.ones((ROW_TILE, ROW_TILE), np.float32), -1), _bf16),
        "wg": w_expert_gate.astype(_bf16),
        "wu": w_expert_up.astype(_bf16),
        "wd": w_expert_down.astype(_bf16),
    }


def kernel(x_prompt, x_sample, norm1_w, w_in, lb_fwd_logits, lb_bwd_logits, hgrn_norm_w, q_norm_w, k_norm_w, na_rpb, na_out_norm_w, w_out, norm2_w, w_router_group, b_router_group, w_router_expert, b_router_expert, w_expert_gate, w_expert_up, w_expert_down):
    c = _prepare(norm1_w, w_in, lb_fwd_logits, lb_bwd_logits, hgrn_norm_w, q_norm_w, k_norm_w, na_rpb,
                 na_out_norm_w, w_out, norm2_w, w_router_group, b_router_group, w_router_expert,
                 b_router_expert, w_expert_gate, w_expert_up, w_expert_down)
    outs = []
    for x in (x_prompt, x_sample):
        for l in range(w_in.shape[0]):
            x = _layer(x, l, c)
        outs.append(x)
    return tuple(outs)
```

```python
import functools

import numpy as np
import jax
import jax.numpy as jnp
from jax import lax
from jax.experimental import pallas as pl
from jax.experimental.pallas import tpu as pltpu

D_MODEL = 1024
EPS = 1e-6
GRID_W = 64
HG_HEADS = 4
HG_D = 128
HG_WIDTH = HG_HEADS * HG_D
HG_CHUNK = 64
HG_LEVELS = 6
HG_MATMUL_LEVELS = 3
HG_CHUNK_UNROLL = 8
OUT_PROJ_PARTS = 2
NA_HEADS = 8
NA_DH = 64
NA_WIDTH = NA_HEADS * NA_DH
NA_KH = 8
NA_KW = 16
NA_ROWS_PER_STEP = 8
NA_ROWS_PER_ITER = 8
N_GROUPS = 4
EXPERTS_PER_GROUP = 8
N_EXPERTS = N_GROUPS * EXPERTS_PER_GROUP
EXPERT_FF = 512
ROUTER_LANES = 128
EXPERT_LANE0 = N_GROUPS
NEG_BIG = -1e30
LOG2_E = 1.4426950408889634
LANE = 128
TOKEN_SUBLANES = D_MODEL // LANE
VMEM_LIMIT = 56 * 1024 * 1024

COL_Q, COL_V, COL_G, COL_AQ, COL_AK, COL_AV = range(6)

_f32 = jnp.float32
_bf16 = jnp.bfloat16


def _dot(a, b):
    return jnp.dot(a, b, preferred_element_type=_f32)


def _dot_nt(a, b):
    return lax.dot_general(a, b, (((1,), (1,)), ((), ())), preferred_element_type=_f32)


def _dot_tn(a, b):
    return lax.dot_general(a, b, (((0,), (0,)), ((), ())), preferred_element_type=_f32)


def _split2(x):
    hi = x.astype(_bf16)
    lo = (x - hi.astype(_f32)).astype(_bf16)
    return hi, lo


def _head_pair_rsqrt(a):
    first = lax.broadcasted_iota(jnp.int32, a.shape, 1) < NA_DH
    sq = a * a
    s_first = jnp.sum(jnp.where(first, sq, 0.0), axis=-1, keepdims=True)
    s_second = jnp.sum(jnp.where(first, 0.0, sq), axis=-1, keepdims=True)
    return lax.rsqrt(jnp.where(first, s_first, s_second) * (1.0 / NA_DH) + EPS)


def _params(*sem):
    return pltpu.CompilerParams(dimension_semantics=sem, vmem_limit_bytes=VMEM_LIMIT)


def _const_spec(shape):
    nd = len(shape)
    return pl.BlockSpec(shape, lambda *_: (0,) * nd)


def _in_proj_kernel(x_ref, n1_ref, w_ref, lb_ref, qkw_ref, proj_ref, gate_ref):
    x = x_ref[...]
    ms = jnp.mean(x * x, axis=-1, keepdims=True)
    h = (x * lax.rsqrt(ms + EPS) * n1_ref[...]).astype(_bf16)
    W = HG_WIDTH

    def col(j):
        return _dot(h, w_ref[:, j * W:(j + 1) * W])

    proj_ref[:, COL_Q * W:(COL_Q + 1) * W] = col(0).astype(_bf16)
    for d in range(2):
        z = col(1 + d)
        lb = lb_ref[d:d + 1, :]
        f = lb + (1.0 - lb) * (1.0 / (1.0 + jnp.exp(-z)))
        gate_ref[:, d * W:(d + 1) * W] = jnp.log(f) * LOG2_E
    proj_ref[:, COL_V * W:(COL_V + 1) * W] = col(3).astype(_bf16)
    proj_ref[:, COL_G * W:(COL_G + 1) * W] = col(4).astype(_bf16)
    for d, c in ((0, COL_AQ), (1, COL_AK)):
        a = col(5 + d)
        for p in range(NA_HEADS // 2):
            ap = a[:, p * LANE:(p + 1) * LANE]
            lanes = slice(c * W + p * LANE, c * W + (p + 1) * LANE)
            proj_ref[:, lanes] = (ap * _head_pair_rsqrt(ap)
                                  * qkw_ref[d:d + 1, p * LANE:(p + 1) * LANE]).astype(_bf16)
    proj_ref[:, COL_AV * W:(COL_AV + 1) * W] = col(7).astype(_bf16)


def _in_proj(x, n1w, w_in_bf, lb, qkw, tm):
    S, T, D = x.shape
    return pl.pallas_call(
        _in_proj_kernel,
        grid=(S, T // tm),
        in_specs=[
            pl.BlockSpec((None, tm, D), lambda s, i: (s, i, 0)),
            _const_spec((1, D)),
            _const_spec(w_in_bf.shape),
            _const_spec(lb.shape),
            _const_spec(qkw.shape),
        ],
        out_specs=[
            pl.BlockSpec((None, tm, 6 * HG_WIDTH), lambda s, i: (s, i, 0)),
            pl.BlockSpec((None, tm, 2 * HG_WIDTH), lambda s, i: (s, i, 0)),
        ],
        out_shape=[
            jax.ShapeDtypeStruct((S, T, 6 * HG_WIDTH), _bf16),
            jax.ShapeDtypeStruct((S, T, 2 * HG_WIDTH), _f32),
        ],
        compiler_params=_params("parallel", "parallel"),
        name="in_proj",
    )(x, n1w, w_in_bf, lb, qkw)


def _hgrn_constants():
    C = HG_CHUNK
    idx = np.arange(C)
    dmats, masks = [], []
    for direction in (0, 1):
        if direction == 0:
            tri = (idx[None, :] <= idx[:, None]).astype(np.float64)
        else:
            tri = (idx[None, :] >= idx[:, None]).astype(np.float64)
        dm = [tri]
        mk = []
        for lvl in range(HG_LEVELS):
            h = 1 << lvl
            start = (idx // (2 * h)) * (2 * h)
            ref = start + h - 1 if direction == 0 else start + h
            upper = (idx & h) != 0
            later, earlier = (upper, ~upper) if direction == 0 else (~upper, upper)
            if lvl < HG_MATMUL_LEVELS:
                gather = np.zeros((C, C))
                gather[idx, ref] = 1.0
                dm.append(np.where(later[:, None], 1.0, -1.0) * (tri - gather @ tri))
            same = (idx[:, None] // (2 * h)) == (idx[None, :] // (2 * h))
            mk.append((same & later[:, None] & earlier[None, :]).astype(np.float64))
        dmats.append(np.concatenate(dm, axis=0))
        masks.append(np.stack(mk, axis=0))
    return np.stack(dmats, 0), np.stack(masks, 0)


def _hgrn_kernel(qf_ref, qb_ref, vf_ref, vb_ref, gf_ref, gb_ref, dmat_ref, mask_ref,
                 of_ref, ob_ref, state_ref, *, chunks):
    C = HG_CHUNK

    @pl.when(pl.program_id(1) == 0)
    def _():
        state_ref[...] = jnp.zeros_like(state_ref)

    def chunk_step(c, carry):
        deltas = []
        for direction, g_ref in enumerate((gf_ref, gb_ref)):
            cc = c if direction == 0 else chunks - 1 - c
            rows = pl.ds(pl.multiple_of(cc * C, C), C)
            g = g_ref[rows, :]
            g_hi, g_lo = _split2(g)
            dmat = dmat_ref[direction]
            deltas.append((rows, g, _dot(dmat, g_hi) + _dot(dmat, g_lo)))

        units = []
        for direction, (q_ref, v_ref, o_ref) in enumerate(((qf_ref, vf_ref, of_ref), (qb_ref, vb_ref, ob_ref))):
            rows, g, delta_all = deltas[direction]
            last = C - 1 if direction == 0 else 0
            for hd in range(HG_HEADS):
                cols = slice(hd * HG_D, (hd + 1) * HG_D)
                q = q_ref[rows, cols]
                v = v_ref[rows, cols]
                k = (1.0 - jnp.exp2(g[:, cols])).astype(_bf16)
                b = delta_all[0:C, cols]
                v32 = v.astype(_f32)
                diag = jnp.sum(q.astype(_f32) * k.astype(_f32), axis=-1, keepdims=True) * v32
                scores = None
                for lvl in range(HG_LEVELS):
                    if lvl < HG_MATMUL_LEVELS:
                        d = delta_all[(lvl + 1) * C:(lvl + 2) * C, cols]
                    else:
                        h = 1 << lvl
                        pieces = []
                        for s in range(0, C, 2 * h):
                            lo, hi = b[s:s + h, :], b[s + h:s + 2 * h, :]
                            if direction == 0:
                                ref = jnp.broadcast_to(b[s + h - 1:s + h, :], (h, HG_D))
                                pieces += [ref - lo, hi - ref]
                            else:
                                ref = jnp.broadcast_to(b[s + h:s + h + 1, :], (h, HG_D))
                                pieces += [lo - ref, ref - hi]
                        d = jnp.concatenate(pieces, axis=0)
                    e = jnp.exp2(d).astype(_bf16)
                    level = mask_ref[direction, lvl] * _dot_nt(q * e, k * e)
                    scores = level if scores is None else scores + level
                st = state_ref[direction, hd]
                b_last = b[last:last + 1, :]
                inter = diag + _dot_nt(q * jnp.exp2(b).astype(_bf16), st.astype(_bf16))
                k_dec = k * jnp.exp2(b_last - b).astype(_bf16)
                state_ref[direction, hd] = jnp.exp2(b_last) * st + _dot_tn(v, k_dec)
                units.append((o_ref, rows, cols, scores, v, inter))

        for o_ref, rows, cols, scores, v, inter in units:
            o_ref[rows, cols] = inter + _dot(scores.astype(_bf16), v)
        return carry

    lax.fori_loop(0, chunks, chunk_step, 0, unroll=HG_CHUNK_UNROLL)


def _hgrn2(proj, gates, dmat, mask, tb):
    S, T, _ = proj.shape
    n = T // tb
    W = HG_WIDTH
    fwd = lambda col: (lambda s, i: (s, i, col))
    bwd = lambda col: (lambda s, i: (s, n - 1 - i, col))
    blk = lambda im: pl.BlockSpec((None, tb, W), im)
    return pl.pallas_call(
        functools.partial(_hgrn_kernel, chunks=tb // HG_CHUNK),
        grid=(S, n),
        in_specs=[
            blk(fwd(COL_Q)), blk(bwd(COL_Q)), blk(fwd(COL_V)), blk(bwd(COL_V)),
            blk(fwd(0)), blk(bwd(1)),
            _const_spec(dmat.shape), _const_spec(mask.shape),
        ],
        out_specs=[blk(fwd(0)), blk(bwd(0))],
        out_shape=[jax.ShapeDtypeStruct((S, T, W), _f32)] * 2,
        scratch_shapes=[pltpu.VMEM((2, HG_HEADS, HG_D, HG_D), _f32)],
        compiler_params=_params("parallel", "arbitrary"),
        name="hgrn2",
    )(proj, proj, proj, proj, gates, gates, dmat, mask)


def _natten_bias_kernel(rpb_ref, o_ref):
    W = GRID_W
    qc = lax.broadcasted_iota(jnp.int32, (W, LANE), 0)
    lane = lax.broadcasted_iota(jnp.int32, (W, LANE), 1)
    kc = lane & (W - 1)
    cs = jnp.clip(qc - NA_KW // 2, 0, W - NA_KW)
    valid = (kc >= cs) & (kc < cs + NA_KW)
    low = lane < W
    t_lo, t_hi = [], []
    for dr in range(2 * NA_KH - 1):
        row = jnp.broadcast_to(rpb_ref[dr:dr + 1, :], (W, LANE))
        t = pltpu.roll(row, LANE - (NA_KW - 1), 1, stride=1, stride_axis=0)
        t_lo.append(t)
        t_hi.append(pltpu.roll(t, W, 1))
    for case in range(NA_KH):
        for j in range(NA_KH // 2):
            d0 = 2 * j - case + NA_KH - 1
            tile = jnp.where(low, t_lo[d0], t_hi[d0 + 1])
            o_ref[case, :, j * LANE:(j + 1) * LANE] = jnp.where(valid, tile * LOG2_E, NEG_BIG)


def _natten_bias(rpb):
    W = GRID_W
    rows, cols = 2 * NA_KH - 1, 2 * NA_KW - 1
    padded = jnp.pad(rpb.astype(_f32), ((0, 0), (0, 0), (0, LANE - cols)))
    return pl.pallas_call(
        _natten_bias_kernel,
        grid=(NA_HEADS,),
        in_specs=[pl.BlockSpec((None, rows, LANE), lambda h: (h, 0, 0))],
        out_specs=pl.BlockSpec((NA_KH, None, W, NA_KH * W), lambda h: (0, h // 2, h % 2, 0)),
        out_shape=jax.ShapeDtypeStruct((NA_KH, NA_HEADS // 2, 2 * W, NA_KH * W), _f32),
        compiler_params=_params("parallel"),
        name="natten_bias",
    )(padded)


def _natten_kernel(q_ref, kp_ref, kc_ref, kn_ref, vp_ref, vc_ref, vn_ref, bias_ref, ow_ref,
                   o_ref, kbuf, vbuf, *, grid_rows):
    R = NA_ROWS_PER_STEP
    W = GRID_W
    blk = R * W
    for j, (k_r, v_r) in enumerate(((kp_ref, vp_ref), (kc_ref, vc_ref), (kn_ref, vn_ref))):
        kbuf[j * blk:(j + 1) * blk, :] = k_r[...]
        vbuf[j * blk:(j + 1) * blk, :] = v_r[...]
    r0 = pl.program_id(1) * R
    lane = lax.broadcasted_iota(jnp.int32, (W, LANE), 1)
    first = lane < NA_DH

    def row_group(ig, carry):
        units = []
        for k in range(NA_ROWS_PER_ITER):
            i = ig * NA_ROWS_PER_ITER + k
            r = r0 + i
            rs = jnp.clip(r - NA_KH // 2, 0, grid_rows - NA_KH)
            off = pl.multiple_of((rs - (r0 - R)) * W, W)
            for p in range(NA_HEADS // 2):
                units.append((pl.ds(pl.multiple_of(i * W, W), W), slice(p * LANE, (p + 1) * LANE),
                              pl.ds(off, NA_KH * W), r - rs, p))
        scores = []
        for qrows, cols, win, case, p in units:
            qp = q_ref[qrows, cols]
            zero = jnp.zeros_like(qp)
            lhs = jnp.concatenate([jnp.where(first, qp, zero), jnp.where(first, zero, qp)], axis=0)
            scores.append(_dot_nt(lhs, kbuf[win, cols]))
        probs = []
        for (qrows, cols, win, case, p), s in zip(units, scores):
            s = s + bias_ref[case, p]
            e = jnp.exp2(s - jnp.max(s, axis=-1, keepdims=True))
            probs.append((e.astype(_bf16), jnp.sum(e, axis=-1, keepdims=True)))
        for (qrows, cols, win, case, p), (e, l) in zip(units, probs):
            pv = _dot(e, vbuf[win, cols]) / l
            out = jnp.where(first, pv[0:W], pv[W:2 * W])
            o_ref[qrows, cols] = (out * _head_pair_rsqrt(out) * ow_ref[:, cols]).astype(_bf16)
        return carry

    lax.fori_loop(0, R // NA_ROWS_PER_ITER, row_group, 0)


def _natten(proj, bias, out_w):
    S, T, _ = proj.shape
    rows = T // GRID_W
    assert rows >= NA_KH and rows % NA_ROWS_PER_STEP == 0
    nb = rows // NA_ROWS_PER_STEP
    blk_t = NA_ROWS_PER_STEP * GRID_W
    Wd = NA_WIDTH
    cur = lambda col: (lambda s, i: (s, i, col))
    prv = lambda col: (lambda s, i: (s, jnp.maximum(i - 1, 0), col))
    nxt = lambda col: (lambda s, i: (s, jnp.minimum(i + 1, nb - 1), col))
    blk = lambda im: pl.BlockSpec((None, blk_t, Wd), im)
    return pl.pallas_call(
        functools.partial(_natten_kernel, grid_rows=rows),
        grid=(S, nb),
        in_specs=[
            blk(cur(COL_AQ)),
            blk(prv(COL_AK)), blk(cur(COL_AK)), blk(nxt(COL_AK)),
            blk(prv(COL_AV)), blk(cur(COL_AV)), blk(nxt(COL_AV)),
            _const_spec(bias.shape), _const_spec(out_w.shape),
        ],
        out_specs=blk(cur(0)),
        out_shape=jax.ShapeDtypeStruct((S, T, Wd), _bf16),
        scratch_shapes=[pltpu.VMEM((3 * blk_t, Wd), _bf16), pltpu.VMEM((3 * blk_t, Wd), _bf16)],
        compiler_params=_params("parallel", "parallel"),
        name="natten",
    )(proj, proj, proj, proj, proj, proj, proj, bias, out_w)


def _out_proj_kernel(x_ref, of_ref, ob_ref, hg_ref, na_ref, w_ref, hw_ref, n2_ref, rw_ref, rb_ref, tril_ref,
                     x1_ref, h2_ref, route_ref, w1_ref, w2_ref, cnt_ref, carry_ref, *, steps_per_tile):
    @pl.when(pl.program_id(1) % steps_per_tile == 0)
    def _():
        carry_ref[...] = jnp.zeros_like(carry_ref)

    tm = x_ref.shape[0]
    pr = tm // OUT_PROJ_PARTS
    parts_rows = [slice(j * pr, (j + 1) * pr) for j in range(OUT_PROJ_PARTS)]

    mixes = []
    for rows in parts_rows:
        o = of_ref[rows, :] + ob_ref[rows, :]
        heads = []
        for hd in range(HG_HEADS):
            oh = o[:, hd * HG_D:(hd + 1) * HG_D]
            ms = jnp.mean(oh * oh, axis=-1, keepdims=True)
            heads.append(oh * lax.rsqrt(ms + EPS))
        on = jnp.concatenate(heads, axis=-1) * hw_ref[...]
        sig = 1.0 / (1.0 + jnp.exp2(hg_ref[rows, :].astype(_f32)))
        mixes.append((on * sig).astype(_bf16))
    accs = [_dot(mix_a, w_ref[0:HG_WIDTH, :]) + _dot(na_ref[rows, :], w_ref[HG_WIDTH:, :])
            for rows, mix_a in zip(parts_rows, mixes)]
    splits = []
    for j, (rows, acc) in enumerate(zip(parts_rows, accs)):
        x1 = x_ref[rows, :] + acc
        x1_ref[rows, :] = x1
        ms2 = jnp.mean(x1 * x1, axis=-1, keepdims=True)
        h2 = x1 * lax.rsqrt(ms2 + EPS) * n2_ref[...]
        for c in range(TOKEN_SUBLANES):
            h2_ref[pl.ds(j * pr * TOKEN_SUBLANES + c, pr, stride=TOKEN_SUBLANES), :] = h2[:, c * LANE:(c + 1) * LANE]
        splits.append(_split2(h2))

    w_hi = rw_ref[0]
    w_lo = rw_ref[1]
    lg = jnp.concatenate([_dot(h_hi, w_hi) + _dot(h_lo, w_hi) + _dot(h_hi, w_lo) for h_hi, h_lo in splits],
                         axis=0) + rb_ref[...]
    lane_i = lax.broadcasted_iota(jnp.int32, lg.shape, 1)
    lane = lane_i.astype(_f32)
    lane_grp = ((lane_i - EXPERT_LANE0) >> 3).astype(_f32)

    def first_argmax(vals, valid):
        masked = jnp.where(valid, vals, NEG_BIG)
        top = jnp.max(masked, axis=-1, keepdims=True)
        idx = jnp.min(jnp.where(valid & (masked == top), lane, float(ROUTER_LANES)), axis=-1, keepdims=True)
        return top, idx

    is_grp = lane_i < N_GROUPS
    g_top, g_idx = first_argmax(lg, is_grp)
    p_top = 1.0 / jnp.sum(jnp.where(is_grp, jnp.exp(lg - g_top), 0.0), axis=-1, keepdims=True)
    in_grp = (lane_i >= EXPERT_LANE0) & (lane_i < EXPERT_LANE0 + N_EXPERTS) & (lane_grp == g_idx)
    t1, i1 = first_argmax(lg, in_grp)
    t2, i2 = first_argmax(lg, in_grp & (lane != i1))
    e2 = jnp.exp(t2 - t1)
    w1 = p_top / (1.0 + e2)
    w2 = p_top * e2 / (1.0 + e2)
    w1_ref[...] = jnp.broadcast_to(w1, w1_ref.shape)
    w2_ref[...] = jnp.broadcast_to(w2, w2_ref.shape)

    sel1 = lane == i1
    sel2 = lane == i2
    onehot = jnp.where(sel1 | sel2, 1.0, 0.0)
    before = _dot(tril_ref[...], onehot.astype(_bf16)) + carry_ref[...]
    rank1 = jnp.sum(jnp.where(sel1, before, 0.0), axis=-1, keepdims=True)
    rank2 = jnp.sum(jnp.where(sel2, before, 0.0), axis=-1, keepdims=True)
    carry = carry_ref[...] + jnp.sum(onehot, axis=0, keepdims=True)
    carry_ref[...] = carry
    cnt_ref[...] = jnp.broadcast_to(carry, cnt_ref.shape)
    route_ref[...] = jnp.where(lane_i == 0, i1 - EXPERT_LANE0,
                     jnp.where(lane_i == 1, i2 - EXPERT_LANE0,
                     jnp.where(lane_i == 2, rank1, jnp.where(lane_i == 3, rank2, 0.0))))


def _out_proj(x, o_f, o_b, proj, na, w_out_bf, hw, n2w, rw, rb, tril, tm, moe_tm):
    S, T, D = x.shape
    W = HG_WIDTH
    n = T // tm
    row = lambda width, col=0: pl.BlockSpec((None, tm, width), lambda s, i: (s, i, col))
    lanes = jax.ShapeDtypeStruct((S, T, ROUTER_LANES), _f32)
    return pl.pallas_call(
        functools.partial(_out_proj_kernel, steps_per_tile=moe_tm // tm),
        grid=(S, n),
        in_specs=[
            row(D), row(W), row(W), row(W, COL_G), row(W),
            _const_spec(w_out_bf.shape), _const_spec(hw.shape), _const_spec(n2w.shape),
            _const_spec(rw.shape), _const_spec(rb.shape), _const_spec(tril.shape),
        ],
        out_specs=[row(D), pl.BlockSpec((None, tm * TOKEN_SUBLANES, LANE), lambda s, i: (s, i, 0)),
                   row(ROUTER_LANES), row(ROUTER_LANES), row(ROUTER_LANES),
                   pl.BlockSpec((None, None, 8, ROUTER_LANES), lambda s, i: (s, i, 0, 0))],
        out_shape=[
            jax.ShapeDtypeStruct((S, T, D), _f32),
            jax.ShapeDtypeStruct((S, T * TOKEN_SUBLANES, LANE), _f32),
            lanes, lanes, lanes,
            jax.ShapeDtypeStruct((S, n, 8, ROUTER_LANES), _f32),
        ],
        scratch_shapes=[pltpu.VMEM((1, ROUTER_LANES), _f32)],
        compiler_params=_params("parallel", "arbitrary"),
        name="out_proj",
    )(x, o_f, o_b, proj, na, w_out_bf, hw, n2w, rw, rb, tril)


MOE_WINDOWS = (128, 288)
MOE_UNROLL = 16
MOE_STREAM_BLOCK = 256
MOE_SLOTS = 3
MOE_AHEAD = MOE_SLOTS - 1


def _moe_rows(tm):
    return 2 * tm + MOE_WINDOWS[-1]


def _moe_kernel(pos1_ref, pos2_ref, seg_ref, x1_hbm, h2_hbm, w1_ref, w2_ref, wg_ref, wu_ref, wd_ref, y_hbm,
                buf_ref, mix_ref, h2_stage, x1_stage, y_stage, h2_sem, x1_sem, y_sem, *, tm, sb):
    i = pl.program_id(0)
    e = pl.program_id(1)
    TS = TOKEN_SUBLANES
    nsb = tm // sb

    def token_tile(row):
        return pl.ds(pl.multiple_of(row * TS, TS), TS)

    def buffer_tiles(t):
        return tuple(pl.ds(pl.multiple_of(ref[i * tm + t], TS), TS) for ref in (pos1_ref, pos2_ref))

    def token_rows(blk):
        return pl.ds(pl.multiple_of(i * tm + blk * sb, sb), sb)

    def h2_copy(blk, slot, tile=i):
        src = h2_hbm.at[pl.ds(pl.multiple_of((tile * tm + blk * sb) * TS, sb * TS), sb * TS), :]
        return pltpu.make_async_copy(src, h2_stage.at[slot], h2_sem.at[slot])

    def x1_copy(blk, slot):
        return pltpu.make_async_copy(x1_hbm.at[token_rows(blk), :], x1_stage.at[slot], x1_sem.at[slot])

    def y_copy(blk, slot):
        return pltpu.make_async_copy(y_stage.at[slot], y_hbm.at[token_rows(blk), :], y_sem.at[slot])

    @pl.when(e == 0)
    def _():
        slack = pl.ds(2 * tm * TS, MOE_WINDOWS[-1] * TS)
        buf_ref[slack, :] = jnp.zeros((MOE_WINDOWS[-1] * TS, LANE), _f32)

        @pl.when(i == 0)
        def _():
            for blk in range(MOE_AHEAD):
                h2_copy(blk, blk).start()

        def dispatch(blk, carry):
            slot = blk % MOE_SLOTS
            h2_copy(blk, slot).wait()

            @pl.when(blk + MOE_AHEAD < nsb)
            def _():
                h2_copy(blk + MOE_AHEAD, (blk + MOE_AHEAD) % MOE_SLOTS).start()

            def tokens(grp, c2):
                for k in range(MOE_UNROLL):
                    tl = grp * MOE_UNROLL + k
                    p1, p2 = buffer_tiles(blk * sb + tl)
                    tok = h2_stage[slot, token_tile(tl), :]
                    buf_ref[p1, :] = tok
                    buf_ref[p2, :] = tok
                return c2

            lax.fori_loop(0, sb // MOE_UNROLL, tokens, 0)
            return carry

        lax.fori_loop(0, nsb, dispatch, 0)

    start = seg_ref[(i * N_EXPERTS + e) * 2]
    count = seg_ref[(i * N_EXPERTS + e) * 2 + 1]

    def window(rows_w, done):
        r0 = start + done
        chunk = lambda c: pl.ds(r0 * TS + c, rows_w, stride=TS)
        xs = [buf_ref[chunk(c), :] for c in range(TS)]
        xb = jnp.concatenate(xs, axis=-1).astype(_bf16)
        a = _dot(xb, wg_ref[...])
        u = _dot(xb, wu_ref[...])
        hmid = a * (1.0 / (1.0 + jnp.exp(-a))) * u
        out = _dot(hmid.astype(_bf16), wd_ref[...])
        valid = lax.broadcasted_iota(jnp.int32, (rows_w, LANE), 0) + done < count
        for c in range(TS):
            buf_ref[chunk(c), :] = jnp.where(valid, out[:, c * LANE:(c + 1) * LANE], xs[c])

    @pl.when(e == N_EXPERTS - 1)
    def _():
        for blk in range(MOE_AHEAD):
            x1_copy(blk, blk).start()

        @pl.when(i + 1 < pl.num_programs(0))
        def _():
            for blk in range(MOE_AHEAD):
                h2_copy(blk, blk, tile=i + 1).start()

    small, large = MOE_WINDOWS

    n_large = count // large
    tail = count - n_large * large

    def body(j, carry):
        window(large, j * large)
        return carry

    lax.fori_loop(0, n_large, body, 0)

    @pl.when(tail > small)
    def _():
        window(large, n_large * large)

    @pl.when((tail > 0) & (tail <= small))
    def _():
        window(small, n_large * large)

    @pl.when(e == N_EXPERTS - 1)
    def _():
        def combine(blk, carry):
            slot = blk % MOE_SLOTS

            @pl.when(blk + MOE_AHEAD < nsb)
            def _():
                x1_copy(blk + MOE_AHEAD, (blk + MOE_AHEAD) % MOE_SLOTS).start()

            def tokens(grp, c2):
                for k in range(MOE_UNROLL):
                    tl = grp * MOE_UNROLL + k
                    t = blk * sb + tl
                    p1, p2 = buffer_tiles(t)
                    g1 = jnp.broadcast_to(w1_ref[pl.ds(t, 1), :], (TS, LANE))
                    g2 = jnp.broadcast_to(w2_ref[pl.ds(t, 1), :], (TS, LANE))
                    mix_ref[token_tile(tl), :] = g1 * buf_ref[p1, :] + g2 * buf_ref[p2, :]
                return c2

            lax.fori_loop(0, sb // MOE_UNROLL, tokens, 0)
            x1_copy(blk, slot).wait()

            @pl.when((blk >= MOE_SLOTS) | (i > 0))
            def _():
                y_copy(blk, slot).wait()

            for c in range(TS):
                cols = slice(c * LANE, (c + 1) * LANE)
                y_stage[slot, :, cols] = x1_stage[slot, :, cols] + mix_ref[pl.ds(c, sb, stride=TS), :]
            y_copy(blk, slot).start()
            return carry

        lax.fori_loop(0, nsb, combine, 0)

        @pl.when(i + 1 == pl.num_programs(0))
        def _():
            for blk in range(nsb - MOE_SLOTS, nsb):
                y_copy(blk, blk % MOE_SLOTS).wait()


def _moe(pos1, pos2, seg, x1, h2, w1, w2, wg, wu, wd, tm):
    N, D = x1.shape
    sb = min(MOE_STREAM_BLOCK, tm // 4)
    assert tm % sb == 0 and tm // sb >= MOE_SLOTS and sb % MOE_UNROLL == 0
    tile = lambda width: pl.BlockSpec((tm, width), lambda i, e, *_: (i, 0), pipeline_mode=pl.Buffered(1))
    wspec = lambda a, b: pl.BlockSpec((None, a, b), lambda i, e, *_: (e, 0, 0))
    hbm = pl.BlockSpec(memory_space=pl.ANY)
    return pl.pallas_call(
        functools.partial(_moe_kernel, tm=tm, sb=sb),
        grid_spec=pltpu.PrefetchScalarGridSpec(
            num_scalar_prefetch=3,
            grid=(N // tm, N_EXPERTS),
            in_specs=[
                hbm, hbm, tile(ROUTER_LANES), tile(ROUTER_LANES),
                wspec(D, EXPERT_FF), wspec(D, EXPERT_FF), wspec(EXPERT_FF, D),
            ],
            out_specs=hbm,
            scratch_shapes=[
                pltpu.VMEM((_moe_rows(tm) * TOKEN_SUBLANES, LANE), _f32),
                pltpu.VMEM((sb * TOKEN_SUBLANES, LANE), _f32),
                pltpu.VMEM((MOE_SLOTS, sb * TOKEN_SUBLANES, LANE), _f32),
                pltpu.VMEM((MOE_SLOTS, sb, D), _f32),
                pltpu.VMEM((MOE_SLOTS, sb, D), _f32),
                pltpu.SemaphoreType.DMA((MOE_SLOTS,)),
                pltpu.SemaphoreType.DMA((MOE_SLOTS,)),
                pltpu.SemaphoreType.DMA((MOE_SLOTS,)),
            ],
        ),
        out_shape=jax.ShapeDtypeStruct((N, D), _f32),
        compiler_params=_params("arbitrary", "arbitrary"),
        name="moe",
    )(pos1, pos2, seg, x1, h2, w1, w2, wg, wu, wd)


def _moe_plan(route, cnt, tm, steps_per_tile):
    N = route.shape[0]
    n_tiles = N // tm
    counts = cnt.reshape(-1, 8, ROUTER_LANES)[steps_per_tile - 1::steps_per_tile, 0,
                                               EXPERT_LANE0:EXPERT_LANE0 + N_EXPERTS].astype(jnp.int32)
    seg_start = jnp.cumsum(counts, axis=1) - counts
    r = route[:, :4].astype(jnp.int32).reshape(n_tiles, tm, 4)
    experts = jnp.arange(N_EXPERTS, dtype=jnp.int32)

    def row_of(expert, rank):
        onehot = expert[..., None] == experts
        return jnp.sum(jnp.where(onehot, seg_start[:, None, :], 0), axis=-1) + rank

    pos1 = row_of(r[..., 0], r[..., 2]) * TOKEN_SUBLANES
    pos2 = row_of(r[..., 1], r[..., 3]) * TOKEN_SUBLANES
    seg = jnp.stack([seg_start, counts], axis=-1)
    return pos1.reshape(-1), pos2.reshape(-1), seg.reshape(-1)


def _lower_bound(logits, l):
    return jnp.cumsum(jax.nn.softmax(logits.astype(_f32), axis=0), axis=0)[l]


ROW_TILE = 512
HGRN_BLOCK = 512
MOE_TILE = 4096


def _layer(x, l, c):
    S, T, D = x.shape
    tm = min(ROW_TILE, T)
    moe_tm = min(MOE_TILE, T)
    proj, gates = _in_proj(x, c["n1w"][l], c["w_in"][l], c["lb"][l], c["qkw"][l], tm)
    o_f, o_b = _hgrn2(proj, gates, c["dmat"], c["mask"], min(HGRN_BLOCK, T))
    na = _natten(proj, c["na_bias"][l], c["na_ow"][l])
    x1, h2, route, w1, w2, cnt = _out_proj(x, o_f, o_b, proj, na, c["w_out"][l], c["hw"][l], c["n2w"][l],
                                           c["rw"][l], c["rb"][l], c["tril"][:tm, :tm], tm, moe_tm)
    N = S * T
    flat = lambda a: a.reshape(N, a.shape[-1])
    pos1, pos2, seg = _moe_plan(flat(route), cnt, moe_tm, moe_tm // tm)
    y = _moe(pos1, pos2, seg, flat(x1), h2.reshape(N * TOKEN_SUBLANES, LANE), flat(w1), flat(w2),
             c["wg"][l], c["wu"][l], c["wd"][l], moe_tm)
    return y.reshape(S, T, D)


def _prepare(norm1_w, w_in, lb_fwd_logits, lb_bwd_logits, hgrn_norm_w, q_norm_w, k_norm_w, na_rpb,
             na_out_norm_w, w_out, norm2_w, w_router_group, b_router_group, w_router_expert,
             b_router_expert, w_expert_gate, w_expert_up, w_expert_down):
    depth = w_in.shape[0]
    dmat, mask = _hgrn_constants()
    col_scale = np.ones((1, 1, w_in.shape[-1]), np.float32)
    col_scale[..., 4 * HG_WIDTH:5 * HG_WIDTH] = -LOG2_E
    pad = ROUTER_LANES - N_GROUPS - N_EXPERTS
    rw = jnp.pad(jnp.concatenate([w_router_group, w_router_expert], axis=-1).astype(_f32),
                 ((0, 0), (0, 0), (0, pad)))
    rw_hi = rw.astype(_bf16)
    rw_lo = (rw - rw_hi.astype(_f32)).astype(_bf16)
    rb = jnp.pad(jnp.concatenate([b_router_group, b_router_expert], axis=-1).astype(_f32),
                 ((0, 0), (0, pad)))
    return {
        "n1w": norm1_w.astype(_f32)[:, None, :],
        "w_in": (w_in.astype(_f32) * col_scale).astype(_bf16),
        "lb": jnp.stack([jnp.stack([_lower_bound(lb_fwd_logits, l), _lower_bound(lb_bwd_logits, l)])
                         for l in range(depth)]),
        "qkw": jnp.stack([jnp.tile(q_norm_w.astype(_f32), (1, NA_HEADS)) * (NA_DH ** -0.5 * LOG2_E),
                          jnp.tile(k_norm_w.astype(_f32), (1, NA_HEADS))], axis=1),
        "dmat": jnp.asarray(dmat, _bf16),
        "mask": jnp.asarray(mask, _f32),
        "na_bias": jnp.stack([_natten_bias(na_rpb[l]) for l in range(depth)]),
        "na_ow": na_out_norm_w.astype(_f32)[:, None, :],
        "w_out": w_out.astype(_bf16),
        "hw": hgrn_norm_w.astype(_f32)[:, None, :],
        "n2w": norm2_w.astype(_f32)[:, None, :],
        "rw": jnp.stack([rw_hi, rw_lo], axis=1),
        "rb": rb[:, None, :],
        "tril": jnp.asarray(np.tril(np.ones((ROW_TILE, ROW_TILE), np.float32), -1), _bf16),
        "wg": w_expert_gate.astype(_bf16),
        "wu": w_expert_up.astype(_bf16),
        "wd": w_expert_down.astype(_bf16),
    }


def kernel(x_prompt, x_sample, norm1_w, w_in, lb_fwd_logits, lb_bwd_logits, hgrn_norm_w, q_norm_w, k_norm_w, na_rpb, na_out_norm_w, w_out, norm2_w, w_router_group, b_router_group, w_router_expert, b_router_expert, w_expert_gate, w_expert_up, w_expert_down):
    c = _prepare(norm1_w, w_in, lb_fwd_logits, lb_bwd_logits, hgrn_norm_w, q_norm_w, k_norm_w, na_rpb,
                 na_out_norm_w, w_out, norm2_w, w_router_group, b_router_group, w_router_expert,
                 b_router_expert, w_expert_gate, w_expert_up, w_expert_down)
    outs = []
    for x in (x_prompt, x_sample):
        for l in range(w_in.shape[0]):
            x = _layer(x, l, c)
        outs.append(x)
    return tuple(outs)
```

```python
import functools

import numpy as np
import jax
import jax.numpy as jnp
from jax import lax
from jax.experimental import pallas as pl
from jax.experimental.pallas import tpu as pltpu

D_MODEL = 1024
EPS = 1e-6
GRID_W = 64
HG_HEADS = 4
HG_D = 128
HG_WIDTH = HG_HEADS * HG_D
HG_CHUNK = 64
HG_LEVELS = 6
HG_MATMUL_LEVELS = 3
HG_CHUNK_UNROLL = 8
ROUTE_ROWS = 8
OUT_PROJ_PARTS = 2
NA_HEADS = 8
NA_DH = 64
NA_WIDTH = NA_HEADS * NA_DH
NA_KH = 8
NA_KW = 16
NA_ROWS_PER_STEP = 8
NA_ROWS_PER_ITER = 8
N_GROUPS = 4
EXPERTS_PER_GROUP = 8
N_EXPERTS = N_GROUPS * EXPERTS_PER_GROUP
EXPERT_FF = 512
ROUTER_LANES = 128
EXPERT_LANE0 = N_GROUPS
NEG_BIG = -1e30
LOG2_E = 1.4426950408889634
LANE = 128
TOKEN_SUBLANES = D_MODEL // LANE
VMEM_LIMIT = 56 * 1024 * 1024

COL_Q, COL_V, COL_G, COL_AQ, COL_AK, COL_AV = range(6)

_f32 = jnp.float32
_bf16 = jnp.bfloat16


def _dot(a, b):
    return jnp.dot(a, b, preferred_element_type=_f32)


def _dot_nt(a, b):
    return lax.dot_general(a, b, (((1,), (1,)), ((), ())), preferred_element_type=_f32)


def _dot_tn(a, b):
    return lax.dot_general(a, b, (((0,), (0,)), ((), ())), preferred_element_type=_f32)


def _split2(x):
    hi = x.astype(_bf16)
    lo = (x - hi.astype(_f32)).astype(_bf16)
    return hi, lo


def _head_pair_rsqrt(a):
    first = lax.broadcasted_iota(jnp.int32, a.shape, 1) < NA_DH
    sq = a * a
    s_first = jnp.sum(jnp.where(first, sq, 0.0), axis=-1, keepdims=True)
    s_second = jnp.sum(jnp.where(first, 0.0, sq), axis=-1, keepdims=True)
    return lax.rsqrt(jnp.where(first, s_first, s_second) * (1.0 / NA_DH) + EPS)


def _params(*sem):
    return pltpu.CompilerParams(dimension_semantics=sem, vmem_limit_bytes=VMEM_LIMIT)


def _const_spec(shape):
    nd = len(shape)
    return pl.BlockSpec(shape, lambda *_: (0,) * nd)


def _in_proj_kernel(x_ref, n1_ref, w_ref, lb_ref, qkw_ref, proj_ref, gate_ref):
    x = x_ref[...]
    ms = jnp.mean(x * x, axis=-1, keepdims=True)
    h = (x * lax.rsqrt(ms + EPS) * n1_ref[...]).astype(_bf16)
    W = HG_WIDTH

    def col(j):
        return _dot(h, w_ref[:, j * W:(j + 1) * W])

    proj_ref[:, COL_Q * W:(COL_Q + 1) * W] = col(0).astype(_bf16)
    for d in range(2):
        z = col(1 + d)
        lb = lb_ref[d:d + 1, :]
        f = lb + (1.0 - lb) * (1.0 / (1.0 + jnp.exp(-z)))
        gate_ref[:, d * W:(d + 1) * W] = jnp.log(f) * LOG2_E
    proj_ref[:, COL_V * W:(COL_V + 1) * W] = col(3).astype(_bf16)
    proj_ref[:, COL_G * W:(COL_G + 1) * W] = col(4).astype(_bf16)
    for d, c in ((0, COL_AQ), (1, COL_AK)):
        a = col(5 + d)
        for p in range(NA_HEADS // 2):
            ap = a[:, p * LANE:(p + 1) * LANE]
            lanes = slice(c * W + p * LANE, c * W + (p + 1) * LANE)
            proj_ref[:, lanes] = (ap * _head_pair_rsqrt(ap)
                                  * qkw_ref[d:d + 1, p * LANE:(p + 1) * LANE]).astype(_bf16)
    proj_ref[:, COL_AV * W:(COL_AV + 1) * W] = col(7).astype(_bf16)


def _in_proj(x, n1w, w_in_bf, lb, qkw, tm):
    S, T, D = x.shape
    return pl.pallas_call(
        _in_proj_kernel,
        grid=(S, T // tm),
        in_specs=[
            pl.BlockSpec((None, tm, D), lambda s, i: (s, i, 0)),
            _const_spec((1, D)),
            _const_spec(w_in_bf.shape),
            _const_spec(lb.shape),
            _const_spec(qkw.shape),
        ],
        out_specs=[
            pl.BlockSpec((None, tm, 6 * HG_WIDTH), lambda s, i: (s, i, 0)),
            pl.BlockSpec((None, tm, 2 * HG_WIDTH), lambda s, i: (s, i, 0)),
        ],
        out_shape=[
            jax.ShapeDtypeStruct((S, T, 6 * HG_WIDTH), _bf16),
            jax.ShapeDtypeStruct((S, T, 2 * HG_WIDTH), _f32),
        ],
        compiler_params=_params("parallel", "parallel"),
        name="in_proj",
    )(x, n1w, w_in_bf, lb, qkw)


def _hgrn_constants():
    C = HG_CHUNK
    idx = np.arange(C)
    dmats, masks = [], []
    for direction in (0, 1):
        if direction == 0:
            tri = (idx[None, :] <= idx[:, None]).astype(np.float64)
        else:
            tri = (idx[None, :] >= idx[:, None]).astype(np.float64)
        dm = [tri]
        mk = []
        for lvl in range(HG_LEVELS):
            h = 1 << lvl
            start = (idx // (2 * h)) * (2 * h)
            ref = start + h - 1 if direction == 0 else start + h
            upper = (idx & h) != 0
            later, earlier = (upper, ~upper) if direction == 0 else (~upper, upper)
            if lvl < HG_MATMUL_LEVELS:
                gather = np.zeros((C, C))
                gather[idx, ref] = 1.0
                dm.append(np.where(later[:, None], 1.0, -1.0) * (tri - gather @ tri))
            same = (idx[:, None] // (2 * h)) == (idx[None, :] // (2 * h))
            mk.append((same & later[:, None] & earlier[None, :]).astype(np.float64))
        dmats.append(np.concatenate(dm, axis=0))
        masks.append(np.stack(mk, axis=0))
    return np.stack(dmats, 0), np.stack(masks, 0)


def _hgrn_kernel(qf_ref, qb_ref, vf_ref, vb_ref, gf_ref, gb_ref, dmat_ref, mask_ref,
                 of_ref, ob_ref, state_ref, *, chunks):
    C = HG_CHUNK

    @pl.when(pl.program_id(1) == 0)
    def _():
        state_ref[...] = jnp.zeros_like(state_ref)

    def chunk_step(c, carry):
        deltas = []
        for direction, g_ref in enumerate((gf_ref, gb_ref)):
            cc = c if direction == 0 else chunks - 1 - c
            rows = pl.ds(pl.multiple_of(cc * C, C), C)
            g = g_ref[rows, :]
            g_hi, g_lo = _split2(g)
            dmat = dmat_ref[direction]
            deltas.append((rows, g, _dot(dmat, g_hi) + _dot(dmat, g_lo)))

        units = []
        for direction, (q_ref, v_ref, o_ref) in enumerate(((qf_ref, vf_ref, of_ref), (qb_ref, vb_ref, ob_ref))):
            rows, g, delta_all = deltas[direction]
            last = C - 1 if direction == 0 else 0
            for hd in range(HG_HEADS):
                cols = slice(hd * HG_D, (hd + 1) * HG_D)
                q = q_ref[rows, cols]
                v = v_ref[rows, cols]
                k = (1.0 - jnp.exp2(g[:, cols])).astype(_bf16)
                b = delta_all[0:C, cols]
                v32 = v.astype(_f32)
                diag = jnp.sum(q.astype(_f32) * k.astype(_f32), axis=-1, keepdims=True) * v32
                scores = None
                for lvl in range(HG_LEVELS):
                    if lvl < HG_MATMUL_LEVELS:
                        d = delta_all[(lvl + 1) * C:(lvl + 2) * C, cols]
                    else:
                        h = 1 << lvl
                        pieces = []
                        for s in range(0, C, 2 * h):
                            lo, hi = b[s:s + h, :], b[s + h:s + 2 * h, :]
                            if direction == 0:
                                ref = jnp.broadcast_to(b[s + h - 1:s + h, :], (h, HG_D))
                                pieces += [ref - lo, hi - ref]
                            else:
                                ref = jnp.broadcast_to(b[s + h:s + h + 1, :], (h, HG_D))
                                pieces += [lo - ref, ref - hi]
                        d = jnp.concatenate(pieces, axis=0)
                    e = jnp.exp2(d).astype(_bf16)
                    level = mask_ref[direction, lvl] * _dot_nt(q * e, k * e)
                    scores = level if scores is None else scores + level
                st = state_ref[direction, hd]
                b_last = b[last:last + 1, :]
                inter = diag + _dot_nt(q * jnp.exp2(b).astype(_bf16), st.astype(_bf16))
                k_dec = k * jnp.exp2(b_last - b).astype(_bf16)
                state_ref[direction, hd] = jnp.exp2(b_last) * st + _dot_tn(v, k_dec)
                units.append((o_ref, rows, cols, scores, v, inter))

        for o_ref, rows, cols, scores, v, inter in units:
            o_ref[rows, cols] = inter + _dot(scores.astype(_bf16), v)
        return carry

    lax.fori_loop(0, chunks, chunk_step, 0, unroll=HG_CHUNK_UNROLL)


def _hgrn2(proj, gates, dmat, mask, tb):
    S, T, _ = proj.shape
    n = T // tb
    W = HG_WIDTH
    fwd = lambda col: (lambda s, i: (s, i, col))
    bwd = lambda col: (lambda s, i: (s, n - 1 - i, col))
    blk = lambda im: pl.BlockSpec((None, tb, W), im)
    return pl.pallas_call(
        functools.partial(_hgrn_kernel, chunks=tb // HG_CHUNK),
        grid=(S, n),
        in_specs=[
            blk(fwd(COL_Q)), blk(bwd(COL_Q)), blk(fwd(COL_V)), blk(bwd(COL_V)),
            blk(fwd(0)), blk(bwd(1)),
            _const_spec(dmat.shape), _const_spec(mask.shape),
        ],
        out_specs=[blk(fwd(0)), blk(bwd(0))],
        out_shape=[jax.ShapeDtypeStruct((S, T, W), _f32)] * 2,
        scratch_shapes=[pltpu.VMEM((2, HG_HEADS, HG_D, HG_D), _f32)],
        compiler_params=_params("parallel", "arbitrary"),
        name="hgrn2",
    )(proj, proj, proj, proj, gates, gates, dmat, mask)


def _natten_bias_kernel(rpb_ref, o_ref):
    W = GRID_W
    qc = lax.broadcasted_iota(jnp.int32, (W, LANE), 0)
    lane = lax.broadcasted_iota(jnp.int32, (W, LANE), 1)
    kc = lane & (W - 1)
    cs = jnp.clip(qc - NA_KW // 2, 0, W - NA_KW)
    valid = (kc >= cs) & (kc < cs + NA_KW)
    low = lane < W
    t_lo, t_hi = [], []
    for dr in range(2 * NA_KH - 1):
        row = jnp.broadcast_to(rpb_ref[dr:dr + 1, :], (W, LANE))
        t = pltpu.roll(row, LANE - (NA_KW - 1), 1, stride=1, stride_axis=0)
        t_lo.append(t)
        t_hi.append(pltpu.roll(t, W, 1))
    for case in range(NA_KH):
        for j in range(NA_KH // 2):
            d0 = 2 * j - case + NA_KH - 1
            tile = jnp.where(low, t_lo[d0], t_hi[d0 + 1])
            o_ref[case, :, j * LANE:(j + 1) * LANE] = jnp.where(valid, tile * LOG2_E, NEG_BIG)


def _natten_bias(rpb):
    W = GRID_W
    rows, cols = 2 * NA_KH - 1, 2 * NA_KW - 1
    padded = jnp.pad(rpb.astype(_f32), ((0, 0), (0, 0), (0, LANE - cols)))
    return pl.pallas_call(
        _natten_bias_kernel,
        grid=(NA_HEADS,),
        in_specs=[pl.BlockSpec((None, rows, LANE), lambda h: (h, 0, 0))],
        out_specs=pl.BlockSpec((NA_KH, None, W, NA_KH * W), lambda h: (0, h // 2, h % 2, 0)),
        out_shape=jax.ShapeDtypeStruct((NA_KH, NA_HEADS // 2, 2 * W, NA_KH * W), _f32),
        compiler_params=_params("parallel"),
        name="natten_bias",
    )(padded)


def _natten_kernel(q_ref, kp_ref, kc_ref, kn_ref, vp_ref, vc_ref, vn_ref, bias_ref, ow_ref,
                   o_ref, kbuf, vbuf, *, grid_rows):
    R = NA_ROWS_PER_STEP
    W = GRID_W
    blk = R * W
    for j, (k_r, v_r) in enumerate(((kp_ref, vp_ref), (kc_ref, vc_ref), (kn_ref, vn_ref))):
        kbuf[j * blk:(j + 1) * blk, :] = k_r[...]
        vbuf[j * blk:(j + 1) * blk, :] = v_r[...]
    r0 = pl.program_id(1) * R
    lane = lax.broadcasted_iota(jnp.int32, (W, LANE), 1)
    first = lane < NA_DH

    def row_group(ig, carry):
        units = []
        for k in range(NA_ROWS_PER_ITER):
            i = ig * NA_ROWS_PER_ITER + k
            r = r0 + i
            rs = jnp.clip(r - NA_KH // 2, 0, grid_rows - NA_KH)
            off = pl.multiple_of((rs - (r0 - R)) * W, W)
            for p in range(NA_HEADS // 2):
                units.append((pl.ds(pl.multiple_of(i * W, W), W), slice(p * LANE, (p + 1) * LANE),
                              pl.ds(off, NA_KH * W), r - rs, p))
        scores = []
        for qrows, cols, win, case, p in units:
            qp = q_ref[qrows, cols]
            zero = jnp.zeros_like(qp)
            lhs = jnp.concatenate([jnp.where(first, qp, zero), jnp.where(first, zero, qp)], axis=0)
            scores.append(_dot_nt(lhs, kbuf[win, cols]))
        probs = []
        for (qrows, cols, win, case, p), s in zip(units, scores):
            s = s + bias_ref[case, p]
            e = jnp.exp2(s - jnp.max(s, axis=-1, keepdims=True))
            probs.append((e.astype(_bf16), jnp.sum(e, axis=-1, keepdims=True)))
        for (qrows, cols, win, case, p), (e, l) in zip(units, probs):
            pv = _dot(e, vbuf[win, cols]) / l
            out = jnp.where(first, pv[0:W], pv[W:2 * W])
            o_ref[qrows, cols] = (out * _head_pair_rsqrt(out) * ow_ref[:, cols]).astype(_bf16)
        return carry

    lax.fori_loop(0, R // NA_ROWS_PER_ITER, row_group, 0)


def _natten(proj, bias, out_w):
    S, T, _ = proj.shape
    rows = T // GRID_W
    assert rows >= NA_KH and rows % NA_ROWS_PER_STEP == 0
    nb = rows // NA_ROWS_PER_STEP
    blk_t = NA_ROWS_PER_STEP * GRID_W
    Wd = NA_WIDTH
    cur = lambda col: (lambda s, i: (s, i, col))
    prv = lambda col: (lambda s, i: (s, jnp.maximum(i - 1, 0), col))
    nxt = lambda col: (lambda s, i: (s, jnp.minimum(i + 1, nb - 1), col))
    blk = lambda im: pl.BlockSpec((None, blk_t, Wd), im)
    return pl.pallas_call(
        functools.partial(_natten_kernel, grid_rows=rows),
        grid=(S, nb),
        in_specs=[
            blk(cur(COL_AQ)),
            blk(prv(COL_AK)), blk(cur(COL_AK)), blk(nxt(COL_AK)),
            blk(prv(COL_AV)), blk(cur(COL_AV)), blk(nxt(COL_AV)),
            _const_spec(bias.shape), _const_spec(out_w.shape),
        ],
        out_specs=blk(cur(0)),
        out_shape=jax.ShapeDtypeStruct((S, T, Wd), _bf16),
        scratch_shapes=[pltpu.VMEM((3 * blk_t, Wd), _bf16), pltpu.VMEM((3 * blk_t, Wd), _bf16)],
        compiler_params=_params("parallel", "parallel"),
        name="natten",
    )(proj, proj, proj, proj, proj, proj, proj, bias, out_w)


def _out_proj_kernel(x_ref, of_ref, ob_ref, hg_ref, na_ref, w_ref, hw_ref, n2_ref, rw_ref, rb_ref, tril_ref,
                     x1_ref, h2_ref, route_ref, w1_ref, w2_ref, cnt_ref, carry_ref, *, steps_per_tile):
    @pl.when(pl.program_id(1) % steps_per_tile == 0)
    def _():
        carry_ref[...] = jnp.zeros_like(carry_ref)

    tm = x_ref.shape[0]
    pr = tm // OUT_PROJ_PARTS
    parts_rows = [slice(j * pr, (j + 1) * pr) for j in range(OUT_PROJ_PARTS)]

    mixes = []
    for rows in parts_rows:
        o = of_ref[rows, :] + ob_ref[rows, :]
        heads = []
        for hd in range(HG_HEADS):
            oh = o[:, hd * HG_D:(hd + 1) * HG_D]
            ms = jnp.mean(oh * oh, axis=-1, keepdims=True)
            heads.append(oh * lax.rsqrt(ms + EPS))
        on = jnp.concatenate(heads, axis=-1) * hw_ref[...]
        sig = 1.0 / (1.0 + jnp.exp2(hg_ref[rows, :].astype(_f32)))
        mixes.append((on * sig).astype(_bf16))
    accs = [_dot(mix_a, w_ref[0:HG_WIDTH, :]) + _dot(na_ref[rows, :], w_ref[HG_WIDTH:, :])
            for rows, mix_a in zip(parts_rows, mixes)]
    splits = []
    for j, (rows, acc) in enumerate(zip(parts_rows, accs)):
        x1 = x_ref[rows, :] + acc
        x1_ref[rows, :] = x1
        ms2 = jnp.mean(x1 * x1, axis=-1, keepdims=True)
        h2 = x1 * lax.rsqrt(ms2 + EPS) * n2_ref[...]
        for c in range(TOKEN_SUBLANES):
            h2_ref[pl.ds(j * pr * TOKEN_SUBLANES + c, pr, stride=TOKEN_SUBLANES), :] = h2[:, c * LANE:(c + 1) * LANE]
        splits.append(_split2(h2))

    w_hi = rw_ref[0]
    w_lo = rw_ref[1]
    lg = jnp.concatenate([_dot(h_hi, w_hi) + _dot(h_lo, w_hi) + _dot(h_hi, w_lo) for h_hi, h_lo in splits],
                         axis=0) + rb_ref[...]
    lane_i = lax.broadcasted_iota(jnp.int32, lg.shape, 1)
    lane = lane_i.astype(_f32)
    lane_grp = ((lane_i - EXPERT_LANE0) >> 3).astype(_f32)

    def first_argmax(vals, valid):
        masked = jnp.where(valid, vals, NEG_BIG)
        top = jnp.max(masked, axis=-1, keepdims=True)
        idx = jnp.min(jnp.where(valid & (masked == top), lane, float(ROUTER_LANES)), axis=-1, keepdims=True)
        return top, idx

    is_grp = lane_i < N_GROUPS
    g_top, g_idx = first_argmax(lg, is_grp)
    p_top = 1.0 / jnp.sum(jnp.where(is_grp, jnp.exp(lg - g_top), 0.0), axis=-1, keepdims=True)
    in_grp = (lane_i >= EXPERT_LANE0) & (lane_i < EXPERT_LANE0 + N_EXPERTS) & (lane_grp == g_idx)
    t1, i1 = first_argmax(lg, in_grp)
    t2, i2 = first_argmax(lg, in_grp & (lane != i1))
    e2 = jnp.exp(t2 - t1)
    w1 = p_top / (1.0 + e2)
    w2 = p_top * e2 / (1.0 + e2)
    w1_ref[...] = jnp.broadcast_to(w1, w1_ref.shape)
    w2_ref[...] = jnp.broadcast_to(w2, w2_ref.shape)

    sel1 = lane == i1
    sel2 = lane == i2
    onehot = jnp.where(sel1 | sel2, 1.0, 0.0)
    before = _dot(tril_ref[...], onehot.astype(_bf16)) + carry_ref[...]
    rank1 = jnp.sum(jnp.where(sel1, before, 0.0), axis=-1, keepdims=True)
    rank2 = jnp.sum(jnp.where(sel2, before, 0.0), axis=-1, keepdims=True)
    carry = carry_ref[...] + jnp.sum(onehot, axis=0, keepdims=True)
    carry_ref[...] = carry
    cnt_ref[...] = jnp.broadcast_to(carry, cnt_ref.shape)
    route = jnp.where(lane_i == 0, i1 - EXPERT_LANE0,
            jnp.where(lane_i == 1, i2 - EXPERT_LANE0,
            jnp.where(lane_i == 2, rank1, jnp.where(lane_i == 3, rank2, 0.0))))
    route_ref[...] = jnp.transpose(route)[:ROUTE_ROWS, :]


def _out_proj(x, o_f, o_b, proj, na, w_out_bf, hw, n2w, rw, rb, tril, tm, moe_tm):
    S, T, D = x.shape
    W = HG_WIDTH
    n = T // tm
    row = lambda width, col=0: pl.BlockSpec((None, tm, width), lambda s, i: (s, i, col))
    lanes = jax.ShapeDtypeStruct((S, T, ROUTER_LANES), _f32)
    return pl.pallas_call(
        functools.partial(_out_proj_kernel, steps_per_tile=moe_tm // tm),
        grid=(S, n),
        in_specs=[
            row(D), row(W), row(W), row(W, COL_G), row(W),
            _const_spec(w_out_bf.shape), _const_spec(hw.shape), _const_spec(n2w.shape),
            _const_spec(rw.shape), _const_spec(rb.shape), _const_spec(tril.shape),
        ],
        out_specs=[row(D), pl.BlockSpec((None, tm * TOKEN_SUBLANES, LANE), lambda s, i: (s, i, 0)),
                   pl.BlockSpec((None, ROUTE_ROWS, tm), lambda s, i: (s, 0, i)),
                   row(ROUTER_LANES), row(ROUTER_LANES),
                   pl.BlockSpec((None, None, 8, ROUTER_LANES), lambda s, i: (s, i, 0, 0))],
        out_shape=[
            jax.ShapeDtypeStruct((S, T, D), _f32),
            jax.ShapeDtypeStruct((S, T * TOKEN_SUBLANES, LANE), _f32),
            jax.ShapeDtypeStruct((S, ROUTE_ROWS, T), _f32), lanes, lanes,
            jax.ShapeDtypeStruct((S, n, 8, ROUTER_LANES), _f32),
        ],
        scratch_shapes=[pltpu.VMEM((1, ROUTER_LANES), _f32)],
        compiler_params=_params("parallel", "arbitrary"),
        name="out_proj",
    )(x, o_f, o_b, proj, na, w_out_bf, hw, n2w, rw, rb, tril)


MOE_WINDOWS = (128, 288)
MOE_UNROLL = 16
MOE_STREAM_BLOCK = 256
MOE_SLOTS = 3
MOE_AHEAD = MOE_SLOTS - 1


def _moe_rows(tm):
    return 2 * tm + MOE_WINDOWS[-1]


def _moe_kernel(pos1_ref, pos2_ref, seg_ref, x1_hbm, h2_hbm, w1_ref, w2_ref, wg_ref, wu_ref, wd_ref, y_hbm,
                buf_ref, mix_ref, h2_stage, x1_stage, y_stage, h2_sem, x1_sem, y_sem, *, tm, sb):
    i = pl.program_id(0)
    e = pl.program_id(1)
    TS = TOKEN_SUBLANES
    nsb = tm // sb

    def token_tile(row):
        return pl.ds(pl.multiple_of(row * TS, TS), TS)

    def buffer_tiles(t):
        return tuple(pl.ds(pl.multiple_of(ref[i * tm + t], TS), TS) for ref in (pos1_ref, pos2_ref))

    def token_rows(blk):
        return pl.ds(pl.multiple_of(i * tm + blk * sb, sb), sb)

    def h2_copy(blk, slot, tile=i):
        src = h2_hbm.at[pl.ds(pl.multiple_of((tile * tm + blk * sb) * TS, sb * TS), sb * TS), :]
        return pltpu.make_async_copy(src, h2_stage.at[slot], h2_sem.at[slot])

    def x1_copy(blk, slot):
        return pltpu.make_async_copy(x1_hbm.at[token_rows(blk), :], x1_stage.at[slot], x1_sem.at[slot])

    def y_copy(blk, slot):
        return pltpu.make_async_copy(y_stage.at[slot], y_hbm.at[token_rows(blk), :], y_sem.at[slot])

    @pl.when(e == 0)
    def _():
        slack = pl.ds(2 * tm * TS, MOE_WINDOWS[-1] * TS)
        buf_ref[slack, :] = jnp.zeros((MOE_WINDOWS[-1] * TS, LANE), _f32)

        @pl.when(i == 0)
        def _():
            for blk in range(MOE_AHEAD):
                h2_copy(blk, blk).start()

        def dispatch(blk, carry):
            slot = blk % MOE_SLOTS
            h2_copy(blk, slot).wait()

            @pl.when(blk + MOE_AHEAD < nsb)
            def _():
                h2_copy(blk + MOE_AHEAD, (blk + MOE_AHEAD) % MOE_SLOTS).start()

            def tokens(grp, c2):
                for k in range(MOE_UNROLL):
                    tl = grp * MOE_UNROLL + k
                    p1, p2 = buffer_tiles(blk * sb + tl)
                    tok = h2_stage[slot, token_tile(tl), :]
                    buf_ref[p1, :] = tok
                    buf_ref[p2, :] = tok
                return c2

            lax.fori_loop(0, sb // MOE_UNROLL, tokens, 0)
            return carry

        lax.fori_loop(0, nsb, dispatch, 0)

    start = seg_ref[(i * N_EXPERTS + e) * 2]
    count = seg_ref[(i * N_EXPERTS + e) * 2 + 1]

    def window(rows_w, done):
        r0 = start + done
        chunk = lambda c: pl.ds(r0 * TS + c, rows_w, stride=TS)
        xs = [buf_ref[chunk(c), :] for c in range(TS)]
        xb = jnp.concatenate(xs, axis=-1).astype(_bf16)
        a = _dot(xb, wg_ref[...])
        u = _dot(xb, wu_ref[...])
        hmid = a * (1.0 / (1.0 + jnp.exp(-a))) * u
        out = _dot(hmid.astype(_bf16), wd_ref[...])
        valid = lax.broadcasted_iota(jnp.int32, (rows_w, LANE), 0) + done < count
        for c in range(TS):
            buf_ref[chunk(c), :] = jnp.where(valid, out[:, c * LANE:(c + 1) * LANE], xs[c])

    @pl.when(e == N_EXPERTS - 1)
    def _():
        for blk in range(MOE_AHEAD):
            x1_copy(blk, blk).start()

        @pl.when(i + 1 < pl.num_programs(0))
        def _():
            for blk in range(MOE_AHEAD):
                h2_copy(blk, blk, tile=i + 1).start()

    small, large = MOE_WINDOWS

    n_large = count // large
    tail = count - n_large * large

    def body(j, carry):
        window(large, j * large)
        return carry

    lax.fori_loop(0, n_large, body, 0)

    @pl.when(tail > small)
    def _():
        window(large, n_large * large)

    @pl.when((tail > 0) & (tail <= small))
    def _():
        window(small, n_large * large)

    @pl.when(e == N_EXPERTS - 1)
    def _():
        def combine(blk, carry):
            slot = blk % MOE_SLOTS

            @pl.when(blk + MOE_AHEAD < nsb)
            def _():
                x1_copy(blk + MOE_AHEAD, (blk + MOE_AHEAD) % MOE_SLOTS).start()

            def tokens(grp, c2):
                for k in range(MOE_UNROLL):
                    tl = grp * MOE_UNROLL + k
                    t = blk * sb + tl
                    p1, p2 = buffer_tiles(t)
                    g1 = jnp.broadcast_to(w1_ref[pl.ds(t, 1), :], (TS, LANE))
                    g2 = jnp.broadcast_to(w2_ref[pl.ds(t, 1), :], (TS, LANE))
                    mix_ref[token_tile(tl), :] = g1 * buf_ref[p1, :] + g2 * buf_ref[p2, :]
                return c2

            lax.fori_loop(0, sb // MOE_UNROLL, tokens, 0)
            x1_copy(blk, slot).wait()

            @pl.when((blk >= MOE_SLOTS) | (i > 0))
            def _():
                y_copy(blk, slot).wait()

            for c in range(TS):
                cols = slice(c * LANE, (c + 1) * LANE)
                y_stage[slot, :, cols] = x1_stage[slot, :, cols] + mix_ref[pl.ds(c, sb, stride=TS), :]
            y_copy(blk, slot).start()
            return carry

        lax.fori_loop(0, nsb, combine, 0)

        @pl.when(i + 1 == pl.num_programs(0))
        def _():
            for blk in range(nsb - MOE_SLOTS, nsb):
                y_copy(blk, blk % MOE_SLOTS).wait()


def _moe(pos1, pos2, seg, x1, h2, w1, w2, wg, wu, wd, tm):
    N, D = x1.shape
    sb = min(MOE_STREAM_BLOCK, tm // 4)
    assert tm % sb == 0 and tm // sb >= MOE_SLOTS and sb % MOE_UNROLL == 0
    tile = lambda width: pl.BlockSpec((tm, width), lambda i, e, *_: (i, 0), pipeline_mode=pl.Buffered(1))
    wspec = lambda a, b: pl.BlockSpec((None, a, b), lambda i, e, *_: (e, 0, 0))
    hbm = pl.BlockSpec(memory_space=pl.ANY)
    return pl.pallas_call(
        functools.partial(_moe_kernel, tm=tm, sb=sb),
        grid_spec=pltpu.PrefetchScalarGridSpec(
            num_scalar_prefetch=3,
            grid=(N // tm, N_EXPERTS),
            in_specs=[
                hbm, hbm, tile(ROUTER_LANES), tile(ROUTER_LANES),
                wspec(D, EXPERT_FF), wspec(D, EXPERT_FF), wspec(EXPERT_FF, D),
            ],
            out_specs=hbm,
            scratch_shapes=[
                pltpu.VMEM((_moe_rows(tm) * TOKEN_SUBLANES, LANE), _f32),
                pltpu.VMEM((sb * TOKEN_SUBLANES, LANE), _f32),
                pltpu.VMEM((MOE_SLOTS, sb * TOKEN_SUBLANES, LANE), _f32),
                pltpu.VMEM((MOE_SLOTS, sb, D), _f32),
                pltpu.VMEM((MOE_SLOTS, sb, D), _f32),
                pltpu.SemaphoreType.DMA((MOE_SLOTS,)),
                pltpu.SemaphoreType.DMA((MOE_SLOTS,)),
                pltpu.SemaphoreType.DMA((MOE_SLOTS,)),
            ],
        ),
        out_shape=jax.ShapeDtypeStruct((N, D), _f32),
        compiler_params=_params("arbitrary", "arbitrary"),
        name="moe",
    )(pos1, pos2, seg, x1, h2, w1, w2, wg, wu, wd)


def _moe_plan(route, cnt, tm, steps_per_tile):
    S, _, T = route.shape
    N = S * T
    n_tiles = N // tm
    counts = cnt.reshape(-1, 8, ROUTER_LANES)[steps_per_tile - 1::steps_per_tile, 0,
                                               EXPERT_LANE0:EXPERT_LANE0 + N_EXPERTS].astype(jnp.int32)
    seg_start = jnp.cumsum(counts, axis=1) - counts
    r = route[:, :4, :].astype(jnp.int32).transpose(1, 0, 2).reshape(4, n_tiles, tm)

    def row_of(expert, rank):
        start = sum(jnp.where(expert == e, seg_start[:, e:e + 1], 0) for e in range(N_EXPERTS))
        return start + rank

    pos1 = row_of(r[0], r[2]) * TOKEN_SUBLANES
    pos2 = row_of(r[1], r[3]) * TOKEN_SUBLANES
    seg = jnp.stack([seg_start, counts], axis=-1)
    return pos1.reshape(-1), pos2.reshape(-1), seg.reshape(-1)


def _lower_bound(logits, l):
    return jnp.cumsum(jax.nn.softmax(logits.astype(_f32), axis=0), axis=0)[l]


ROW_TILE = 512
HGRN_BLOCK = 512
MOE_TILE = 4096


def _layer(x, l, c):
    S, T, D = x.shape
    tm = min(ROW_TILE, T)
    moe_tm = min(MOE_TILE, T)
    proj, gates = _in_proj(x, c["n1w"][l], c["w_in"][l], c["lb"][l], c["qkw"][l], tm)
    o_f, o_b = _hgrn2(proj, gates, c["dmat"], c["mask"], min(HGRN_BLOCK, T))
    na = _natten(proj, c["na_bias"][l], c["na_ow"][l])
    x1, h2, route, w1, w2, cnt = _out_proj(x, o_f, o_b, proj, na, c["w_out"][l], c["hw"][l], c["n2w"][l],
                                           c["rw"][l], c["rb"][l], c["tril"][:tm, :tm], tm, moe_tm)
    N = S * T
    flat = lambda a: a.reshape(N, a.shape[-1])
    pos1, pos2, seg = _moe_plan(route, cnt, moe_tm, moe_tm // tm)
    y = _moe(pos1, pos2, seg, flat(x1), h2.reshape(N * TOKEN_SUBLANES, LANE), flat(w1), flat(w2),
             c["wg"][l], c["wu"][l], c["wd"][l], moe_tm)
    return y.reshape(S, T, D)


def _prepare(norm1_w, w_in, lb_fwd_logits, lb_bwd_logits, hgrn_norm_w, q_norm_w, k_norm_w, na_rpb,
             na_out_norm_w, w_out, norm2_w, w_router_group, b_router_group, w_router_expert,
             b_router_expert, w_expert_gate, w_expert_up, w_expert_down):
    depth = w_in.shape[0]
    dmat, mask = _hgrn_constants()
    col_scale = np.ones((1, 1, w_in.shape[-1]), np.float32)
    col_scale[..., 4 * HG_WIDTH:5 * HG_WIDTH] = -LOG2_E
    pad = ROUTER_LANES - N_GROUPS - N_EXPERTS
    rw = jnp.pad(jnp.concatenate([w_router_group, w_router_expert], axis=-1).astype(_f32),
                 ((0, 0), (0, 0), (0, pad)))
    rw_hi = rw.astype(_bf16)
    rw_lo = (rw - rw_hi.astype(_f32)).astype(_bf16)
    rb = jnp.pad(jnp.concatenate([b_router_group, b_router_expert], axis=-1).astype(_f32),
                 ((0, 0), (0, pad)))
    return {
        "n1w": norm1_w.astype(_f32)[:, None, :],
        "w_in": (w_in.astype(_f32) * col_scale).astype(_bf16),
        "lb": jnp.stack([jnp.stack([_lower_bound(lb_fwd_logits, l), _lower_bound(lb_bwd_logits, l)])
                         for l in range(depth)]),
        "qkw": jnp.stack([jnp.tile(q_norm_w.astype(_f32), (1, NA_HEADS)) * (NA_DH ** -0.5 * LOG2_E),
                          jnp.tile(k_norm_w.astype(_f32), (1, NA_HEADS))], axis=1),
        "dmat": jnp.asarray(dmat, _bf16),
        "mask": jnp.asarray(mask, _f32),
        "na_bias": jnp.stack([_natten_bias(na_rpb[l]) for l in range(depth)]),
        "na_ow": na_out_norm_w.astype(_f32)[:, None, :],
        "w_out": w_out.astype(_bf16),
        "hw": hgrn_norm_w.astype(_f32)[:, None, :],
        "n2w": norm2_w.astype(_f32)[:, None, :],
        "rw": jnp.stack([rw_hi, rw_lo], axis=1),
        "rb": rb[:, None, :],
        "tril": jnp.asarray(np.tril(np.ones((ROW_TILE, ROW_TILE), np.float32), -1), _bf16),
        "wg": w_expert_gate.astype(_bf16),
        "wu": w_expert_up.astype(_bf16),
        "wd": w_expert_down.astype(_bf16),
    }


def kernel(x_prompt, x_sample, norm1_w, w_in, lb_fwd_logits, lb_bwd_logits, hgrn_norm_w, q_norm_w, k_norm_w, na_rpb, na_out_norm_w, w_out, norm2_w, w_router_group, b_router_group, w_router_expert, b_router_expert, w_expert_gate, w_expert_up, w_expert_down):
    c = _prepare(norm1_w, w_in, lb_fwd_logits, lb_bwd_logits, hgrn_norm_w, q_norm_w, k_norm_w, na_rpb,
                 na_out_norm_w, w_out, norm2_w, w_router_group, b_router_group, w_router_expert,
                 b_router_expert, w_expert_gate, w_expert_up, w_expert_down)
    outs = []
    for x in (x_prompt, x_sample):
        for l in range(w_in.shape[0]):
            x = _layer(x, l, c)
        outs.append(x)
    return tuple(outs)
```

```python
import functools

import numpy as np
import jax
import jax.numpy as jnp
from jax import lax
from jax.experimental import pallas as pl
from jax.experimental.pallas import tpu as pltpu

D_MODEL = 1024
EPS = 1e-6
GRID_W = 64
HG_HEADS = 4
HG_D = 128
HG_WIDTH = HG_HEADS * HG_D
HG_CHUNK = 64
HG_LEVELS = 6
HG_MATMUL_LEVELS = 3
HG_CHUNK_UNROLL = 8
ROUTE_ROWS = 8
OUT_PROJ_PARTS = 2
NA_HEADS = 8
NA_DH = 64
NA_WIDTH = NA_HEADS * NA_DH
NA_KH = 8
NA_KW = 16
NA_ROWS_PER_STEP = 8
NA_ROWS_PER_ITER = 8
N_GROUPS = 4
EXPERTS_PER_GROUP = 8
N_EXPERTS = N_GROUPS * EXPERTS_PER_GROUP
EXPERT_FF = 512
ROUTER_LANES = 128
EXPERT_LANE0 = N_GROUPS
NEG_BIG = -1e30
LOG2_E = 1.4426950408889634
LANE = 128
TOKEN_SUBLANES = D_MODEL // LANE
VMEM_LIMIT = 56 * 1024 * 1024

COL_Q, COL_V, COL_G, COL_AQ, COL_AK, COL_AV = range(6)

_f32 = jnp.float32
_bf16 = jnp.bfloat16


def _dot(a, b):
    return jnp.dot(a, b, preferred_element_type=_f32)


def _dot_nt(a, b):
    return lax.dot_general(a, b, (((1,), (1,)), ((), ())), preferred_element_type=_f32)


def _dot_tn(a, b):
    return lax.dot_general(a, b, (((0,), (0,)), ((), ())), preferred_element_type=_f32)


def _split2(x):
    hi = x.astype(_bf16)
    lo = (x - hi.astype(_f32)).astype(_bf16)
    return hi, lo


def _head_pair_rsqrt(a):
    first = lax.broadcasted_iota(jnp.int32, a.shape, 1) < NA_DH
    sq = a * a
    s_first = jnp.sum(jnp.where(first, sq, 0.0), axis=-1, keepdims=True)
    s_second = jnp.sum(jnp.where(first, 0.0, sq), axis=-1, keepdims=True)
    return lax.rsqrt(jnp.where(first, s_first, s_second) * (1.0 / NA_DH) + EPS)


def _params(*sem):
    return pltpu.CompilerParams(dimension_semantics=sem, vmem_limit_bytes=VMEM_LIMIT)


def _const_spec(shape):
    nd = len(shape)
    return pl.BlockSpec(shape, lambda *_: (0,) * nd)


def _in_proj_kernel(x_ref, n1_ref, w_ref, lb_ref, qkw_ref, proj_ref, gate_ref):
    x = x_ref[...]
    ms = jnp.mean(x * x, axis=-1, keepdims=True)
    h = (x * lax.rsqrt(ms + EPS) * n1_ref[...]).astype(_bf16)
    W = HG_WIDTH

    def col(j):
        return _dot(h, w_ref[:, j * W:(j + 1) * W])

    proj_ref[:, COL_Q * W:(COL_Q + 1) * W] = col(0).astype(_bf16)
    for d in range(2):
        z = col(1 + d)
        lb = lb_ref[d:d + 1, :]
        f = lb + (1.0 - lb) * (1.0 / (1.0 + jnp.exp(-z)))
        gate_ref[:, d * W:(d + 1) * W] = jnp.log(f) * LOG2_E
    proj_ref[:, COL_V * W:(COL_V + 1) * W] = col(3).astype(_bf16)
    proj_ref[:, COL_G * W:(COL_G + 1) * W] = col(4).astype(_bf16)
    for d, c in ((0, COL_AQ), (1, COL_AK)):
        a = col(5 + d)
        for p in range(NA_HEADS // 2):
            ap = a[:, p * LANE:(p + 1) * LANE]
            lanes = slice(c * W + p * LANE, c * W + (p + 1) * LANE)
            proj_ref[:, lanes] = (ap * _head_pair_rsqrt(ap)
                                  * qkw_ref[d:d + 1, p * LANE:(p + 1) * LANE]).astype(_bf16)
    proj_ref[:, COL_AV * W:(COL_AV + 1) * W] = col(7).astype(_bf16)


def _in_proj(x, n1w, w_in_bf, lb, qkw, tm):
    S, T, D = x.shape
    return pl.pallas_call(
        _in_proj_kernel,
        grid=(S, T // tm),
        in_specs=[
            pl.BlockSpec((None, tm, D), lambda s, i: (s, i, 0)),
            _const_spec((1, D)),
            _const_spec(w_in_bf.shape),
            _const_spec(lb.shape),
            _const_spec(qkw.shape),
        ],
        out_specs=[
            pl.BlockSpec((None, tm, 6 * HG_WIDTH), lambda s, i: (s, i, 0)),
            pl.BlockSpec((None, tm, 2 * HG_WIDTH), lambda s, i: (s, i, 0)),
        ],
        out_shape=[
            jax.ShapeDtypeStruct((S, T, 6 * HG_WIDTH), _bf16),
            jax.ShapeDtypeStruct((S, T, 2 * HG_WIDTH), _f32),
        ],
        compiler_params=_params("parallel", "parallel"),
        name="in_proj",
    )(x, n1w, w_in_bf, lb, qkw)


def _hgrn_constants():
    C = HG_CHUNK
    idx = np.arange(C)
    dmats, masks = [], []
    for direction in (0, 1):
        if direction == 0:
            tri = (idx[None, :] <= idx[:, None]).astype(np.float64)
        else:
            tri = (idx[None, :] >= idx[:, None]).astype(np.float64)
        dm = [tri]
        mk = []
        for lvl in range(HG_LEVELS):
            h = 1 << lvl
            start = (idx // (2 * h)) * (2 * h)
            ref = start + h - 1 if direction == 0 else start + h
            upper = (idx & h) != 0
            later, earlier = (upper, ~upper) if direction == 0 else (~upper, upper)
            if lvl < HG_MATMUL_LEVELS:
                gather = np.zeros((C, C))
                gather[idx, ref] = 1.0
                dm.append(np.where(later[:, None], 1.0, -1.0) * (tri - gather @ tri))
            same = (idx[:, None] // (2 * h)) == (idx[None, :] // (2 * h))
            mk.append((same & later[:, None] & earlier[None, :]).astype(np.float64))
        dmats.append(np.concatenate(dm, axis=0))
        masks.append(np.stack(mk, axis=0))
    return np.stack(dmats, 0), np.stack(masks, 0)


def _hgrn_kernel(qf_ref, qb_ref, vf_ref, vb_ref, gf_ref, gb_ref, dmat_ref, mask_ref,
                 of_ref, ob_ref, state_ref, *, chunks):
    C = HG_CHUNK

    @pl.when(pl.program_id(1) == 0)
    def _():
        state_ref[...] = jnp.zeros_like(state_ref)

    def chunk_step(c, carry):
        deltas = []
        for direction, g_ref in enumerate((gf_ref, gb_ref)):
            cc = c if direction == 0 else chunks - 1 - c
            rows = pl.ds(pl.multiple_of(cc * C, C), C)
            g = g_ref[rows, :]
            g_hi, g_lo = _split2(g)
            dmat = dmat_ref[direction]
            deltas.append((rows, g, _dot(dmat, g_hi) + _dot(dmat, g_lo)))

        units = []
        for direction, (q_ref, v_ref, o_ref) in enumerate(((qf_ref, vf_ref, of_ref), (qb_ref, vb_ref, ob_ref))):
            rows, g, delta_all = deltas[direction]
            last = C - 1 if direction == 0 else 0
            for hd in range(HG_HEADS):
                cols = slice(hd * HG_D, (hd + 1) * HG_D)
                q = q_ref[rows, cols]
                v = v_ref[rows, cols]
                k = (1.0 - jnp.exp2(g[:, cols])).astype(_bf16)
                b = delta_all[0:C, cols]
                v32 = v.astype(_f32)
                diag = jnp.sum(q.astype(_f32) * k.astype(_f32), axis=-1, keepdims=True) * v32
                scores = None
                for lvl in range(HG_LEVELS):
                    if lvl < HG_MATMUL_LEVELS:
                        d = delta_all[(lvl + 1) * C:(lvl + 2) * C, cols]
                    else:
                        h = 1 << lvl
                        pieces = []
                        for s in range(0, C, 2 * h):
                            lo, hi = b[s:s + h, :], b[s + h:s + 2 * h, :]
                            if direction == 0:
                                ref = jnp.broadcast_to(b[s + h - 1:s + h, :], (h, HG_D))
                                pieces += [ref - lo, hi - ref]
                            else:
                                ref = jnp.broadcast_to(b[s + h:s + h + 1, :], (h, HG_D))
                                pieces += [lo - ref, ref - hi]
                        d = jnp.concatenate(pieces, axis=0)
                    e = jnp.exp2(d).astype(_bf16)
                    level = mask_ref[direction, lvl] * _dot_nt(q * e, k * e)
                    scores = level if scores is None else scores + level
                st = state_ref[direction, hd]
                b_last = b[last:last + 1, :]
                inter = diag + _dot_nt(q * jnp.exp2(b).astype(_bf16), st.astype(_bf16))
                k_dec = k * jnp.exp2(b_last - b).astype(_bf16)
                state_ref[direction, hd] = jnp.exp2(b_last) * st + _dot_tn(v, k_dec)
                units.append((o_ref, rows, cols, scores, v, inter))

        for o_ref, rows, cols, scores, v, inter in units:
            o_ref[rows, cols] = inter + _dot(scores.astype(_bf16), v)
        return carry

    lax.fori_loop(0, chunks, chunk_step, 0, unroll=HG_CHUNK_UNROLL)


def _hgrn2(proj, gates, dmat, mask, tb):
    S, T, _ = proj.shape
    n = T // tb
    W = HG_WIDTH
    fwd = lambda col: (lambda s, i: (s, i, col))
    bwd = lambda col: (lambda s, i: (s, n - 1 - i, col))
    blk = lambda im: pl.BlockSpec((None, tb, W), im)
    return pl.pallas_call(
        functools.partial(_hgrn_kernel, chunks=tb // HG_CHUNK),
        grid=(S, n),
        in_specs=[
            blk(fwd(COL_Q)), blk(bwd(COL_Q)), blk(fwd(COL_V)), blk(bwd(COL_V)),
            blk(fwd(0)), blk(bwd(1)),
            _const_spec(dmat.shape), _const_spec(mask.shape),
        ],
        out_specs=[blk(fwd(0)), blk(bwd(0))],
        out_shape=[jax.ShapeDtypeStruct((S, T, W), _f32)] * 2,
        scratch_shapes=[pltpu.VMEM((2, HG_HEADS, HG_D, HG_D), _f32)],
        compiler_params=_params("parallel", "arbitrary"),
        name="hgrn2",
    )(proj, proj, proj, proj, gates, gates, dmat, mask)


def _natten_bias_kernel(rpb_ref, o_ref):
    W = GRID_W
    qc = lax.broadcasted_iota(jnp.int32, (W, LANE), 0)
    lane = lax.broadcasted_iota(jnp.int32, (W, LANE), 1)
    kc = lane & (W - 1)
    cs = jnp.clip(qc - NA_KW // 2, 0, W - NA_KW)
    valid = (kc >= cs) & (kc < cs + NA_KW)
    low = lane < W
    t_lo, t_hi = [], []
    for dr in range(2 * NA_KH - 1):
        row = jnp.broadcast_to(rpb_ref[dr:dr + 1, :], (W, LANE))
        t = pltpu.roll(row, LANE - (NA_KW - 1), 1, stride=1, stride_axis=0)
        t_lo.append(t)
        t_hi.append(pltpu.roll(t, W, 1))
    for case in range(NA_KH):
        for j in range(NA_KH // 2):
            d0 = 2 * j - case + NA_KH - 1
            tile = jnp.where(low, t_lo[d0], t_hi[d0 + 1])
            o_ref[case, :, j * LANE:(j + 1) * LANE] = jnp.where(valid, tile * LOG2_E, NEG_BIG)


def _natten_bias(rpb):
    W = GRID_W
    rows, cols = 2 * NA_KH - 1, 2 * NA_KW - 1
    padded = jnp.pad(rpb.astype(_f32), ((0, 0), (0, 0), (0, LANE - cols)))
    return pl.pallas_call(
        _natten_bias_kernel,
        grid=(NA_HEADS,),
        in_specs=[pl.BlockSpec((None, rows, LANE), lambda h: (h, 0, 0))],
        out_specs=pl.BlockSpec((NA_KH, None, W, NA_KH * W), lambda h: (0, h // 2, h % 2, 0)),
        out_shape=jax.ShapeDtypeStruct((NA_KH, NA_HEADS // 2, 2 * W, NA_KH * W), _f32),
        compiler_params=_params("parallel"),
        name="natten_bias",
    )(padded)


def _natten_kernel(q_ref, kp_ref, kc_ref, kn_ref, vp_ref, vc_ref, vn_ref, bias_ref, ow_ref,
                   o_ref, kbuf, vbuf, *, grid_rows):
    R = NA_ROWS_PER_STEP
    W = GRID_W
    blk = R * W
    for j, (k_r, v_r) in enumerate(((kp_ref, vp_ref), (kc_ref, vc_ref), (kn_ref, vn_ref))):
        kbuf[j * blk:(j + 1) * blk, :] = k_r[...]
        vbuf[j * blk:(j + 1) * blk, :] = v_r[...]
    r0 = pl.program_id(1) * R
    lane = lax.broadcasted_iota(jnp.int32, (W, LANE), 1)
    first = lane < NA_DH

    def row_group(ig, carry):
        units = []
        for k in range(NA_ROWS_PER_ITER):
            i = ig * NA_ROWS_PER_ITER + k
            r = r0 + i
            rs = jnp.clip(r - NA_KH // 2, 0, grid_rows - NA_KH)
            off = pl.multiple_of((rs - (r0 - R)) * W, W)
            for p in range(NA_HEADS // 2):
                units.append((pl.ds(pl.multiple_of(i * W, W), W), slice(p * LANE, (p + 1) * LANE),
                              pl.ds(off, NA_KH * W), r - rs, p))
        scores = []
        for qrows, cols, win, case, p in units:
            qp = q_ref[qrows, cols]
            zero = jnp.zeros_like(qp)
            lhs = jnp.concatenate([jnp.where(first, qp, zero), jnp.where(first, zero, qp)], axis=0)
            scores.append(_dot_nt(lhs, kbuf[win, cols]))
        probs = []
        for (qrows, cols, win, case, p), s in zip(units, scores):
            s = s + bias_ref[case, p]
            e = jnp.exp2(s - jnp.max(s, axis=-1, keepdims=True))
            probs.append((e.astype(_bf16), jnp.sum(e, axis=-1, keepdims=True)))
        for (qrows, cols, win, case, p), (e, l) in zip(units, probs):
            pv = _dot(e, vbuf[win, cols]) / l
            out = jnp.where(first, pv[0:W], pv[W:2 * W])
            o_ref[qrows, cols] = (out * _head_pair_rsqrt(out) * ow_ref[:, cols]).astype(_bf16)
        return carry

    lax.fori_loop(0, R // NA_ROWS_PER_ITER, row_group, 0)


def _natten(proj, bias, out_w):
    S, T, _ = proj.shape
    rows = T // GRID_W
    assert rows >= NA_KH and rows % NA_ROWS_PER_STEP == 0
    nb = rows // NA_ROWS_PER_STEP
    blk_t = NA_ROWS_PER_STEP * GRID_W
    Wd = NA_WIDTH
    cur = lambda col: (lambda s, i: (s, i, col))
    prv = lambda col: (lambda s, i: (s, jnp.maximum(i - 1, 0), col))
    nxt = lambda col: (lambda s, i: (s, jnp.minimum(i + 1, nb - 1), col))
    blk = lambda im: pl.BlockSpec((None, blk_t, Wd), im)
    return pl.pallas_call(
        functools.partial(_natten_kernel, grid_rows=rows),
        grid=(S, nb),
        in_specs=[
            blk(cur(COL_AQ)),
            blk(prv(COL_AK)), blk(cur(COL_AK)), blk(nxt(COL_AK)),
            blk(prv(COL_AV)), blk(cur(COL_AV)), blk(nxt(COL_AV)),
            _const_spec(bias.shape), _const_spec(out_w.shape),
        ],
        out_specs=blk(cur(0)),
        out_shape=jax.ShapeDtypeStruct((S, T, Wd), _bf16),
        scratch_shapes=[pltpu.VMEM((3 * blk_t, Wd), _bf16), pltpu.VMEM((3 * blk_t, Wd), _bf16)],
        compiler_params=_params("parallel", "parallel"),
        name="natten",
    )(proj, proj, proj, proj, proj, proj, proj, bias, out_w)


def _out_proj_kernel(x_ref, of_ref, ob_ref, hg_ref, na_ref, w_ref, hw_ref, n2_ref, rw_ref, rb_ref, tril_ref,
                     x1_ref, h2_ref, route_ref, w1_ref, w2_ref, cnt_ref, carry_ref, *, steps_per_tile):
    @pl.when(pl.program_id(1) % steps_per_tile == 0)
    def _():
        carry_ref[...] = jnp.zeros_like(carry_ref)

    tm = x_ref.shape[0]
    pr = tm // OUT_PROJ_PARTS
    parts_rows = [slice(j * pr, (j + 1) * pr) for j in range(OUT_PROJ_PARTS)]

    mixes = []
    for rows in parts_rows:
        o = of_ref[rows, :] + ob_ref[rows, :]
        heads = []
        for hd in range(HG_HEADS):
            oh = o[:, hd * HG_D:(hd + 1) * HG_D]
            ms = jnp.mean(oh * oh, axis=-1, keepdims=True)
            heads.append(oh * lax.rsqrt(ms + EPS))
        on = jnp.concatenate(heads, axis=-1) * hw_ref[...]
        sig = 1.0 / (1.0 + jnp.exp2(hg_ref[rows, :].astype(_f32)))
        mixes.append((on * sig).astype(_bf16))
    accs = [_dot(mix_a, w_ref[0:HG_WIDTH, :]) + _dot(na_ref[rows, :], w_ref[HG_WIDTH:, :])
            for rows, mix_a in zip(parts_rows, mixes)]
    splits = []
    for j, (rows, acc) in enumerate(zip(parts_rows, accs)):
        x1 = x_ref[rows, :] + acc
        x1_ref[rows, :] = x1
        ms2 = jnp.mean(x1 * x1, axis=-1, keepdims=True)
        h2 = x1 * lax.rsqrt(ms2 + EPS) * n2_ref[...]
        for c in range(TOKEN_SUBLANES):
            h2_ref[pl.ds(j * pr * TOKEN_SUBLANES + c, pr, stride=TOKEN_SUBLANES), :] = h2[:, c * LANE:(c + 1) * LANE]
        splits.append(_split2(h2))

    w_hi = rw_ref[0]
    w_lo = rw_ref[1]
    lg = jnp.concatenate([_dot(h_hi, w_hi) + _dot(h_lo, w_hi) + _dot(h_hi, w_lo) for h_hi, h_lo in splits],
                         axis=0) + rb_ref[...]
    lane_i = lax.broadcasted_iota(jnp.int32, lg.shape, 1)
    lane = lane_i.astype(_f32)
    lane_grp = ((lane_i - EXPERT_LANE0) >> 3).astype(_f32)

    def first_argmax(vals, valid):
        masked = jnp.where(valid, vals, NEG_BIG)
        top = jnp.max(masked, axis=-1, keepdims=True)
        idx = jnp.min(jnp.where(valid & (masked == top), lane, float(ROUTER_LANES)), axis=-1, keepdims=True)
        return top, idx

    is_grp = lane_i < N_GROUPS
    g_top, g_idx = first_argmax(lg, is_grp)
    p_top = 1.0 / jnp.sum(jnp.where(is_grp, jnp.exp(lg - g_top), 0.0), axis=-1, keepdims=True)
    in_grp = (lane_i >= EXPERT_LANE0) & (lane_i < EXPERT_LANE0 + N_EXPERTS) & (lane_grp == g_idx)
    t1, i1 = first_argmax(lg, in_grp)
    t2, i2 = first_argmax(lg, in_grp & (lane != i1))
    e2 = jnp.exp(t2 - t1)
    w1 = p_top / (1.0 + e2)
    w2 = p_top * e2 / (1.0 + e2)
    w1_ref[...] = jnp.broadcast_to(w1, w1_ref.shape)
    w2_ref[...] = jnp.broadcast_to(w2, w2_ref.shape)

    sel1 = lane == i1
    sel2 = lane == i2
    onehot = jnp.where(sel1 | sel2, 1.0, 0.0)
    before = _dot(tril_ref[...], onehot.astype(_bf16)) + carry_ref[...]
    rank1 = jnp.sum(jnp.where(sel1, before, 0.0), axis=-1, keepdims=True)
    rank2 = jnp.sum(jnp.where(sel2, before, 0.0), axis=-1, keepdims=True)
    carry = carry_ref[...] + jnp.sum(onehot, axis=0, keepdims=True)
    carry_ref[...] = carry
    cnt_ref[...] = jnp.broadcast_to(carry, cnt_ref.shape)
    route = jnp.where(lane_i == 0, i1 - EXPERT_LANE0,
            jnp.where(lane_i == 1, i2 - EXPERT_LANE0,
            jnp.where(lane_i == 2, rank1, jnp.where(lane_i == 3, rank2, 0.0))))
    route_ref[...] = jnp.transpose(route)[:ROUTE_ROWS, :]


def _out_proj(x, o_f, o_b, proj, na, w_out_bf, hw, n2w, rw, rb, tril, tm, moe_tm):
    S, T, D = x.shape
    W = HG_WIDTH
    n = T // tm
    row = lambda width, col=0: pl.BlockSpec((None, tm, width), lambda s, i: (s, i, col))
    lanes = jax.ShapeDtypeStruct((S, T, ROUTER_LANES), _f32)
    return pl.pallas_call(
        functools.partial(_out_proj_kernel, steps_per_tile=moe_tm // tm),
        grid=(S, n),
        in_specs=[
            row(D), row(W), row(W), row(W, COL_G), row(W),
            _const_spec(w_out_bf.shape), _const_spec(hw.shape), _const_spec(n2w.shape),
            _const_spec(rw.shape), _const_spec(rb.shape), _const_spec(tril.shape),
        ],
        out_specs=[row(D), pl.BlockSpec((None, tm * TOKEN_SUBLANES, LANE), lambda s, i: (s, i, 0)),
                   pl.BlockSpec((None, ROUTE_ROWS, tm), lambda s, i: (s, 0, i)),
                   row(ROUTER_LANES), row(ROUTER_LANES),
                   pl.BlockSpec((None, None, 8, ROUTER_LANES), lambda s, i: (s, i, 0, 0))],
        out_shape=[
            jax.ShapeDtypeStruct((S, T, D), _f32),
            jax.ShapeDtypeStruct((S, T * TOKEN_SUBLANES, LANE), _f32),
            jax.ShapeDtypeStruct((S, ROUTE_ROWS, T), _f32), lanes, lanes,
            jax.ShapeDtypeStruct((S, n, 8, ROUTER_LANES), _f32),
        ],
        scratch_shapes=[pltpu.VMEM((1, ROUTER_LANES), _f32)],
        compiler_params=_params("parallel", "arbitrary"),
        name="out_proj",
    )(x, o_f, o_b, proj, na, w_out_bf, hw, n2w, rw, rb, tril)


MOE_WINDOWS = (128, 288)
MOE_UNROLL = 16
MOE_STREAM_BLOCK = 256
MOE_SLOTS = 3
MOE_AHEAD = MOE_SLOTS - 1


def _moe_rows(tm):
    return 2 * tm + MOE_WINDOWS[-1]


def _moe_kernel(pos1_ref, pos2_ref, seg_ref, x1_hbm, h2_hbm, w1_ref, w2_ref, wg_ref, wu_ref, wd_ref, y_hbm,
                buf_ref, mix_ref, h2_stage, x1_stage, y_stage, h2_sem, x1_sem, y_sem, *, tm, sb):
    i = pl.program_id(0)
    e = pl.program_id(1)
    TS = TOKEN_SUBLANES
    nsb = tm // sb

    def token_tile(row):
        return pl.ds(pl.multiple_of(row * TS, TS), TS)

    def buffer_tiles(t):
        return tuple(pl.ds(pl.multiple_of(ref[i * tm + t], TS), TS) for ref in (pos1_ref, pos2_ref))

    def token_rows(blk):
        return pl.ds(pl.multiple_of(i * tm + blk * sb, sb), sb)

    def h2_copy(blk, slot, tile=i):
        src = h2_hbm.at[pl.ds(pl.multiple_of((tile * tm + blk * sb) * TS, sb * TS), sb * TS), :]
        return pltpu.make_async_copy(src, h2_stage.at[slot], h2_sem.at[slot])

    def x1_copy(blk, slot):
        return pltpu.make_async_copy(x1_hbm.at[token_rows(blk), :], x1_stage.at[slot], x1_sem.at[slot])

    def y_copy(blk, slot):
        return pltpu.make_async_copy(y_stage.at[slot], y_hbm.at[token_rows(blk), :], y_sem.at[slot])

    @pl.when(e == 0)
    def _():
        slack = pl.ds(2 * tm * TS, MOE_WINDOWS[-1] * TS)
        buf_ref[slack, :] = jnp.zeros((MOE_WINDOWS[-1] * TS, LANE), _f32)

        @pl.when(i == 0)
        def _():
            for blk in range(MOE_AHEAD):
                h2_copy(blk, blk).start()

        def dispatch(blk, carry):
            slot = blk % MOE_SLOTS
            h2_copy(blk, slot).wait()

            @pl.when(blk + MOE_AHEAD < nsb)
            def _():
                h2_copy(blk + MOE_AHEAD, (blk + MOE_AHEAD) % MOE_SLOTS).start()

            def tokens(grp, c2):
                for k in range(MOE_UNROLL):
                    tl = grp * MOE_UNROLL + k
                    p1, p2 = buffer_tiles(blk * sb + tl)
                    tok = h2_stage[slot, token_tile(tl), :]
                    buf_ref[p1, :] = tok
                    buf_ref[p2, :] = tok
                return c2

            lax.fori_loop(0, sb // MOE_UNROLL, tokens, 0)
            return carry

        lax.fori_loop(0, nsb, dispatch, 0)

    start = seg_ref[(i * N_EXPERTS + e) * 2]
    count = seg_ref[(i * N_EXPERTS + e) * 2 + 1]

    def window(rows_w, done):
        r0 = start + done
        chunk = lambda c: pl.ds(r0 * TS + c, rows_w, stride=TS)
        xs = [buf_ref[chunk(c), :] for c in range(TS)]
        xb = jnp.concatenate(xs, axis=-1).astype(_bf16)
        a = _dot(xb, wg_ref[...])
        u = _dot(xb, wu_ref[...])
        hmid = a * (1.0 / (1.0 + jnp.exp(-a))) * u
        out = _dot(hmid.astype(_bf16), wd_ref[...])
        valid = lax.broadcasted_iota(jnp.int32, (rows_w, LANE), 0) + done < count
        for c in range(TS):
            buf_ref[chunk(c), :] = jnp.where(valid, out[:, c * LANE:(c + 1) * LANE], xs[c])

    @pl.when(e == N_EXPERTS - 1)
    def _():
        for blk in range(MOE_AHEAD):
            x1_copy(blk, blk).start()

        @pl.when(i + 1 < pl.num_programs(0))
        def _():
            for blk in range(MOE_AHEAD):
                h2_copy(blk, blk, tile=i + 1).start()

    small, large = MOE_WINDOWS

    n_large = count // large
    tail = count - n_large * large

    def body(j, carry):
        window(large, j * large)
        return carry

    lax.fori_loop(0, n_large, body, 0)

    @pl.when(tail > small)
    def _():
        window(large, n_large * large)

    @pl.when((tail > 0) & (tail <= small))
    def _():
        window(small, n_large * large)

    @pl.when(e == N_EXPERTS - 1)
    def _():
        def combine(blk, carry):
            slot = blk % MOE_SLOTS

            @pl.when(blk + MOE_AHEAD < nsb)
            def _():
                x1_copy(blk + MOE_AHEAD, (blk + MOE_AHEAD) % MOE_SLOTS).start()

            def tokens(grp, c2):
                for k in range(MOE_UNROLL):
                    tl = grp * MOE_UNROLL + k
                    t = blk * sb + tl
                    p1, p2 = buffer_tiles(t)
                    g1 = jnp.broadcast_to(w1_ref[pl.ds(t, 1), :], (TS, LANE))
                    g2 = jnp.broadcast_to(w2_ref[pl.ds(t, 1), :], (TS, LANE))
                    mix_ref[token_tile(tl), :] = g1 * buf_ref[p1, :] + g2 * buf_ref[p2, :]
                return c2

            lax.fori_loop(0, sb // MOE_UNROLL, tokens, 0)
            x1_copy(blk, slot).wait()

            @pl.when((blk >= MOE_SLOTS) | (i > 0))
            def _():
                y_copy(blk, slot).wait()

            for c in range(TS):
                cols = slice(c * LANE, (c + 1) * LANE)
                y_stage[slot, :, cols] = x1_stage[slot, :, cols] + mix_ref[pl.ds(c, sb, stride=TS), :]
            y_copy(blk, slot).start()
            return carry

        lax.fori_loop(0, nsb, combine, 0)

        @pl.when(i + 1 == pl.num_programs(0))
        def _():
            for blk in range(nsb - MOE_SLOTS, nsb):
                y_copy(blk, blk % MOE_SLOTS).wait()


def _moe(pos1, pos2, seg, x1, h2, w1, w2, wg, wu, wd, tm):
    N, D = x1.shape
    sb = min(MOE_STREAM_BLOCK, tm // 4)
    assert tm % sb == 0 and tm // sb >= MOE_SLOTS and sb % MOE_UNROLL == 0
    tile = lambda width: pl.BlockSpec((tm, width), lambda i, e, *_: (i, 0), pipeline_mode=pl.Buffered(1))
    wspec = lambda a, b: pl.BlockSpec((None, a, b), lambda i, e, *_: (e, 0, 0))
    hbm = pl.BlockSpec(memory_space=pl.ANY)
    return pl.pallas_call(
        functools.partial(_moe_kernel, tm=tm, sb=sb),
        grid_spec=pltpu.PrefetchScalarGridSpec(
            num_scalar_prefetch=3,
            grid=(N // tm, N_EXPERTS),
            in_specs=[
                hbm, hbm, tile(ROUTER_LANES), tile(ROUTER_LANES),
                wspec(D, EXPERT_FF), wspec(D, EXPERT_FF), wspec(EXPERT_FF, D),
            ],
            out_specs=hbm,
            scratch_shapes=[
                pltpu.VMEM((_moe_rows(tm) * TOKEN_SUBLANES, LANE), _f32),
                pltpu.VMEM((sb * TOKEN_SUBLANES, LANE), _f32),
                pltpu.VMEM((MOE_SLOTS, sb * TOKEN_SUBLANES, LANE), _f32),
                pltpu.VMEM((MOE_SLOTS, sb, D), _f32),
                pltpu.VMEM((MOE_SLOTS, sb, D), _f32),
                pltpu.SemaphoreType.DMA((MOE_SLOTS,)),
                pltpu.SemaphoreType.DMA((MOE_SLOTS,)),
                pltpu.SemaphoreType.DMA((MOE_SLOTS,)),
            ],
        ),
        out_shape=jax.ShapeDtypeStruct((N, D), _f32),
        compiler_params=_params("arbitrary", "arbitrary"),
        name="moe",
    )(pos1, pos2, seg, x1, h2, w1, w2, wg, wu, wd)


def _moe_plan(route, cnt, tm, steps_per_tile):
    S, _, T = route.shape
    N = S * T
    n_tiles = N // tm
    counts = cnt.reshape(-1, 8, ROUTER_LANES)[steps_per_tile - 1::steps_per_tile, 0,
                                               EXPERT_LANE0:EXPERT_LANE0 + N_EXPERTS].astype(jnp.int32)
    seg_start = jnp.cumsum(counts, axis=1) - counts
    r = route[:, :4, :].astype(jnp.int32).transpose(1, 0, 2).reshape(4, n_tiles, tm)

    experts = jnp.arange(N_EXPERTS, dtype=jnp.int32)[:, None, None]
    starts = seg_start.T[:, :, None]

    def row_of(expert, rank):
        return jnp.sum(jnp.where(expert[None] == experts, starts, 0), axis=0) + rank

    pos1 = row_of(r[0], r[2]) * TOKEN_SUBLANES
    pos2 = row_of(r[1], r[3]) * TOKEN_SUBLANES
    seg = jnp.stack([seg_start, counts], axis=-1)
    return pos1.reshape(-1), pos2.reshape(-1), seg.reshape(-1)


def _lower_bound(logits, l):
    return jnp.cumsum(jax.nn.softmax(logits.astype(_f32), axis=0), axis=0)[l]


IN_TILE = 1024
ROW_TILE = 512
HGRN_BLOCK = 1024
MOE_TILE = 4096


def _layer(x, l, c):
    S, T, D = x.shape
    tm = min(ROW_TILE, T)
    moe_tm = min(MOE_TILE, T)
    proj, gates = _in_proj(x, c["n1w"][l], c["w_in"][l], c["lb"][l], c["qkw"][l], min(IN_TILE, T))
    o_f, o_b = _hgrn2(proj, gates, c["dmat"], c["mask"], min(HGRN_BLOCK, T))
    na = _natten(proj, c["na_bias"][l], c["na_ow"][l])
    x1, h2, route, w1, w2, cnt = _out_proj(x, o_f, o_b, proj, na, c["w_out"][l], c["hw"][l], c["n2w"][l],
                                           c["rw"][l], c["rb"][l], c["tril"][:tm, :tm], tm, moe_tm)
    N = S * T
    flat = lambda a: a.reshape(N, a.shape[-1])
    pos1, pos2, seg = _moe_plan(route, cnt, moe_tm, moe_tm // tm)
    y = _moe(pos1, pos2, seg, flat(x1), h2.reshape(N * TOKEN_SUBLANES, LANE), flat(w1), flat(w2),
             c["wg"][l], c["wu"][l], c["wd"][l], moe_tm)
    return y.reshape(S, T, D)


def _prepare(norm1_w, w_in, lb_fwd_logits, lb_bwd_logits, hgrn_norm_w, q_norm_w, k_norm_w, na_rpb,
             na_out_norm_w, w_out, norm2_w, w_router_group, b_router_group, w_router_expert,
             b_router_expert, w_expert_gate, w_expert_up, w_expert_down):
    depth = w_in.shape[0]
    dmat, mask = _hgrn_constants()
    col_scale = np.ones((1, 1, w_in.shape[-1]), np.float32)
    col_scale[..., 4 * HG_WIDTH:5 * HG_WIDTH] = -LOG2_E
    pad = ROUTER_LANES - N_GROUPS - N_EXPERTS
    rw = jnp.pad(jnp.concatenate([w_router_group, w_router_expert], axis=-1).astype(_f32),
                 ((0, 0), (0, 0), (0, pad)))
    rw_hi = rw.astype(_bf16)
    rw_lo = (rw - rw_hi.astype(_f32)).astype(_bf16)
    rb = jnp.pad(jnp.concatenate([b_router_group, b_router_expert], axis=-1).astype(_f32),
                 ((0, 0), (0, pad)))
    return {
        "n1w": norm1_w.astype(_f32)[:, None, :],
        "w_in": (w_in.astype(_f32) * col_scale).astype(_bf16),
        "lb": jnp.stack([jnp.stack([_lower_bound(lb_fwd_logits, l), _lower_bound(lb_bwd_logits, l)])
                         for l in range(depth)]),
        "qkw": jnp.stack([jnp.tile(q_norm_w.astype(_f32), (1, NA_HEADS)) * (NA_DH ** -0.5 * LOG2_E),
                          jnp.tile(k_norm_w.astype(_f32), (1, NA_HEADS))], axis=1),
        "dmat": jnp.asarray(dmat, _bf16),
        "mask": jnp.asarray(mask, _f32),
        "na_bias": jnp.stack([_natten_bias(na_rpb[l]) for l in range(depth)]),
        "na_ow": na_out_norm_w.astype(_f32)[:, None, :],
        "w_out": w_out.astype(_bf16),
        "hw": hgrn_norm_w.astype(_f32)[:, None, :],
        "n2w": norm2_w.astype(_f32)[:, None, :],
        "rw": jnp.stack([rw_hi, rw_lo], axis=1),
        "rb": rb[:, None, :],
        "tril": jnp.asarray(np.tril(np.ones((ROW_TILE, ROW_TILE), np.float32), -1), _bf16),
        "wg": w_expert_gate.astype(_bf16),
        "wu": w_expert_up.astype(_bf16),
        "wd": w_expert_down.astype(_bf16),
    }


def kernel(x_prompt, x_sample, norm1_w, w_in, lb_fwd_logits, lb_bwd_logits, hgrn_norm_w, q_norm_w, k_norm_w, na_rpb, na_out_norm_w, w_out, norm2_w, w_router_group, b_router_group, w_router_expert, b_router_expert, w_expert_gate, w_expert_up, w_expert_down):
    c = _prepare(norm1_w, w_in, lb_fwd_logits, lb_bwd_logits, hgrn_norm_w, q_norm_w, k_norm_w, na_rpb,
                 na_out_norm_w, w_out, norm2_w, w_router_group, b_router_group, w_router_expert,
                 b_router_expert, w_expert_gate, w_expert_up, w_expert_down)
    outs = []
    for x in (x_prompt, x_sample):
        for l in range(w_in.shape[0]):
            x = _layer(x, l, c)
        outs.append(x)
    return tuple(outs)
```
